```python
import math
import jax
import jax.numpy as jnp
from jax import lax
import numpy as np

D_MODEL = 4096
BATCH = 4
SEQ = 4096
DEPTH = 2

N_A_LAYERS = DEPTH // 2
N_B_LAYERS = DEPTH - N_A_LAYERS
PLE_DIM = 256
RMS_EPS = 1e-6
ROPE_THETA = 500000.0
ROPE_FRACTION = 4
NEG_BIG = -1e30
FORCE_SCORE = 1e9

NSA_HEAD_DIM = 128
NSA_HEADS = D_MODEL // NSA_HEAD_DIM
NSA_KV_GROUPS = 4
NSA_HPG = NSA_HEADS // NSA_KV_GROUPS
NSA_WIDTH = NSA_HEADS * NSA_HEAD_DIM
NSA_KV_WIDTH = NSA_KV_GROUPS * NSA_HEAD_DIM
N_BRANCH = 3
CMP_LEN = 32
CMP_STRIDE = 16
CMP_HIDDEN = 256
SEL_BLOCK = 64
SEL_TOPK = 16
SEL_LOCAL = 2
NSA_WINDOW = 512
NSA_Q_BLOCK = 32
A_SIZES = (NSA_WIDTH,) + (NSA_KV_WIDTH,) * 6 + (N_BRANCH * NSA_HEADS, N_BRANCH * NSA_WIDTH)
A_IN = sum(A_SIZES)

SWA_HEAD_DIM = 64
SWA_HEADS = D_MODEL // SWA_HEAD_DIM
SWA_KV_HEADS = 8
SWA_HPG = SWA_HEADS // SWA_KV_HEADS
SWA_WIDTH = SWA_HEADS * SWA_HEAD_DIM
SWA_KV_WIDTH = SWA_KV_HEADS * SWA_HEAD_DIM
SWA_WINDOW = 128
SWA_Q_BLOCK = 128
B_IN = 2 * SWA_WIDTH

kernel_name = 'hybrid_nsa_yoco_swa_sink'


def rms_norm(x, g):
    xf = x.astype(jnp.float32)
    y = xf * lax.rsqrt(jnp.mean(xf * xf, axis=-1, keepdims=True) + RMS_EPS)
    return (y * g.astype(jnp.float32)).astype(x.dtype)


def rope_partial(t, pos):
    rot_dim = t.shape[-1] // ROPE_FRACTION
    half = rot_dim // 2
    inv_freq = ROPE_THETA ** (-jnp.arange(half, dtype=jnp.float32) / half)
    ang = pos.astype(jnp.float32)[:, None] * inv_freq[None, :]
    cos, sin = jnp.cos(ang), jnp.sin(ang)
    tr = t[..., :rot_dim].astype(jnp.float32)
    t1, t2 = tr[..., :half], tr[..., half:]
    rot = jnp.concatenate([t1 * cos - t2 * sin, t1 * sin + t2 * cos], axis=-1)
    return jnp.concatenate([rot.astype(t.dtype), t[..., rot_dim:]], axis=-1)


def masked_softmax(s, mask):
    return jax.nn.softmax(jnp.where(mask, s, NEG_BIG), axis=-1)


def split_cols(t, sizes):
    return jnp.split(t, [int(v) for v in np.cumsum(sizes)[:-1]], axis=-1)


def compress_blocks(blocks, pos_emb, w1, w2):
    b, g, n, l, d = blocks.shape
    flat = (blocks + pos_emb).reshape(b, g, n, l * d)
    return jax.nn.silu(flat @ w1) @ w2


def nsa_mixer(hn, pos, w_in, w_out, pos_k, w1_k, w2_k, pos_v, w1_v, w2_v):
    B, S, _ = hn.shape
    G, HPG, DH, H = NSA_KV_GROUPS, NSA_HPG, NSA_HEAD_DIM, NSA_HEADS
    QB, W = NSA_Q_BLOCK, NSA_WINDOW
    scale = 1.0 / math.sqrt(DH)
    q, k_c, v_c, k_s, v_s, k_w, v_w, g_logit, z = split_cols(hn @ w_in, A_SIZES)
    q = q.reshape(B, S, G, HPG, DH).transpose(0, 2, 3, 1, 4)
    q_rot = rope_partial(q, pos)

    def heads(t):
        return t.reshape(B, S, G, DH).transpose(0, 2, 1, 3)

    k_c, v_c, k_s, v_s, k_w, v_w = (heads(t) for t in (k_c, v_c, k_s, v_s, k_w, v_w))
    k_s = rope_partial(k_s, pos)
    k_w = rope_partial(k_w, pos)
    gates = jax.nn.sigmoid(g_logit.astype(jnp.float32)).reshape(B, S, N_BRANCH, H)
    z = z.reshape(B, S, N_BRANCH, H, DH)

    n_cmp = (S - CMP_LEN) // CMP_STRIDE + 1
    cmp_idx = np.arange(n_cmp)[:, None] * CMP_STRIDE + np.arange(CMP_LEN)[None, :]
    k_cmp = compress_blocks(k_c[:, :, cmp_idx], pos_k, w1_k, w2_k)
    v_cmp = compress_blocks(v_c[:, :, cmp_idx], pos_v, w1_v, w2_v)
    cmp_last = jnp.asarray(cmp_idx[:, -1], dtype=jnp.int32)

    n_sel = S // SEL_BLOCK
    n_top = min(SEL_TOPK, n_sel)
    c0 = np.arange(n_cmp) * CMP_STRIDE
    s0 = np.arange(n_sel) * SEL_BLOCK
    cmp_to_sel = jnp.asarray((c0[:, None] < s0[None, :] + SEL_BLOCK) & (c0[:, None] + CMP_LEN > s0[None, :]), dtype=jnp.float32)
    k_blk = k_s.reshape(B, G, n_sel, SEL_BLOCK, DH)
    v_blk = v_s.reshape(B, G, n_sel, SEL_BLOCK, DH)
    blk_ids = jnp.arange(n_sel, dtype=jnp.int32)
    b_ix = jnp.arange(B)[:, None, None, None]
    g_ix = jnp.arange(G)[None, :, None, None]

    k_wpad = jnp.pad(k_w, ((0, 0), (0, 0), (W, 0), (0, 0)))
    v_wpad = jnp.pad(v_w, ((0, 0), (0, 0), (W, 0), (0, 0)))

    def chunk(c):
        t0 = c * QB
        tq = t0 + jnp.arange(QB, dtype=jnp.int32)
        q_c = lax.dynamic_slice_in_dim(q, t0, QB, axis=3)
        q_r = lax.dynamic_slice_in_dim(q_rot, t0, QB, axis=3)
        valid_c = cmp_last[None, :] <= tq[:, None]
        s_c = jnp.einsum('bghtd,bgnd->bghtn', q_c, k_cmp).astype(jnp.float32) * scale
        p_c = jnp.where(valid_c, masked_softmax(s_c, valid_c), 0.0)
        o_c = jnp.einsum('bghtn,bgnd->bghtd', p_c.astype(v_cmp.dtype), v_cmp)
        imp = jnp.einsum('bghtn,nj->bgtj', p_c, cmp_to_sel)
        dist = (tq // SEL_BLOCK)[:, None] - blk_ids[None, :]
        forced = (blk_ids[None, :] == 0) | ((dist >= 0) & (dist < SEL_LOCAL))
        imp = jnp.where(forced, FORCE_SCORE, imp)
        imp = jnp.where(dist >= 0, imp, -1.0)
        _, sel = lax.top_k(imp, n_top)
        k_sel = k_blk[b_ix, g_ix, sel].reshape(B, G, QB, n_top * SEL_BLOCK, DH)
        v_sel = v_blk[b_ix, g_ix, sel].reshape(B, G, QB, n_top * SEL_BLOCK, DH)
        kpos = (sel[..., None] * SEL_BLOCK + jnp.arange(SEL_BLOCK, dtype=jnp.int32)).reshape(B, G, QB, n_top * SEL_BLOCK)
        valid_s = (kpos <= tq[None, None, :, None])[:, :, None]
        s_s = jnp.einsum('bghtd,bgtkd->bghtk', q_r, k_sel).astype(jnp.float32) * scale
        p_s = masked_softmax(s_s, valid_s)
        o_s = jnp.einsum('bghtk,bgtkd->bghtd', p_s.astype(v_sel.dtype), v_sel)
        kp = t0 - W + jnp.arange(QB + W, dtype=jnp.int32)
        k_win = lax.dynamic_slice_in_dim(k_wpad, t0, QB + W, axis=2)
        v_win = lax.dynamic_slice_in_dim(v_wpad, t0, QB + W, axis=2)
        valid_w = (kp[None, :] <= tq[:, None]) & (kp[None, :] > tq[:, None] - W) & (kp[None, :] >= 0)
        s_w = jnp.einsum('bghtd,bgkd->bghtk', q_r, k_win).astype(jnp.float32) * scale
        p_w = masked_softmax(s_w, valid_w)
        o_w = jnp.einsum('bghtk,bgkd->bghtd', p_w.astype(v_win.dtype), v_win)
        branch = jnp.stack([o_c, o_s, o_w], axis=2)
        branch = branch.transpose(0, 4, 2, 1, 3, 5).reshape(B, QB, N_BRANCH, H, DH)
        g_c = lax.dynamic_slice_in_dim(gates, t0, QB, axis=1)
        z_c = lax.dynamic_slice_in_dim(z, t0, QB, axis=1)
        mixed = jnp.sum(g_c[..., None] * branch.astype(jnp.float32) * jax.nn.silu(z_c.astype(jnp.float32)), axis=2)
        return mixed.astype(hn.dtype).reshape(B, QB, NSA_WIDTH)

    o = lax.map(chunk, jnp.arange(S // QB))
    o = o.transpose(1, 0, 2, 3).reshape(B, S, NSA_WIDTH)
    return o @ w_out


def shared_kv(h, pos, kv_norm, w_kv):
    B, S, _ = h.shape
    k, v = split_cols(rms_norm(h, kv_norm) @ w_kv, (SWA_KV_WIDTH, SWA_KV_WIDTH))
    k = k.reshape(B, S, SWA_KV_HEADS, SWA_HEAD_DIM).transpose(0, 2, 1, 3)
    v = v.reshape(B, S, SWA_KV_HEADS, SWA_HEAD_DIM).transpose(0, 2, 1, 3)
    return rope_partial(k, pos), v


def swa_sink_mixer(hn, pos, w_in, w_out, sinks, k_sh, v_sh):
    B, S, _ = hn.shape
    G, HPG, DH, QB = SWA_KV_HEADS, SWA_HPG, SWA_HEAD_DIM, SWA_Q_BLOCK
    scale = 1.0 / math.sqrt(DH)
    q, z = split_cols(hn @ w_in, (SWA_WIDTH, SWA_WIDTH))
    q = rope_partial(q.reshape(B, S, G, HPG, DH).transpose(0, 2, 3, 1, 4), pos)
    k_pad = jnp.pad(k_sh, ((0, 0), (0, 0), (QB, 0), (0, 0)))
    v_pad = jnp.pad(v_sh, ((0, 0), (0, 0), (QB, 0), (0, 0)))
    sink = sinks.astype(jnp.float32).reshape(1, G, HPG, 1, 1)

    def block(c):
        t0 = c * QB
        tq = t0 + jnp.arange(QB, dtype=jnp.int32)
        kp = t0 - QB + jnp.arange(2 * QB, dtype=jnp.int32)
        qb = lax.dynamic_slice_in_dim(q, t0, QB, axis=3)
        kb = lax.dynamic_slice_in_dim(k_pad, t0, 2 * QB, axis=2)
        vb = lax.dynamic_slice_in_dim(v_pad, t0, 2 * QB, axis=2)
        s = jnp.einsum('bghtd,bgkd->bghtk', qb, kb).astype(jnp.float32) * scale
        mask = (kp[None, :] <= tq[:, None]) & (kp[None, :] > tq[:, None] - SWA_WINDOW) & (kp[None, :] >= 0)
        s = jnp.where(mask, s, NEG_BIG)
        m = jnp.maximum(jnp.max(s, axis=-1, keepdims=True), sink)
        e = jnp.exp(s - m)
        prob = e / (jnp.sum(e, axis=-1, keepdims=True) + jnp.exp(sink - m))
        o = jnp.einsum('bghtk,bgkd->bghtd', prob.astype(vb.dtype), vb)
        return o.transpose(0, 3, 1, 2, 4).reshape(B, QB, SWA_WIDTH)

    o = lax.map(block, jnp.arange(S // QB)).transpose(1, 0, 2, 3).reshape(B, S, SWA_WIDTH)
    return (o * jax.nn.silu(z)) @ w_out


def setup_inputs(seed: int = 0) -> dict:
    key = jax.random.key(seed)
    ks = jax.random.split(key, 22)
    f32 = jnp.float32

    def nrm(k, shape, scale):
        return jax.random.normal(k, shape, f32) * scale

    def gain(k, shape):
        return 1.0 + 0.05 * jax.random.normal(k, shape, f32)

    cin = CMP_LEN * NSA_HEAD_DIM
    return {
        'x': nrm(ks[0], (BATCH, SEQ, D_MODEL), 1.0),
        'p': nrm(ks[1], (DEPTH, BATCH, SEQ, PLE_DIM), 1.0),
        'a_norm': gain(ks[2], (N_A_LAYERS, D_MODEL)),
        'a_w_in': nrm(ks[3], (N_A_LAYERS, D_MODEL, A_IN), D_MODEL ** -0.5),
        'a_w_out': nrm(ks[4], (N_A_LAYERS, NSA_WIDTH, D_MODEL), NSA_WIDTH ** -0.5),
        'a_cmp_pos_k': nrm(ks[5], (N_A_LAYERS, CMP_LEN, NSA_HEAD_DIM), 0.1),
        'a_cmp_w1_k': nrm(ks[6], (N_A_LAYERS, cin, CMP_HIDDEN), cin ** -0.5),
        'a_cmp_w2_k': nrm(ks[7], (N_A_LAYERS, CMP_HIDDEN, NSA_HEAD_DIM), CMP_HIDDEN ** -0.5),
        'a_cmp_pos_v': nrm(ks[8], (N_A_LAYERS, CMP_LEN, NSA_HEAD_DIM), 0.1),
        'a_cmp_w1_v': nrm(ks[9], (N_A_LAYERS, cin, CMP_HIDDEN), cin ** -0.5),
        'a_cmp_w2_v': nrm(ks[10], (N_A_LAYERS, CMP_HIDDEN, NSA_HEAD_DIM), CMP_HIDDEN ** -0.5),
        'kv_norm': gain(ks[11], (D_MODEL,)),
        'w_kv': nrm(ks[12], (D_MODEL, 2 * SWA_KV_WIDTH), D_MODEL ** -0.5),
        'b_norm': gain(ks[13], (N_B_LAYERS, D_MODEL)),
        'b_w_in': nrm(ks[14], (N_B_LAYERS, D_MODEL, B_IN), D_MODEL ** -0.5),
        'b_w_out': nrm(ks[15], (N_B_LAYERS, SWA_WIDTH, D_MODEL), SWA_WIDTH ** -0.5),
        'b_sinks': nrm(ks[16], (N_B_LAYERS, SWA_HEADS), 0.5),
        'ple_norm': gain(ks[17], (DEPTH, D_MODEL)),
        'ple_gate_w': nrm(ks[18], (DEPTH, D_MODEL, D_MODEL), D_MODEL ** -0.5),
        'ple_proj': nrm(ks[19], (DEPTH, PLE_DIM, D_MODEL), PLE_DIM ** -0.5),
        'final_norm': gain(ks[20], (D_MODEL,)),
    }


def reference(x, p, a_norm, a_w_in, a_w_out, a_cmp_pos_k, a_cmp_w1_k, a_cmp_w2_k, a_cmp_pos_v, a_cmp_w1_v, a_cmp_w2_v, kv_norm, w_kv, b_norm, b_w_in, b_w_out, b_sinks, ple_norm, ple_gate_w, ple_proj, final_norm):
    S = x.shape[1]
    pos = jnp.arange(S, dtype=jnp.int32)
    h = x
    k_sh = None
    v_sh = None
    for i in range(DEPTH):
        if i < N_A_LAYERS:
            a = i
            h = h + nsa_mixer(rms_norm(h, a_norm[a]), pos, a_w_in[a], a_w_out[a], a_cmp_pos_k[a], a_cmp_w1_k[a], a_cmp_w2_k[a], a_cmp_pos_v[a], a_cmp_w1_v[a], a_cmp_w2_v[a])
        else:
            if i == N_A_LAYERS:
                k_sh, v_sh = shared_kv(h, pos, kv_norm, w_kv)
            b = i - N_A_LAYERS
            h = h + swa_sink_mixer(rms_norm(h, b_norm[b]), pos, b_w_in[b], b_w_out[b], b_sinks[b], k_sh, v_sh)
        gate = jax.nn.sigmoid(rms_norm(h, ple_norm[i]) @ ple_gate_w[i])
        h = h + gate * (p[i] @ ple_proj[i])
    return rms_norm(h, final_norm)
```

```python
import functools
import math

import numpy as np
import jax
import jax.numpy as jnp
from jax import lax
from jax.experimental import pallas as pl
from jax.experimental.pallas import tpu as pltpu

F32 = jnp.float32
BF16 = jnp.bfloat16

D_MODEL = 4096
RMS_EPS = 1e-6
ROPE_THETA = 500000.0
NEG_BIG = -1e30
FORCE_SCORE = 1e9
PLE_DIM = 256

NSA_DH = 128
NSA_G = 4
NSA_HPG = 8
NSA_WIDTH = 4096
NSA_KVW = 512
CMP_LEN = 32
CMP_STRIDE = 16
CMP_HIDDEN = 256
SEL_BLOCK = 64
SEL_TOPK = 16
SEL_LOCAL = 2
NSA_WINDOW = 512

SWA_DH = 64
SWA_KVH = 8
SWA_HPG = 8
SWA_WINDOW = 128

LANES = 128
VMEM_LIMIT_BYTES = 58 * 1024 * 1024

MM_TM = 1024
MM_TN = 1024
MM_TN_FUSED = 512
NORM_ROWS = 256
PREP_ROWS = 512
ATT_TQ = 128
NSA_TK = 512


def _cparams(sem):
    return pltpu.CompilerParams(dimension_semantics=sem, vmem_limit_bytes=VMEM_LIMIT_BYTES)


def _sigmoid(v):
    return 1.0 / (1.0 + jnp.exp(-v))


def _silu(v):
    return v * _sigmoid(v)


def _dot(a, b):
    return jnp.dot(a, b, preferred_element_type=F32)


def _dot_nt(a, b):
    return lax.dot_general(a, b, (((1,), (1,)), ((), ())), preferred_element_type=F32)


def _norm_kernel(h_ref, g_ref, *o_refs):
    x = h_ref[...]
    r = lax.rsqrt(jnp.mean(x * x, axis=-1, keepdims=True) + RMS_EPS)
    y = x * r
    for i, o_ref in enumerate(o_refs):
        o_ref[...] = (y * g_ref[i:i + 1, :]).astype(o_ref.dtype)


def rms_norm_rows(h, gains, out_dtype):
    t, d = h.shape
    k = gains.shape[0]
    outs = pl.pallas_call(
        _norm_kernel,
        grid=(t // NORM_ROWS,),
        in_specs=[pl.BlockSpec((NORM_ROWS, d), lambda i: (i, 0)),
                  pl.BlockSpec((k, d), lambda i: (0, 0))],
        out_specs=[pl.BlockSpec((NORM_ROWS, d), lambda i: (i, 0))] * k,
        out_shape=[jax.ShapeDtypeStruct((t, d), out_dtype)] * k,
        compiler_params=_cparams(("arbitrary",)),
        name="rms_norm",
    )(h, gains)
    return outs


def _mm_kernel(x_ref, w_ref, o_ref):
    o_ref[...] = _dot(x_ref[...], w_ref[...]).astype(o_ref.dtype)


def _mm_res_kernel(x_ref, w_ref, r_ref, o_ref):
    o_ref[...] = r_ref[...] + _dot(x_ref[...], w_ref[...])


def _ple_kernel(n_ref, wg_ref, p_ref, wp_ref, h_ref, o_ref):
    gate = _sigmoid(_dot(n_ref[...], wg_ref[...]))
    emb = _dot(p_ref[...].astype(BF16), wp_ref[...])
    o_ref[...] = h_ref[...] + gate * emb


def _mm_tiles(m, n, tn_max=MM_TN):
    tm = min(MM_TM, m)
    tn = min(tn_max, n)
    assert m % tm == 0 and n % tn == 0, (m, n)
    return tm, tn


def matmul(x, w, name):
    m, k = x.shape
    n = w.shape[1]
    tm, tn = _mm_tiles(m, n)
    return pl.pallas_call(
        _mm_kernel,
        grid=(m // tm, n // tn),
        in_specs=[pl.BlockSpec((tm, k), lambda i, j: (i, 0)),
                  pl.BlockSpec((k, tn), lambda i, j: (0, j))],
        out_specs=pl.BlockSpec((tm, tn), lambda i, j: (i, j)),
        out_shape=jax.ShapeDtypeStruct((m, n), F32),
        compiler_params=_cparams(("arbitrary", "arbitrary")),
        name=name,
    )(x, w)


def matmul_residual(x, w, res, name):
    m, k = x.shape
    n = w.shape[1]
    tm, tn = _mm_tiles(m, n, MM_TN_FUSED)
    return pl.pallas_call(
        _mm_res_kernel,
        grid=(m // tm, n // tn),
        in_specs=[pl.BlockSpec((tm, k), lambda i, j: (i, 0)),
                  pl.BlockSpec((k, tn), lambda i, j: (0, j)),
                  pl.BlockSpec((tm, tn), lambda i, j: (i, j))],
        out_specs=pl.BlockSpec((tm, tn), lambda i, j: (i, j)),
        out_shape=jax.ShapeDtypeStruct((m, n), F32),
        compiler_params=_cparams(("arbitrary", "arbitrary")),
        name=name,
    )(x, w, res)


def ple_update(n, wg, p, wp, h, name):
    m, k = n.shape
    d = wg.shape[1]
    kp = p.shape[1]
    tm, tn = _mm_tiles(m, d, MM_TN_FUSED)
    return pl.pallas_call(
        _ple_kernel,
        grid=(m // tm, d // tn),
        in_specs=[pl.BlockSpec((tm, k), lambda i, j: (i, 0)),
                  pl.BlockSpec((k, tn), lambda i, j: (0, j)),
                  pl.BlockSpec((tm, kp), lambda i, j: (i, 0)),
                  pl.BlockSpec((kp, tn), lambda i, j: (0, j)),
                  pl.BlockSpec((tm, tn), lambda i, j: (i, j))],
        out_specs=pl.BlockSpec((tm, tn), lambda i, j: (i, j)),
        out_shape=jax.ShapeDtypeStruct((m, d), F32),
        compiler_params=_cparams(("arbitrary", "arbitrary")),
        name=name,
    )(n, wg, p, wp, h)


def _rope_tables(seq, head_dim):
    rot = head_dim // 4
    half = rot // 2
    inv_freq = ROPE_THETA ** (-jnp.arange(half, dtype=F32) / half)
    ang = jnp.arange(seq, dtype=jnp.int32).astype(F32)[:, None] * inv_freq[None, :]
    cos, sin = jnp.cos(ang), jnp.sin(ang)
    rest = head_dim - rot
    zeros_h = jnp.zeros((seq, half), F32)
    c = jnp.concatenate([cos, cos, jnp.ones((seq, rest), F32)], axis=1)
    s1 = jnp.concatenate([-sin, zeros_h, jnp.zeros((seq, rest), F32)], axis=1)
    s2 = jnp.concatenate([zeros_h, sin, jnp.zeros((seq, rest), F32)], axis=1)
    reps = LANES // head_dim
    return tuple(jnp.tile(a, (1, reps)) for a in (c, s1, s2))


def _rope_lanes(t, c, s1, s2, half):
    return t * c + pltpu.roll(t, LANES - half, 1) * s1 + pltpu.roll(t, half, 1) * s2


def _prep_a_kernel(ks_ref, vs_ref, kw_ref, vw_ref, c_ref, s1_ref, s2_ref, o_ref):
    c, s1, s2 = c_ref[...], s1_ref[...], s2_ref[...]
    half = NSA_DH // 8
    for slot, (src, roped) in enumerate(((ks_ref, True), (vs_ref, False), (kw_ref, True), (vw_ref, False))):
        for g in range(NSA_G):
            t = src[:, g * LANES:(g + 1) * LANES]
            if roped:
                t = _rope_lanes(t, c, s1, s2, half)
            col = slot * NSA_KVW + g * LANES
            o_ref[:, col:col + LANES] = t.astype(BF16)


def prep_kv_a(qa, tabs, seq):
    t = qa.shape[0]
    rows = min(PREP_ROWS, seq)
    per_seq = seq // rows
    first = (NSA_WIDTH + 2 * NSA_KVW) // NSA_KVW
    kv_specs = [pl.BlockSpec((rows, NSA_KVW), functools.partial(lambda i, c: (i, c), c=first + s))
                for s in range(4)]
    tab_spec = pl.BlockSpec((rows, LANES), lambda i: (i % per_seq, 0))
    return pl.pallas_call(
        _prep_a_kernel,
        grid=(t // rows,),
        in_specs=kv_specs + [tab_spec] * 3,
        out_specs=pl.BlockSpec((rows, 4 * NSA_KVW), lambda i: (i, 0)),
        out_shape=jax.ShapeDtypeStruct((t, 4 * NSA_KVW), BF16),
        compiler_params=_cparams(("arbitrary",)),
        name="prep_kv_a",
    )(qa, qa, qa, qa, *tabs)


def _compress_kernel(x_ref, pos_ref, w1_ref, w2_ref, o_ref, *, n_cmp_pad):
    nc = n_cmp_pad
    lo = jnp.zeros((nc, CMP_HIDDEN), F32)
    hi = jnp.zeros((nc, CMP_HIDDEN), F32)
    for l in range(CMP_STRIDE):
        xl = x_ref[pl.ds(l, nc, stride=CMP_STRIDE), :]
        a = (xl + pos_ref[0, l:l + 1, :]).astype(BF16)
        b = (xl + pos_ref[0, CMP_STRIDE + l:CMP_STRIDE + l + 1, :]).astype(BF16)
        lo = lo + _dot(a, w1_ref[0, l * LANES:(l + 1) * LANES, :])
        hi = hi + _dot(b, w1_ref[0, (CMP_STRIDE + l) * LANES:(CMP_STRIDE + l + 1) * LANES, :])
    hid = lo + pltpu.roll(hi, nc - 1, 0)
    out = _dot(_silu(hid).astype(BF16), w2_ref[0])
    n_idx = lax.broadcasted_iota(jnp.int32, (nc, LANES), 0)
    o_ref[0, 0, 0] = jnp.where(n_idx < nc - 1, out, 0.0).astype(BF16)


def compress_kv(qa, pos, w1, w2, batch, seq):
    nc = seq // CMP_STRIDE
    first = NSA_WIDTH // LANES
    return pl.pallas_call(
        functools.partial(_compress_kernel, n_cmp_pad=nc),
        grid=(batch, 2, NSA_G),
        in_specs=[pl.BlockSpec((seq, LANES), lambda b, s, g: (b, first + s * NSA_G + g)),
                  pl.BlockSpec((1, CMP_LEN, LANES), lambda b, s, g: (s, 0, 0)),
                  pl.BlockSpec((1, CMP_LEN * LANES, CMP_HIDDEN), lambda b, s, g: (s, 0, 0)),
                  pl.BlockSpec((1, CMP_HIDDEN, LANES), lambda b, s, g: (s, 0, 0))],
        out_specs=pl.BlockSpec((1, 1, 1, nc, LANES), lambda b, s, g: (b, s, g, 0, 0)),
        out_shape=jax.ShapeDtypeStruct((batch, 2, NSA_G, nc, LANES), BF16),
        compiler_params=_cparams(("arbitrary", "arbitrary", "arbitrary")),
        name="compress_kv",
    )(qa, pos, w1, w2)


def _stack_heads(x, n):
    return jnp.concatenate([x[:, h * LANES:(h + 1) * LANES] for h in range(n)], axis=0)


def _tile_rows(x, n):
    return jnp.concatenate([x] * n, axis=0)


def _nsa_kernel(q_ref, z0_ref, z1_ref, z2_ref, gl_ref, c_ref, s1_ref, s2_ref,
                kc_ref, vc_ref, ks_ref, vs_ref, kw_ref, vw_ref, e_ref, mt_ref,
                o_ref, m_scr, l_scr, acc_scr, *, seq):
    tq, tk, hpg = ATT_TQ, NSA_TK, NSA_HPG
    rows = hpg * tq
    nc = seq // CMP_STRIDE
    ns = seq // SEL_BLOCK
    n_top = min(SEL_TOPK, ns)
    nsp = e_ref.shape[0]
    scale = 1.0 / math.sqrt(NSA_DH)
    t0 = pl.program_id(2) * tq

    q = _stack_heads(q_ref[...], hpg)
    c8, s18, s28 = (_tile_rows(r[...], hpg) for r in (c_ref, s1_ref, s2_ref))
    q_rot = _rope_lanes(q, c8, s18, s28, NSA_DH // 8).astype(BF16)
    q_plain = q.astype(BF16)

    t_col = t0 + lax.broadcasted_iota(jnp.int32, (tq, 1), 0)
    t_rows = _tile_rows(t_col, hpg)

    gate = _sigmoid(gl_ref[...])

    def gate_rows(br):
        return jnp.concatenate([gate[:, br * hpg + h:br * hpg + h + 1] for h in range(hpg)], axis=0)

    s = _dot_nt(q_plain, kc_ref[0, 0, 0]) * scale
    n_idx = lax.broadcasted_iota(jnp.int32, (1, nc), 1)
    valid = (n_idx * CMP_STRIDE + (CMP_LEN - 1) <= t_rows) & (n_idx < nc - 1)
    s = jnp.where(valid, s, NEG_BIG)
    m = jnp.max(s, axis=-1, keepdims=True)
    p = jnp.exp(s - m)
    l = jnp.sum(p, axis=-1, keepdims=True)
    p = (jnp.where(valid, p, 0.0) * (1.0 / l)).astype(BF16)
    o_cmp = _dot(p, vc_ref[0, 0, 0]) * gate_rows(0)

    imp_h = _dot_nt(mt_ref[...], p)
    imp = imp_h[:, 0:tq]
    for h in range(1, hpg):
        imp = imp + imp_h[:, h * tq:(h + 1) * tq]
    j_io = lax.broadcasted_iota(jnp.int32, (nsp, tq), 0)
    t_io = t0 + lax.broadcasted_iota(jnp.int32, (nsp, tq), 1)
    dist = lax.shift_right_logical(t_io, int(math.log2(SEL_BLOCK))) - j_io
    forced = (j_io == 0) | ((dist >= 0) & (dist < SEL_LOCAL))
    imp = jnp.where(forced, FORCE_SCORE, imp)
    imp = jnp.where(dist >= 0, imp, -1.0)
    rank = jnp.zeros((nsp, tq), F32)
    for k in range(ns):
        rk = imp[k:k + 1, :]
        ahead = (rk > imp) | ((rk == imp) & (j_io > k))
        rank = rank + jnp.where(ahead, 1.0, 0.0)
    sel_t = jnp.where((rank < n_top) & (j_io < ns), 1.0, 0.0)
    sel = sel_t.T.astype(BF16)

    m_scr[...] = jnp.full((rows, 1), NEG_BIG, F32)
    l_scr[...] = jnp.zeros((rows, 1), F32)
    acc_scr[...] = jnp.zeros((rows, LANES), F32)
    n_kt = lax.shift_right_logical(t0 + (tq + tk - 1), int(math.log2(tk)))

    def slc_step(kt, carry):
        k0 = pl.multiple_of(kt * tk, tk)
        kb = ks_ref[pl.ds(k0, tk), :]
        vb = vs_ref[pl.ds(k0, tk), :]
        sc = _dot_nt(q_rot, kb) * scale
        picked = _dot(sel, e_ref[:, pl.ds(k0, tk)])
        kpos = k0 + lax.broadcasted_iota(jnp.int32, (1, tk), 1)
        ok = (picked > 0.5) & (kpos <= t_col)
        bias = jnp.where(ok, 0.0, NEG_BIG)
        sc = sc + _tile_rows(bias, hpg)
        m_old = m_scr[...]
        m_new = jnp.maximum(m_old, jnp.max(sc, axis=-1, keepdims=True))
        alpha = jnp.exp(m_old - m_new)
        pe = jnp.exp(sc - m_new)
        l_scr[...] = alpha * l_scr[...] + jnp.sum(pe, axis=-1, keepdims=True)
        acc_scr[...] = alpha * acc_scr[...] + _dot(pe.astype(BF16), vb)
        m_scr[...] = m_new
        return carry

    lax.fori_loop(0, n_kt, slc_step, 0)
    o_slc = acc_scr[...] * (gate_rows(1) * (1.0 / l_scr[...]))

    span = NSA_WINDOW + tq
    k0 = pl.multiple_of(jnp.maximum(t0 - NSA_WINDOW, 0), tq)
    kb = kw_ref[pl.ds(k0, span), :]
    vb = vw_ref[pl.ds(k0, span), :]
    sc = _dot_nt(q_rot, kb) * scale
    kpos = k0 + lax.broadcasted_iota(jnp.int32, (1, span), 1)
    ok = (kpos <= t_col) & (kpos > t_col - NSA_WINDOW)
    sc = sc + _tile_rows(jnp.where(ok, 0.0, NEG_BIG), hpg)
    m = jnp.max(sc, axis=-1, keepdims=True)
    pe = jnp.exp(sc - m)
    l = jnp.sum(pe, axis=-1, keepdims=True)
    o_win = _dot(pe.astype(BF16), vb) * (gate_rows(2) * (1.0 / l))

    for h in range(hpg):
        r0, r1 = h * tq, (h + 1) * tq
        c0, c1 = h * LANES, (h + 1) * LANES
        mixed = (o_cmp[r0:r1] * _silu(z0_ref[:, c0:c1])
                 + o_slc[r0:r1] * _silu(z1_ref[:, c0:c1])
                 + o_win[r0:r1] * _silu(z2_ref[:, c0:c1]))
        o_ref[:, c0:c1] = mixed.astype(BF16)


def nsa_attention(qa, gl, cmp_kv, kv, tabs, expand, cmp_to_sel_t, batch, seq):
    t = qa.shape[0]
    tq = ATT_TQ
    assert seq % NSA_TK == 0 and seq >= NSA_WINDOW + tq
    qt_per_seq = seq // tq
    nc = seq // CMP_STRIDE
    hw = NSA_HPG * NSA_DH
    z_first = (NSA_WIDTH + 6 * NSA_KVW) // hw

    def row(b, g, i):
        return b * qt_per_seq + i

    q_spec = pl.BlockSpec((tq, hw), lambda b, g, i: (row(b, g, i), g))
    z_specs = [pl.BlockSpec((tq, hw), functools.partial(
        lambda b, g, i, br: (row(b, g, i), z_first + br * NSA_G + g), br=br)) for br in range(3)]
    gl_spec = pl.BlockSpec((tq, LANES), lambda b, g, i: (row(b, g, i), g))
    tab_spec = pl.BlockSpec((tq, LANES), lambda b, g, i: (i, 0))
    cmp_specs = [pl.BlockSpec((1, 1, 1, nc, LANES), functools.partial(
        lambda b, g, i, s: (b, s, g, 0, 0), s=s)) for s in range(2)]
    kv_specs = [pl.BlockSpec((seq, LANES), functools.partial(
        lambda b, g, i, s: (b, s * NSA_G + g), s=s)) for s in range(4)]
    const_specs = [pl.BlockSpec(expand.shape, lambda b, g, i: (0, 0)),
                   pl.BlockSpec(cmp_to_sel_t.shape, lambda b, g, i: (0, 0))]
    rows = NSA_HPG * tq
    return pl.pallas_call(
        functools.partial(_nsa_kernel, seq=seq),
        grid=(batch, NSA_G, qt_per_seq),
        in_specs=[q_spec] + z_specs + [gl_spec] + [tab_spec] * 3 + cmp_specs + kv_specs + const_specs,
        out_specs=pl.BlockSpec((tq, hw), lambda b, g, i: (row(b, g, i), g)),
        out_shape=jax.ShapeDtypeStruct((t, NSA_WIDTH), BF16),
        scratch_shapes=[pltpu.VMEM((rows, 1), F32), pltpu.VMEM((rows, 1), F32),
                        pltpu.VMEM((rows, LANES), F32)],
        compiler_params=_cparams(("arbitrary", "arbitrary", "arbitrary")),
        name="nsa_attention",
    )(qa, qa, qa, qa, gl, *tabs, cmp_kv, cmp_kv, kv, kv, kv, kv, expand, cmp_to_sel_t)


def _swa_kernel(sink_ref, q_ref, z_ref, kp_ref, kc_ref, vp_ref, vc_ref,
                cq_ref, s1q_ref, s2q_ref, ck_ref, s1k_ref, s2k_ref, o_ref):
    tq = ATT_TQ
    half = SWA_DH // 8
    n_chunk = 2 * SWA_HPG * SWA_DH // LANES
    per_kv = n_chunk // 2
    scale = 1.0 / math.sqrt(SWA_DH)
    pair = pl.program_id(1)
    i = pl.program_id(2)
    t0 = i * tq

    cq, s1q, s2q = cq_ref[...], s1q_ref[...], s2q_ref[...]
    q = q_ref[...]
    chunks = [_rope_lanes(q[:, c * LANES:(c + 1) * LANES], cq, s1q, s2q, half).astype(BF16)
              for c in range(n_chunk)]
    q_kv = [jnp.concatenate(chunks[j * per_kv:(j + 1) * per_kv], axis=0) for j in range(2)]

    k_prev = _rope_lanes(kp_ref[...], ck_ref[0], s1k_ref[0], s2k_ref[0], half)
    k_cur = _rope_lanes(kc_ref[...], ck_ref[1], s1k_ref[1], s2k_ref[1], half)
    k2 = jnp.concatenate([k_prev, k_cur], axis=0)
    v2 = jnp.concatenate([vp_ref[...], vc_ref[...]], axis=0)
    low = lax.broadcasted_iota(jnp.int32, (2 * tq, LANES), 1) < SWA_DH
    k2s = pltpu.roll(k2, SWA_DH, 1)
    v2s = pltpu.roll(v2, SWA_DH, 1)
    zero = jnp.zeros_like(k2)
    k_lo = [jnp.where(low, k2, zero).astype(BF16), jnp.where(low, k2s, zero).astype(BF16)]
    k_hi = [jnp.where(low, zero, k2s).astype(BF16), jnp.where(low, zero, k2).astype(BF16)]
    v_lo = [jnp.where(low, v2, zero).astype(BF16), jnp.where(low, v2s, zero).astype(BF16)]
    v_hi = [jnp.where(low, zero, v2s).astype(BF16), jnp.where(low, zero, v2).astype(BF16)]

    t_col = t0 + lax.broadcasted_iota(jnp.int32, (tq, 1), 0)
    kpos = t0 - tq + lax.broadcasted_iota(jnp.int32, (1, 2 * tq), 1)
    ok = (kpos <= t_col) & (kpos > t_col - SWA_WINDOW) & (kpos >= 0)
    bias = _tile_rows(jnp.where(ok, 0.0, NEG_BIG), per_kv)

    for j in range(2):
        outs = []
        for parity, (kk, vv) in enumerate(((k_lo[j], v_lo[j]), (k_hi[j], v_hi[j]))):
            sc = _dot_nt(q_kv[j], kk) * scale + bias
            head0 = pair * (2 * SWA_HPG) + j * SWA_HPG + parity
            sink = jnp.concatenate(
                [jnp.full((tq, 1), sink_ref[head0 + 2 * c], F32) for c in range(per_kv)], axis=0)
            m = jnp.maximum(jnp.max(sc, axis=-1, keepdims=True), sink)
            e = jnp.exp(sc - m)
            denom = jnp.sum(e, axis=-1, keepdims=True) + jnp.exp(sink - m)
            prob = (e * (1.0 / denom)).astype(BF16)
            outs.append(_dot(prob, vv))
        o_pair = outs[0] + outs[1]
        for c in range(per_kv):
            col = (j * per_kv + c) * LANES
            o_ref[:, col:col + LANES] = (o_pair[c * tq:(c + 1) * tq]
                                         * _silu(z_ref[:, col:col + LANES])).astype(BF16)


def swa_attention(bq, kvb, tabs, sinks, batch, seq):
    t = bq.shape[0]
    tq = ATT_TQ
    qt_per_seq = seq // tq
    hw = 2 * SWA_HPG * SWA_DH
    n_pair = SWA_KVH // 2
    z_first = (SWA_KVH * SWA_HPG * SWA_DH) // hw

    def row(b, i):
        return b * qt_per_seq + i

    def prev_row(b, i):
        return b * qt_per_seq + jnp.maximum(i - 1, 0)

    q_spec = pl.BlockSpec((tq, hw), lambda b, pr, i: (row(b, i), pr))
    z_spec = pl.BlockSpec((tq, hw), lambda b, pr, i: (row(b, i), z_first + pr))
    kp_spec = pl.BlockSpec((tq, LANES), lambda b, pr, i: (prev_row(b, i), pr))
    kc_spec = pl.BlockSpec((tq, LANES), lambda b, pr, i: (row(b, i), pr))
    vp_spec = pl.BlockSpec((tq, LANES), lambda b, pr, i: (prev_row(b, i), n_pair + pr))
    vc_spec = pl.BlockSpec((tq, LANES), lambda b, pr, i: (row(b, i), n_pair + pr))
    tq_spec = pl.BlockSpec((tq, LANES), lambda b, pr, i: (i, 0))
    tk_spec = pl.BlockSpec((None, 2, tq, LANES), lambda b, pr, i: (i, 0, 0, 0))
    c, s1, s2 = tabs

    def pairs(a):
        blocks = a.reshape(qt_per_seq, tq, LANES)
        prev = jnp.concatenate([blocks[:1], blocks[:-1]], axis=0)
        return jnp.stack([prev, blocks], axis=1)

    return pl.pallas_call(
        _swa_kernel,
        grid=(batch, n_pair, qt_per_seq),
        in_specs=[pl.BlockSpec(memory_space=pltpu.SMEM), q_spec, z_spec, kp_spec, kc_spec, vp_spec, vc_spec,
                  tq_spec, tq_spec, tq_spec, tk_spec, tk_spec, tk_spec],
        out_specs=pl.BlockSpec((tq, hw), lambda b, pr, i: (row(b, i), pr)),
        out_shape=jax.ShapeDtypeStruct((t, SWA_KVH * SWA_HPG * SWA_DH), BF16),
        compiler_params=_cparams(("arbitrary", "arbitrary", "arbitrary")),
        name="swa_attention",
    )(sinks, bq, bq, kvb, kvb, kvb, kvb, c, s1, s2, pairs(c), pairs(s1), pairs(s2))


def _selection_constants(seq):
    nc = seq // CMP_STRIDE
    ns = seq // SEL_BLOCK
    nsp = -(-ns // LANES) * LANES
    expand = np.zeros((nsp, seq), np.float32)
    expand[np.arange(seq) // SEL_BLOCK, np.arange(seq)] = 1.0
    c0 = np.arange(nc - 1) * CMP_STRIDE
    s0 = np.arange(ns) * SEL_BLOCK
    overlap = (c0[:, None] < s0[None, :] + SEL_BLOCK) & (c0[:, None] + CMP_LEN > s0[None, :])
    cmp_to_sel_t = np.zeros((nsp, nc), np.float32)
    cmp_to_sel_t[:ns, :nc - 1] = overlap.T
    return jnp.asarray(expand, BF16), jnp.asarray(cmp_to_sel_t, BF16)


def kernel(x, p, a_norm, a_w_in, a_w_out, a_cmp_pos_k, a_cmp_w1_k, a_cmp_w2_k, a_cmp_pos_v, a_cmp_w1_v,
           a_cmp_w2_v, kv_norm, w_kv, b_norm, b_w_in, b_w_out, b_sinks, ple_norm, ple_gate_w, ple_proj,
           final_norm):
    batch, seq, d = x.shape
    assert d == D_MODEL and p.shape[0] == 2 and a_w_in.shape[0] == 1 and b_w_in.shape[0] == 1
    t = batch * seq
    xf = x.reshape(t, d)

    qkv_w = NSA_WIDTH + 6 * NSA_KVW
    n_gate = 3 * NSA_G * NSA_HPG
    wa = a_w_in[0]
    w_main = jnp.concatenate([wa[:, :qkv_w], wa[:, qkv_w + n_gate:]], axis=1).astype(BF16)
    wg = wa[:, qkv_w:qkv_w + n_gate].reshape(d, 3, NSA_G, NSA_HPG).transpose(0, 2, 1, 3)
    wg = wg.reshape(d, NSA_G, 3 * NSA_HPG)
    wg = jnp.pad(wg, ((0, 0), (0, 0), (0, LANES - 3 * NSA_HPG))).reshape(d, NSA_G * LANES).astype(BF16)
    pos_kv = jnp.stack([a_cmp_pos_k[0], a_cmp_pos_v[0]])
    w1_kv = jnp.stack([a_cmp_w1_k[0], a_cmp_w1_v[0]]).astype(BF16)
    w2_kv = jnp.stack([a_cmp_w2_k[0], a_cmp_w2_v[0]]).astype(BF16)
    tabs_a = _rope_tables(seq, NSA_DH)
    tabs_b = _rope_tables(seq, SWA_DH)
    expand, cmp_to_sel_t = _selection_constants(seq)

    (n_a,) = rms_norm_rows(xf, a_norm, BF16)
    qa = matmul(n_a, w_main, "a_in_proj")
    gl = matmul(n_a, wg, "a_gate_proj")
    cmp_kv = compress_kv(qa, pos_kv, w1_kv, w2_kv, batch, seq)
    kv_a = prep_kv_a(qa, tabs_a, seq)
    mixed = nsa_attention(qa, gl, cmp_kv, kv_a, tabs_a, expand, cmp_to_sel_t, batch, seq)
    h = matmul_residual(mixed, a_w_out[0].astype(BF16), xf, "a_out_proj")
    (n_p,) = rms_norm_rows(h, ple_norm[0:1], BF16)
    pf = p.reshape(2, t, PLE_DIM)
    h = ple_update(n_p, ple_gate_w[0].astype(BF16), pf[0], ple_proj[0].astype(BF16), h, "ple0")

    n_kv, n_b = rms_norm_rows(h, jnp.stack([kv_norm, b_norm[0]]), BF16)
    kvb = matmul(n_kv, w_kv.astype(BF16), "kv_proj")
    bq = matmul(n_b, b_w_in[0].astype(BF16), "b_in_proj")
    ob = swa_attention(bq, kvb, tabs_b, b_sinks[0], batch, seq)
    h = matmul_residual(ob, b_w_out[0].astype(BF16), h, "b_out_proj")
    (n_p,) = rms_norm_rows(h, ple_norm[1:2], BF16)
    h = ple_update(n_p, ple_gate_w[1].astype(BF16), pf[1], ple_proj[1].astype(BF16), h, "ple1")

    (out,) = rms_norm_rows(h, final_norm[None, :], F32)
    return out.reshape(batch, seq, d)
```

```python
import functools
import math

import numpy as np
import jax
import jax.numpy as jnp
from jax import lax
from jax.experimental import pallas as pl
from jax.experimental.pallas import tpu as pltpu

F32 = jnp.float32
BF16 = jnp.bfloat16

D_MODEL = 4096
RMS_EPS = 1e-6
ROPE_THETA = 500000.0
NEG_BIG = -1e30
FORCE_SCORE = 1e9
PLE_DIM = 256

NSA_DH = 128
NSA_G = 4
NSA_HPG = 8
NSA_WIDTH = 4096
NSA_KVW = 512
CMP_LEN = 32
CMP_STRIDE = 16
CMP_HIDDEN = 256
SEL_BLOCK = 64
SEL_TOPK = 16
SEL_LOCAL = 2
NSA_WINDOW = 512

SWA_DH = 64
SWA_KVH = 8
SWA_HPG = 8
SWA_WIDTH = 4096
SWA_WINDOW = 128

LANES = 128
VMEM_LIMIT_BYTES = 58 * 1024 * 1024

MM_TM = 1024
MM_TN = 1024
MM_TN_FUSED = 512
NORM_ROWS = 256
ATT_TQ = 128
NSA_TK = 512
CHAIN_ROWS = 256

LOG2E = math.log2(math.e)


def _cparams(sem):
    return pltpu.CompilerParams(dimension_semantics=sem, vmem_limit_bytes=VMEM_LIMIT_BYTES)


def _sigmoid(v):
    return 1.0 / (1.0 + jnp.exp(-v))


def _silu(v):
    return v * _sigmoid(v)


def _dot(a, b):
    return jnp.dot(a, b, preferred_element_type=F32)


def _dot_nt(a, b):
    return lax.dot_general(a, b, (((1,), (1,)), ((), ())), preferred_element_type=F32)


def _lane_tile(x, n):
    return jnp.concatenate([x] * n, axis=1)


def _row_tile(x, n):
    return jnp.concatenate([x] * n, axis=0)


def _norm_kernel(h_ref, g_ref, *o_refs):
    x = h_ref[...]
    r = lax.rsqrt(jnp.mean(x * x, axis=-1, keepdims=True) + RMS_EPS)
    y = x * r
    for i, o_ref in enumerate(o_refs):
        o_ref[...] = (y * g_ref[i:i + 1, :]).astype(o_ref.dtype)


def rms_norm_rows(h, gains, out_dtype):
    t, d = h.shape
    k = gains.shape[0]
    outs = pl.pallas_call(
        _norm_kernel,
        grid=(t // NORM_ROWS,),
        in_specs=[pl.BlockSpec((NORM_ROWS, d), lambda i: (i, 0)),
                  pl.BlockSpec((k, d), lambda i: (0, 0))],
        out_specs=[pl.BlockSpec((NORM_ROWS, d), lambda i: (i, 0))] * k,
        out_shape=[jax.ShapeDtypeStruct((t, d), out_dtype)] * k,
        compiler_params=_cparams(("arbitrary",)),
        name="rms_norm",
    )(h, gains)
    return outs


def _rope_tables(seq, head_dim):
    rot = head_dim // 4
    half = rot // 2
    inv_freq = ROPE_THETA ** (-jnp.arange(half, dtype=F32) / half)
    ang = jnp.arange(seq, dtype=jnp.int32).astype(F32)[:, None] * inv_freq[None, :]
    cos, sin = jnp.cos(ang), jnp.sin(ang)
    rest = head_dim - rot
    zeros_h = jnp.zeros((seq, half), F32)
    c = jnp.concatenate([cos, cos, jnp.ones((seq, rest), F32)], axis=1)
    s1 = jnp.concatenate([-sin, zeros_h, jnp.zeros((seq, rest), F32)], axis=1)
    s2 = jnp.concatenate([zeros_h, sin, jnp.zeros((seq, rest), F32)], axis=1)
    reps = LANES // head_dim
    return tuple(jnp.tile(a, (1, reps)) for a in (c, s1, s2))


def _rope_lanes(t, c, s1, s2, half):
    return t * c + pltpu.roll(t, LANES - half, 1) * s1 + pltpu.roll(t, half, 1) * s2


def _mm_kernel(x_ref, w_ref, o_ref, *, act):
    acc = _dot(x_ref[...], w_ref[...])
    if act == "silu":
        acc = _silu(acc)
    elif act == "sigmoid":
        acc = _sigmoid(acc)
    o_ref[...] = acc.astype(o_ref.dtype)


def _mm_res_kernel(x_ref, w_ref, r_ref, o_ref):
    o_ref[...] = r_ref[...] + _dot(x_ref[...], w_ref[...])


def _ple_kernel(n_ref, wg_ref, p_ref, wp_ref, h_ref, o_ref):
    gate = _sigmoid(_dot(n_ref[...], wg_ref[...]))
    emb = _dot(p_ref[...].astype(BF16), wp_ref[...])
    o_ref[...] = h_ref[...] + gate * emb


def _mm_rope_kernel(x_ref, w_ref, c_ref, s1_ref, s2_ref, *o_refs, half, n_rope, plain_too):
    acc = _dot(x_ref[...], w_ref[...])
    c, s1, s2 = c_ref[...], s1_ref[...], s2_ref[...]
    for ch in range(acc.shape[1] // LANES):
        sl = slice(ch * LANES, (ch + 1) * LANES)
        t = acc[:, sl]
        o_refs[0][:, sl] = (_rope_lanes(t, c, s1, s2, half) if ch < n_rope else t).astype(BF16)
        if plain_too:
            o_refs[1][:, sl] = t.astype(BF16)


def _mm_kvb_kernel(x_ref, w_ref, c_ref, s1_ref, s2_ref, o_ref, *, half):
    acc = _dot(x_ref[...], w_ref[...])
    c, s1, s2 = c_ref[...], s1_ref[...], s2_ref[...]
    kvw = SWA_KVH * SWA_DH
    n_ch = kvw // LANES
    low = lax.broadcasted_iota(jnp.int32, (acc.shape[0], LANES), 1) < SWA_DH
    for kind in range(2):
        for ch in range(n_ch):
            t = acc[:, kind * kvw + ch * LANES: kind * kvw + (ch + 1) * LANES]
            if kind == 0:
                t = _rope_lanes(t, c, s1, s2, half)
            ts = pltpu.roll(t, SWA_DH, 1)
            variants = (jnp.where(low, t, 0.0), jnp.where(low, 0.0, ts),
                        jnp.where(low, ts, 0.0), jnp.where(low, 0.0, t))
            for var, val in enumerate(variants):
                col = (kind * 4 + var) * kvw + ch * LANES
                o_ref[:, col:col + LANES] = val.astype(BF16)


def _mm_tiles(m, n, tn_max=MM_TN):
    tm = min(MM_TM, m)
    tn = min(tn_max, n)
    assert m % tm == 0 and n % tn == 0, (m, n)
    return tm, tn


def matmul(x, w, name, act=None, out_dtype=F32):
    m, k = x.shape
    n = w.shape[1]
    tm, tn = _mm_tiles(m, n)
    return pl.pallas_call(
        functools.partial(_mm_kernel, act=act),
        grid=(m // tm, n // tn),
        in_specs=[pl.BlockSpec((tm, k), lambda i, j: (i, 0)),
                  pl.BlockSpec((k, tn), lambda i, j: (0, j))],
        out_specs=pl.BlockSpec((tm, tn), lambda i, j: (i, j)),
        out_shape=jax.ShapeDtypeStruct((m, n), out_dtype),
        compiler_params=_cparams(("arbitrary", "arbitrary")),
        name=name,
    )(x, w)


def matmul_rope(x, w, tabs, seq, head_dim, rope_cols, plain_too, name):
    m, k = x.shape
    n = w.shape[1]
    tm, tn = _mm_tiles(m, n)
    assert seq % tm == 0
    per_seq = seq // tm
    tab_spec = pl.BlockSpec((tm, LANES), lambda i, j: (i % per_seq, 0))
    n_out = 2 if plain_too else 1
    outs = pl.pallas_call(
        functools.partial(_mm_rope_kernel, half=head_dim // 8, n_rope=rope_cols // LANES, plain_too=plain_too),
        grid=(m // tm, n // tn),
        in_specs=[pl.BlockSpec((tm, k), lambda i, j: (i, 0)),
                  pl.BlockSpec((k, tn), lambda i, j: (0, j))] + [tab_spec] * 3,
        out_specs=[pl.BlockSpec((tm, tn), lambda i, j: (i, j))] * n_out,
        out_shape=[jax.ShapeDtypeStruct((m, n), BF16)] * n_out,
        compiler_params=_cparams(("arbitrary", "arbitrary")),
        name=name,
    )(x, w, *tabs)
    return outs


def matmul_kvb(x, w, tabs, seq, name):
    m, k = x.shape
    n = w.shape[1]
    tm = min(MM_TM, m)
    assert seq % tm == 0 and n == 2 * SWA_KVH * SWA_DH
    per_seq = seq // tm
    tab_spec = pl.BlockSpec((tm, LANES), lambda i: (i % per_seq, 0))
    return pl.pallas_call(
        functools.partial(_mm_kvb_kernel, half=SWA_DH // 8),
        grid=(m // tm,),
        in_specs=[pl.BlockSpec((tm, k), lambda i: (i, 0)),
                  pl.BlockSpec((k, n), lambda i: (0, 0))] + [tab_spec] * 3,
        out_specs=pl.BlockSpec((tm, 4 * n), lambda i: (i, 0)),
        out_shape=jax.ShapeDtypeStruct((m, 4 * n), BF16),
        compiler_params=_cparams(("arbitrary",)),
        name=name,
    )(x, w, *tabs)


def matmul_residual(x, w, res, name):
    m, k = x.shape
    n = w.shape[1]
    tm, tn = _mm_tiles(m, n, MM_TN_FUSED)
    return pl.pallas_call(
        _mm_res_kernel,
        grid=(m // tm, n // tn),
        in_specs=[pl.BlockSpec((tm, k), lambda i, j: (i, 0)),
                  pl.BlockSpec((k, tn), lambda i, j: (0, j)),
                  pl.BlockSpec((tm, tn), lambda i, j: (i, j))],
        out_specs=pl.BlockSpec((tm, tn), lambda i, j: (i, j)),
        out_shape=jax.ShapeDtypeStruct((m, n), F32),
        compiler_params=_cparams(("arbitrary", "arbitrary")),
        name=name,
    )(x, w, res)


def ple_update(n, wg, p, wp, h, name):
    m, k = n.shape
    d = wg.shape[1]
    kp = p.shape[1]
    tm, tn = _mm_tiles(m, d, MM_TN_FUSED)
    return pl.pallas_call(
        _ple_kernel,
        grid=(m // tm, d // tn),
        in_specs=[pl.BlockSpec((tm, k), lambda i, j: (i, 0)),
                  pl.BlockSpec((k, tn), lambda i, j: (0, j)),
                  pl.BlockSpec((tm, kp), lambda i, j: (i, 0)),
                  pl.BlockSpec((kp, tn), lambda i, j: (0, j)),
                  pl.BlockSpec((tm, tn), lambda i, j: (i, j))],
        out_specs=pl.BlockSpec((tm, tn), lambda i, j: (i, j)),
        out_shape=jax.ShapeDtypeStruct((m, d), F32),
        compiler_params=_cparams(("arbitrary", "arbitrary")),
        name=name,
    )(n, wg, p, wp, h)


def _compress_kernel(x_ref, pos_ref, w1_ref, w2_ref, o_ref, *, n_cmp_pad):
    nc = n_cmp_pad
    lo = jnp.zeros((nc, CMP_HIDDEN), F32)
    hi = jnp.zeros((nc, CMP_HIDDEN), F32)
    for l in range(CMP_STRIDE):
        xl = x_ref[pl.ds(l, nc, stride=CMP_STRIDE), :]
        a = (xl + pos_ref[0, l:l + 1, :]).astype(BF16)
        b = (xl + pos_ref[0, CMP_STRIDE + l:CMP_STRIDE + l + 1, :]).astype(BF16)
        lo = lo + _dot(a, w1_ref[0, l * LANES:(l + 1) * LANES, :])
        hi = hi + _dot(b, w1_ref[0, (CMP_STRIDE + l) * LANES:(CMP_STRIDE + l + 1) * LANES, :])
    hid = lo + pltpu.roll(hi, nc - 1, 0)
    out = _dot(_silu(hid).astype(BF16), w2_ref[0])
    n_idx = lax.broadcasted_iota(jnp.int32, (nc, LANES), 0)
    o_ref[0, 0, 0] = jnp.where(n_idx < nc - 1, out, 0.0).astype(BF16)


def compress_kv(kcvc, pos, w1, w2, batch, seq):
    nc = seq // CMP_STRIDE
    return pl.pallas_call(
        functools.partial(_compress_kernel, n_cmp_pad=nc),
        grid=(batch, 2, NSA_G),
        in_specs=[pl.BlockSpec((seq, LANES), lambda b, s, g: (b, s * NSA_G + g)),
                  pl.BlockSpec((1, CMP_LEN, LANES), lambda b, s, g: (s, 0, 0)),
                  pl.BlockSpec((1, CMP_LEN * LANES, CMP_HIDDEN), lambda b, s, g: (s, 0, 0)),
                  pl.BlockSpec((1, CMP_HIDDEN, LANES), lambda b, s, g: (s, 0, 0))],
        out_specs=pl.BlockSpec((1, 1, 1, nc, LANES), lambda b, s, g: (b, s, g, 0, 0)),
        out_shape=jax.ShapeDtypeStruct((batch, 2, NSA_G, nc, LANES), BF16),
        compiler_params=_cparams(("arbitrary", "arbitrary", "arbitrary")),
        name="compress_kv",
    )(kcvc, pos, w1, w2)


def _nsa_kernel(qp_ref, qr_ref, z0_ref, z1_ref, z2_ref, g_ref,
                kc_ref, vc_ref, ks_ref, vs_ref, kw_ref, vw_ref, et_ref, mt_ref,
                o_ref, qa_scr, m_scr, acc_scr, mix_scr, *, seq):
    tq, tk, hpg, cr = ATT_TQ, NSA_TK, NSA_HPG, CHAIN_ROWS
    rows = hpg * tq
    n_chain = rows // cr
    nc = seq // CMP_STRIDE
    ns = seq // SEL_BLOCK
    n_top = min(SEL_TOPK, ns)
    nsp = mt_ref.shape[0]
    c_exp = LOG2E / math.sqrt(NSA_DH)
    t0 = pl.program_id(2) * tq
    ones_k = jnp.ones((tk, LANES), BF16)

    t_col = t0 + lax.broadcasted_iota(jnp.int32, (tq, 1), 0)
    gate = g_ref[...]

    def softmax_chain(sc, v_aug):
        m = jnp.max(sc, axis=-1, keepdims=True)
        pe = jnp.exp2((sc - m) * c_exp)
        return _dot(pe.astype(BF16), v_aug)

    n_idx = lax.broadcasted_iota(jnp.int32, (1, nc), 1)
    valid = (n_idx * CMP_STRIDE + (CMP_LEN - 1) <= t_col) & (n_idx < nc - 1)
    bias_c = _row_tile(jnp.where(valid, 0.0, NEG_BIG), cr // tq)
    row_ok = _row_tile(jnp.where(t_col >= CMP_LEN - 1, 1.0, 0.0), cr // tq)
    kc = kc_ref[0, 0, 0]
    vc = vc_ref[0, 0, 0]
    imp = jnp.zeros((nsp, tq), F32)
    for ch in range(n_chain):
        q2 = jnp.concatenate([qp_ref[:, (ch * (cr // tq) + i) * LANES:(ch * (cr // tq) + i + 1) * LANES]
                              for i in range(cr // tq)], axis=0)
        sc = _dot_nt(q2, kc) + bias_c
        m = jnp.max(sc, axis=-1, keepdims=True)
        pe = jnp.exp2((sc - m) * c_exp)
        l = jnp.sum(pe, axis=-1, keepdims=True)
        pn = (pe * (row_ok / l)).astype(BF16)
        oc = _dot(pn, vc)
        imp_h = _dot_nt(mt_ref[...], pn)
        for i in range(cr // tq):
            h = ch * (cr // tq) + i
            imp = imp + imp_h[:, i * tq:(i + 1) * tq]
            sl = slice(h * LANES, (h + 1) * LANES)
            mix_scr[:, sl] = oc[i * tq:(i + 1) * tq] * gate[:, h:h + 1] * z0_ref[:, sl]

    j_io = lax.broadcasted_iota(jnp.int32, (nsp, tq), 0)
    t_io = t0 + lax.broadcasted_iota(jnp.int32, (nsp, tq), 1)
    dist = lax.shift_right_logical(t_io, int(math.log2(SEL_BLOCK))) - j_io
    forced = (j_io == 0) | ((dist >= 0) & (dist < SEL_LOCAL))
    imp = jnp.where(forced, FORCE_SCORE, imp)
    imp = jnp.where(dist >= 0, imp, -1.0)
    rank = jnp.zeros((nsp, tq), F32)
    for k in range(ns):
        rk = imp[k:k + 1, :]
        ahead = (rk > imp) | ((rk == imp) & (j_io > k))
        rank = rank + jnp.where(ahead, 1.0, 0.0)
    not_sel_t = jnp.where((rank < n_top) & (j_io < ns), 0.0, 1.0)
    not_sel = not_sel_t.T.astype(BF16)

    for h in range(hpg):
        qa_scr[h * tq:(h + 1) * tq, 0:LANES] = qr_ref[:, h * LANES:(h + 1) * LANES]
        qa_scr[h * tq:(h + 1) * tq, LANES:2 * LANES] = not_sel

    span = NSA_WINDOW + tq
    k0 = pl.multiple_of(jnp.maximum(t0 - NSA_WINDOW, 0), tq)
    kwin = kw_ref[pl.ds(k0, span), :]
    v_aug = jnp.concatenate([vw_ref[pl.ds(k0, span), :], jnp.ones((span, LANES), BF16)], axis=1)
    kpos = k0 + lax.broadcasted_iota(jnp.int32, (1, span), 1)
    ok = (kpos <= t_col) & (kpos > t_col - NSA_WINDOW)
    bias_w = _row_tile(jnp.where(ok, 0.0, NEG_BIG), cr // tq)
    for ch in range(n_chain):
        q2 = qa_scr[ch * cr:(ch + 1) * cr, 0:LANES]
        acc = softmax_chain(_dot_nt(q2, kwin) + bias_w, v_aug)
        ow = acc[:, 0:LANES] * (1.0 / acc[:, LANES:2 * LANES])
        for i in range(cr // tq):
            h = ch * (cr // tq) + i
            sl = slice(h * LANES, (h + 1) * LANES)
            mix_scr[:, sl] = mix_scr[:, sl] + ow[i * tq:(i + 1) * tq] * gate[:, 2 * hpg + h:2 * hpg + h + 1] * z2_ref[:, sl]

    m_scr[...] = jnp.full((rows, LANES), NEG_BIG, F32)
    acc_scr[...] = jnp.zeros((rows, 2 * LANES), F32)
    n_kt = lax.shift_right_logical(t0 + (tq + tk - 1), int(math.log2(tk)))

    def slc_tile(kt, causal):
        k0 = pl.multiple_of(kt * tk, tk)
        k_aug = jnp.concatenate([ks_ref[pl.ds(k0, tk), :], et_ref[pl.ds(k0, tk), :]], axis=1)
        v_aug = jnp.concatenate([vs_ref[pl.ds(k0, tk), :], ones_k], axis=1)
        if causal:
            kpos = k0 + lax.broadcasted_iota(jnp.int32, (1, tk), 1)
            bias = _row_tile(jnp.where(kpos <= t_col, 0.0, NEG_BIG), cr // tq)
        for ch in range(n_chain):
            rs = slice(ch * cr, (ch + 1) * cr)
            sc = _dot_nt(qa_scr[rs, :], k_aug)
            if causal:
                sc = sc + bias
            m_old = m_scr[rs, :]
            m_new = jnp.maximum(m_old, jnp.max(sc, axis=-1, keepdims=True))
            alpha = jnp.exp2((m_old - m_new) * c_exp)
            pe = jnp.exp2((sc - _lane_tile(m_new, tk // LANES)) * c_exp)
            acc_scr[rs, :] = _lane_tile(alpha, 2) * acc_scr[rs, :] + _dot(pe.astype(BF16), v_aug)
            m_scr[rs, :] = m_new

    def slc_body(kt, carry):
        slc_tile(kt, causal=False)
        return carry

    lax.fori_loop(0, n_kt - 1, slc_body, 0)
    slc_tile(n_kt - 1, causal=True)

    for h in range(hpg):
        rs = slice(h * tq, (h + 1) * tq)
        sl = slice(h * LANES, (h + 1) * LANES)
        o_slc = acc_scr[rs, 0:LANES] * (1.0 / acc_scr[rs, LANES:2 * LANES])
        o_ref[:, sl] = (mix_scr[:, sl] + o_slc * gate[:, hpg + h:hpg + h + 1] * z1_ref[:, sl]).astype(BF16)


def nsa_attention(qp, qr, zs, gates, cmp_kv, kv, e_t, cmp_to_sel_t, batch, seq):
    t = qp.shape[0]
    tq = ATT_TQ
    assert seq % NSA_TK == 0 and seq >= NSA_WINDOW + tq
    qt_per_seq = seq // tq
    nc = seq // CMP_STRIDE
    hw = NSA_HPG * NSA_DH

    def row(b, g, i):
        return b * qt_per_seq + i

    q_spec = pl.BlockSpec((tq, hw), lambda b, g, i: (row(b, g, i), g))
    z_specs = [pl.BlockSpec((tq, hw), functools.partial(
        lambda b, g, i, br: (row(b, g, i), br * NSA_G + g), br=br)) for br in range(3)]
    g_spec = pl.BlockSpec((tq, LANES), lambda b, g, i: (row(b, g, i), g))
    cmp_specs = [pl.BlockSpec((1, 1, 1, nc, LANES), functools.partial(
        lambda b, g, i, s: (b, s, g, 0, 0), s=s)) for s in range(2)]
    kv_specs = [pl.BlockSpec((seq, LANES), functools.partial(
        lambda b, g, i, s: (b, s * NSA_G + g), s=s)) for s in range(4)]
    const_specs = [pl.BlockSpec(e_t.shape, lambda b, g, i: (0, 0)),
                   pl.BlockSpec(cmp_to_sel_t.shape, lambda b, g, i: (0, 0))]
    rows = NSA_HPG * tq
    return pl.pallas_call(
        functools.partial(_nsa_kernel, seq=seq),
        grid=(batch, NSA_G, qt_per_seq),
        in_specs=[q_spec, q_spec] + z_specs + [g_spec] + cmp_specs + kv_specs + const_specs,
        out_specs=pl.BlockSpec((tq, hw), lambda b, g, i: (row(b, g, i), g)),
        out_shape=jax.ShapeDtypeStruct((t, NSA_WIDTH), BF16),
        scratch_shapes=[pltpu.VMEM((rows, 2 * LANES), BF16),
                        pltpu.VMEM((rows, LANES), F32),
                        pltpu.VMEM((rows, 2 * LANES), F32),
                        pltpu.VMEM((tq, hw), F32)],
        compiler_params=_cparams(("arbitrary", "arbitrary", "arbitrary")),
        name="nsa_attention",
    )(qp, qr, zs, zs, zs, gates, cmp_kv, cmp_kv, kv, kv, kv, kv, e_t, cmp_to_sel_t)


def _swa_kernel(sink_ref, q_ref, z_ref, *rest):
    kv_refs, o_ref = rest[:-1], rest[-1]
    tq = ATT_TQ
    n_chunk = 2 * SWA_HPG * SWA_DH // LANES
    per_kv = n_chunk // 2
    scale = 1.0 / math.sqrt(SWA_DH)
    c_exp = LOG2E * scale
    pair = pl.program_id(1)
    t0 = pl.program_id(2) * tq
    ones_k = jnp.ones((2 * tq, LANES), BF16)

    def window(kind, var):
        base = (kind * 4 + var) * 2
        return jnp.concatenate([kv_refs[base][...], kv_refs[base + 1][...]], axis=0)

    t_col = t0 + lax.broadcasted_iota(jnp.int32, (tq, 1), 0)
    kpos = t0 - tq + lax.broadcasted_iota(jnp.int32, (1, 2 * tq), 1)
    ok = (kpos <= t_col) & (kpos > t_col - SWA_WINDOW) & (kpos >= 0)
    bias = jnp.where(ok, 0.0, NEG_BIG)

    for j in range(2):
        k_var = [window(0, 2 * j + par) for par in range(2)]
        v_var = [jnp.concatenate([window(1, 2 * j + par), ones_k], axis=1) for par in range(2)]
        for c in range(per_kv):
            col = (j * per_kv + c) * LANES
            qc = q_ref[:, col:col + LANES]
            o_chunk = None
            for par in range(2):
                head = pair * (2 * SWA_HPG) + j * SWA_HPG + 2 * c + par
                sink_raw = sink_ref[head] * math.sqrt(SWA_DH)
                sc = _dot_nt(qc, k_var[par]) + bias
                m = jnp.maximum(jnp.max(sc, axis=-1, keepdims=True), sink_raw)
                e = jnp.exp2((sc - m) * c_exp)
                acc = _dot(e.astype(BF16), v_var[par])
                denom = acc[:, LANES:2 * LANES] + jnp.exp2((sink_raw - m) * c_exp)
                o_par = acc[:, 0:LANES] * (1.0 / denom)
                o_chunk = o_par if o_chunk is None else o_chunk + o_par
            o_ref[:, col:col + LANES] = (o_chunk * z_ref[:, col:col + LANES]).astype(BF16)


def swa_attention(q, zs, kvb, sinks, batch, seq):
    t = q.shape[0]
    tq = ATT_TQ
    qt_per_seq = seq // tq
    hw = 2 * SWA_HPG * SWA_DH
    n_pair = SWA_KVH // 2

    def row(b, i):
        return b * qt_per_seq + i

    def prev_row(b, i):
        return b * qt_per_seq + jnp.maximum(i - 1, 0)

    qz_spec = pl.BlockSpec((tq, hw), lambda b, pr, i: (row(b, i), pr))
    kv_specs = []
    for kind in range(2):
        for var in range(4):
            cb = (kind * 4 + var) * n_pair
            kv_specs.append(pl.BlockSpec((tq, LANES), functools.partial(
                lambda b, pr, i, cb: (prev_row(b, i), cb + pr), cb=cb)))
            kv_specs.append(pl.BlockSpec((tq, LANES), functools.partial(
                lambda b, pr, i, cb: (row(b, i), cb + pr), cb=cb)))
    return pl.pallas_call(
        _swa_kernel,
        grid=(batch, n_pair, qt_per_seq),
        in_specs=[pl.BlockSpec(memory_space=pltpu.SMEM), qz_spec, qz_spec] + kv_specs,
        out_specs=pl.BlockSpec((tq, hw), lambda b, pr, i: (row(b, i), pr)),
        out_shape=jax.ShapeDtypeStruct((t, SWA_WIDTH), BF16),
        compiler_params=_cparams(("arbitrary", "arbitrary", "arbitrary")),
        name="swa_attention",
    )(sinks, q, zs, *([kvb] * 16))


def _selection_constants(seq):
    nc = seq // CMP_STRIDE
    ns = seq // SEL_BLOCK
    nsp = -(-ns // LANES) * LANES
    e_t = np.zeros((seq, nsp), np.float32)
    e_t[np.arange(seq), np.arange(seq) // SEL_BLOCK] = NEG_BIG
    c0 = np.arange(nc - 1) * CMP_STRIDE
    s0 = np.arange(ns) * SEL_BLOCK
    overlap = (c0[:, None] < s0[None, :] + SEL_BLOCK) & (c0[:, None] + CMP_LEN > s0[None, :])
    cmp_to_sel_t = np.zeros((nsp, nc), np.float32)
    cmp_to_sel_t[:ns, :nc - 1] = overlap.T
    return jnp.asarray(e_t, BF16), jnp.asarray(cmp_to_sel_t, BF16)


def kernel(x, p, a_norm, a_w_in, a_w_out, a_cmp_pos_k, a_cmp_w1_k, a_cmp_w2_k, a_cmp_pos_v, a_cmp_w1_v,
           a_cmp_w2_v, kv_norm, w_kv, b_norm, b_w_in, b_w_out, b_sinks, ple_norm, ple_gate_w, ple_proj,
           final_norm):
    batch, seq, d = x.shape
    assert d == D_MODEL and p.shape[0] == 2 and a_w_in.shape[0] == 1 and b_w_in.shape[0] == 1
    t = batch * seq
    xf = x.reshape(t, d)

    wa = a_w_in[0]
    c_q, c_cmp, c_kv = NSA_WIDTH, NSA_WIDTH + 2 * NSA_KVW, NSA_WIDTH + 6 * NSA_KVW
    n_gate = 3 * NSA_G * NSA_HPG
    w_q = wa[:, :c_q].astype(BF16)
    w_cmp = wa[:, c_q:c_cmp].astype(BF16)
    w_kvs = wa[:, c_cmp:c_kv].astype(BF16)
    w_z = wa[:, c_kv + n_gate:].astype(BF16)
    wg = wa[:, c_kv:c_kv + n_gate].reshape(d, 3, NSA_G, NSA_HPG).transpose(0, 2, 1, 3)
    wg = wg.reshape(d, NSA_G, 3 * NSA_HPG)
    wg = jnp.pad(wg, ((0, 0), (0, 0), (0, LANES - 3 * NSA_HPG))).reshape(d, NSA_G * LANES).astype(BF16)
    pos_kv = jnp.stack([a_cmp_pos_k[0], a_cmp_pos_v[0]])
    w1_kv = jnp.stack([a_cmp_w1_k[0], a_cmp_w1_v[0]]).astype(BF16)
    w2_kv = jnp.stack([a_cmp_w2_k[0], a_cmp_w2_v[0]]).astype(BF16)
    wb = b_w_in[0]
    tabs_a = _rope_tables(seq, NSA_DH)
    tabs_b = _rope_tables(seq, SWA_DH)
    e_t, cmp_to_sel_t = _selection_constants(seq)

    (n_a,) = rms_norm_rows(xf, a_norm, BF16)
    q_rot, q_plain = matmul_rope(n_a, w_q, tabs_a, seq, NSA_DH, MM_TN, True, "a_q_proj")
    (kv_a,) = matmul_rope(n_a, w_kvs, tabs_a, seq, NSA_DH, NSA_KVW, False, "a_kv_proj")
    kcvc = matmul(n_a, w_cmp, "a_cmp_proj")
    zs_a = matmul(n_a, w_z, "a_z_proj", act="silu")
    gates = matmul(n_a, wg, "a_gate_proj", act="sigmoid")
    cmp_kv = compress_kv(kcvc, pos_kv, w1_kv, w2_kv, batch, seq)
    mixed = nsa_attention(q_plain, q_rot, zs_a, gates, cmp_kv, kv_a, e_t, cmp_to_sel_t, batch, seq)
    h = matmul_residual(mixed, a_w_out[0].astype(BF16), xf, "a_out_proj")
    (n_p,) = rms_norm_rows(h, ple_norm[0:1], BF16)
    pf = p.reshape(2, t, PLE_DIM)
    h = ple_update(n_p, ple_gate_w[0].astype(BF16), pf[0], ple_proj[0].astype(BF16), h, "ple0")

    n_kv, n_b = rms_norm_rows(h, jnp.stack([kv_norm, b_norm[0]]), BF16)
    kvb = matmul_kvb(n_kv, w_kv.astype(BF16), tabs_b, seq, "kv_proj")
    (q_b,) = matmul_rope(n_b, wb[:, :SWA_WIDTH].astype(BF16), tabs_b, seq, SWA_DH, MM_TN, False, "b_q_proj")
    zs_b = matmul(n_b, wb[:, SWA_WIDTH:].astype(BF16), "b_z_proj", act="silu")
    ob = swa_attention(q_b, zs_b, kvb, b_sinks[0], batch, seq)
    h = matmul_residual(ob, b_w_out[0].astype(BF16), h, "b_out_proj")
    (n_p,) = rms_norm_rows(h, ple_norm[1:2], BF16)
    h = ple_update(n_p, ple_gate_w[1].astype(BF16), pf[1], ple_proj[1].astype(BF16), h, "ple1")

    (out,) = rms_norm_rows(h, final_norm[None, :], F32)
    return out.reshape(batch, seq, d)
```

```python
import functools
import math

import numpy as np
import jax
import jax.numpy as jnp
from jax import lax
from jax.experimental import pallas as pl
from jax.experimental.pallas import tpu as pltpu

F32 = jnp.float32
BF16 = jnp.bfloat16

D_MODEL = 4096
RMS_EPS = 1e-6
ROPE_THETA = 500000.0
NEG_BIG = -1e30
FORCE_SCORE = 1e9
PLE_DIM = 256

NSA_DH = 128
NSA_G = 4
NSA_HPG = 8
NSA_WIDTH = 4096
NSA_KVW = 512
CMP_LEN = 32
CMP_STRIDE = 16
CMP_HIDDEN = 256
SEL_BLOCK = 64
SEL_TOPK = 16
SEL_LOCAL = 2
NSA_WINDOW = 512

SWA_DH = 64
SWA_KVH = 8
SWA_HPG = 8
SWA_WIDTH = 4096
SWA_WINDOW = 128

LANES = 128
VMEM_LIMIT_BYTES = 58 * 1024 * 1024

MM_TM = 1024
MM_TN = 1024
MM_TN_FUSED = 512
NORM_ROWS = 256
ATT_TQ = 128
NSA_TQ = 256
NSA_TK = 512

LOG2E = math.log2(math.e)


def _cparams(sem):
    return pltpu.CompilerParams(dimension_semantics=sem, vmem_limit_bytes=VMEM_LIMIT_BYTES)


def _sigmoid(v):
    return 1.0 / (1.0 + jnp.exp(-v))


def _silu(v):
    return v * _sigmoid(v)


def _dot(a, b):
    return jnp.dot(a, b, preferred_element_type=F32)


def _dot_nt(a, b):
    return lax.dot_general(a, b, (((1,), (1,)), ((), ())), preferred_element_type=F32)


def _lane_tile(x, n):
    return jnp.concatenate([x] * n, axis=1)


def _norm_kernel(h_ref, g_ref, *o_refs):
    x = h_ref[...]
    r = lax.rsqrt(jnp.mean(x * x, axis=-1, keepdims=True) + RMS_EPS)
    y = x * r
    for i, o_ref in enumerate(o_refs):
        o_ref[...] = (y * g_ref[i:i + 1, :]).astype(o_ref.dtype)


def rms_norm_rows(h, gains, out_dtype):
    t, d = h.shape
    k = gains.shape[0]
    outs = pl.pallas_call(
        _norm_kernel,
        grid=(t // NORM_ROWS,),
        in_specs=[pl.BlockSpec((NORM_ROWS, d), lambda i: (i, 0)),
                  pl.BlockSpec((k, d), lambda i: (0, 0))],
        out_specs=[pl.BlockSpec((NORM_ROWS, d), lambda i: (i, 0))] * k,
        out_shape=[jax.ShapeDtypeStruct((t, d), out_dtype)] * k,
        compiler_params=_cparams(("arbitrary",)),
        name="rms_norm",
    )(h, gains)
    return outs


def _rope_tables(seq, head_dim):
    rot = head_dim // 4
    half = rot // 2
    inv_freq = ROPE_THETA ** (-jnp.arange(half, dtype=F32) / half)
    ang = jnp.arange(seq, dtype=jnp.int32).astype(F32)[:, None] * inv_freq[None, :]
    cos, sin = jnp.cos(ang), jnp.sin(ang)
    rest = head_dim - rot
    zeros_h = jnp.zeros((seq, half), F32)
    c = jnp.concatenate([cos, cos, jnp.ones((seq, rest), F32)], axis=1)
    s1 = jnp.concatenate([-sin, zeros_h, jnp.zeros((seq, rest), F32)], axis=1)
    s2 = jnp.concatenate([zeros_h, sin, jnp.zeros((seq, rest), F32)], axis=1)
    reps = LANES // head_dim
    return tuple(jnp.tile(a, (1, reps)) for a in (c, s1, s2))


def _rope_lanes(t, c, s1, s2, half):
    return t * c + pltpu.roll(t, LANES - half, 1) * s1 + pltpu.roll(t, half, 1) * s2


def _row_rsqrt(s_ref, k):
    return lax.rsqrt(s_ref[...] * (1.0 / k) + RMS_EPS)


def _scaled_dot(x_ref, w_ref, s_ref):
    acc = _dot(x_ref[...], w_ref[...])
    if s_ref is not None:
        acc = acc * _lane_tile(_row_rsqrt(s_ref, x_ref.shape[1]), acc.shape[1] // LANES)
    return acc


def _write_norm_outputs(h, g_ref, hb_refs, ssq_ref):
    for i, hb_ref in enumerate(hb_refs):
        hb_ref[...] = (h * g_ref[i:i + 1, :]).astype(BF16)
    part = jnp.broadcast_to(jnp.sum(h * h, axis=-1, keepdims=True), ssq_ref.shape)
    j = pl.program_id(1)

    @pl.when(j == 0)
    def _():
        ssq_ref[...] = part

    @pl.when(j > 0)
    def _():
        ssq_ref[...] = ssq_ref[...] + part


def _mm_kernel(*refs, act, scaled):
    x_ref, w_ref = refs[0], refs[1]
    s_ref = refs[2] if scaled else None
    o_ref = refs[-1]
    acc = _scaled_dot(x_ref, w_ref, s_ref)
    if act == "silu":
        acc = _silu(acc)
    elif act == "sigmoid":
        acc = _sigmoid(acc)
    o_ref[...] = acc.astype(o_ref.dtype)


def _mm_zgate_kernel(x_ref, w_ref, g_ref, o_ref, *, first_col):
    acc = _silu(_dot(x_ref[...], w_ref[...]))
    for h in range(acc.shape[1] // LANES):
        sl = slice(h * LANES, (h + 1) * LANES)
        o_ref[:, sl] = acc[:, sl] * g_ref[:, first_col + h:first_col + h + 1]


def _mm_res_kernel(x_ref, w_ref, r_ref, g_ref, h_ref, *rest):
    h = r_ref[...] + _dot(x_ref[...], w_ref[...])
    h_ref[...] = h
    _write_norm_outputs(h, g_ref, rest[:-1], rest[-1])


def _ple_kernel(n_ref, wg_ref, s_ref, p_ref, wp_ref, h_ref, *rest, n_gain):
    gate = _sigmoid(_scaled_dot(n_ref, wg_ref, s_ref))
    emb = _dot(p_ref[...].astype(BF16), wp_ref[...])
    h = h_ref[...] + gate * emb
    if n_gain:
        g_ref, o_ref = rest[0], rest[1]
        o_ref[...] = h
        _write_norm_outputs(h, g_ref, rest[2:-1], rest[-1])
    else:
        rest[0][...] = h


def _mm_rope_kernel(*refs, half, n_rope, plain_too, scaled):
    x_ref, w_ref = refs[0], refs[1]
    s_ref = refs[2] if scaled else None
    c_ref, s1_ref, s2_ref = refs[3:6] if scaled else refs[2:5]
    o_refs = refs[6:] if scaled else refs[5:]
    acc = _scaled_dot(x_ref, w_ref, s_ref)
    c, s1, s2 = c_ref[...], s1_ref[...], s2_ref[...]
    for ch in range(acc.shape[1] // LANES):
        sl = slice(ch * LANES, (ch + 1) * LANES)
        t = acc[:, sl]
        o_refs[0][:, sl] = (_rope_lanes(t, c, s1, s2, half) if ch < n_rope else t).astype(BF16)
        if plain_too:
            o_refs[1][:, sl] = t.astype(BF16)


def _mm_kvb_kernel(x_ref, w_ref, s_ref, c_ref, s1_ref, s2_ref, o_ref, *, half):
    acc = _scaled_dot(x_ref, w_ref, s_ref)
    c, s1, s2 = c_ref[...], s1_ref[...], s2_ref[...]
    kvw = SWA_KVH * SWA_DH
    n_ch = kvw // LANES
    low = lax.broadcasted_iota(jnp.int32, (acc.shape[0], LANES), 1) < SWA_DH
    for kind in range(2):
        for ch in range(n_ch):
            t = acc[:, kind * kvw + ch * LANES: kind * kvw + (ch + 1) * LANES]
            if kind == 0:
                t = _rope_lanes(t, c, s1, s2, half)
            ts = pltpu.roll(t, SWA_DH, 1)
            variants = (jnp.where(low, t, 0.0), jnp.where(low, 0.0, ts),
                        jnp.where(low, ts, 0.0), jnp.where(low, 0.0, t))
            for var, val in enumerate(variants):
                col = (kind * 4 + var) * kvw + ch * LANES
                o_ref[:, col:col + LANES] = val.astype(BF16)


def _mm_tiles(m, n, tn_max=MM_TN):
    tm = min(MM_TM, m)
    tn = min(tn_max, n)
    assert m % tm == 0 and n % tn == 0, (m, n)
    return tm, tn


def _ssq_spec(tm):
    return pl.BlockSpec((tm, LANES), lambda i, j: (i, 0))


def matmul(x, w, name, act=None, out_dtype=F32, ssq=None):
    m, k = x.shape
    n = w.shape[1]
    tm, tn = _mm_tiles(m, n)
    scaled = ssq is not None
    return pl.pallas_call(
        functools.partial(_mm_kernel, act=act, scaled=scaled),
        grid=(m // tm, n // tn),
        in_specs=[pl.BlockSpec((tm, k), lambda i, j: (i, 0)),
                  pl.BlockSpec((k, tn), lambda i, j: (0, j))] + ([_ssq_spec(tm)] if scaled else []),
        out_specs=pl.BlockSpec((tm, tn), lambda i, j: (i, j)),
        out_shape=jax.ShapeDtypeStruct((m, n), out_dtype),
        compiler_params=_cparams(("arbitrary", "arbitrary")),
        name=name,
    )(x, w, *([ssq] if scaled else []))


def matmul_zgate(x, w, gates, branch, name):
    m, k = x.shape
    n = w.shape[1]
    tm, tn = _mm_tiles(m, n)
    assert tn == NSA_HPG * NSA_DH
    return pl.pallas_call(
        functools.partial(_mm_zgate_kernel, first_col=branch * NSA_HPG),
        grid=(m // tm, n // tn),
        in_specs=[pl.BlockSpec((tm, k), lambda i, j: (i, 0)),
                  pl.BlockSpec((k, tn), lambda i, j: (0, j)),
                  pl.BlockSpec((tm, LANES), lambda i, j: (i, j))],
        out_specs=pl.BlockSpec((tm, tn), lambda i, j: (i, j)),
        out_shape=jax.ShapeDtypeStruct((m, n), F32),
        compiler_params=_cparams(("arbitrary", "arbitrary")),
        name=name,
    )(x, w, gates)


def matmul_rope(x, w, tabs, seq, head_dim, rope_cols, plain_too, name, ssq=None):
    m, k = x.shape
    n = w.shape[1]
    tm, tn = _mm_tiles(m, n)
    assert seq % tm == 0
    per_seq = seq // tm
    scaled = ssq is not None
    tab_spec = pl.BlockSpec((tm, LANES), lambda i, j: (i % per_seq, 0))
    n_out = 2 if plain_too else 1
    outs = pl.pallas_call(
        functools.partial(_mm_rope_kernel, half=head_dim // 8, n_rope=rope_cols // LANES,
                          plain_too=plain_too, scaled=scaled),
        grid=(m // tm, n // tn),
        in_specs=[pl.BlockSpec((tm, k), lambda i, j: (i, 0)),
                  pl.BlockSpec((k, tn), lambda i, j: (0, j))]
        + ([_ssq_spec(tm)] if scaled else []) + [tab_spec] * 3,
        out_specs=[pl.BlockSpec((tm, tn), lambda i, j: (i, j))] * n_out,
        out_shape=[jax.ShapeDtypeStruct((m, n), BF16)] * n_out,
        compiler_params=_cparams(("arbitrary", "arbitrary")),
        name=name,
    )(x, w, *([ssq] if scaled else []), *tabs)
    return outs


def matmul_kvb(x, w, ssq, tabs, seq, name):
    m, k = x.shape
    n = w.shape[1]
    tm = min(MM_TM, m)
    assert seq % tm == 0 and n == 2 * SWA_KVH * SWA_DH
    per_seq = seq // tm
    tab_spec = pl.BlockSpec((tm, LANES), lambda i: (i % per_seq, 0))
    return pl.pallas_call(
        functools.partial(_mm_kvb_kernel, half=SWA_DH // 8),
        grid=(m // tm,),
        in_specs=[pl.BlockSpec((tm, k), lambda i: (i, 0)),
                  pl.BlockSpec((k, n), lambda i: (0, 0)),
                  pl.BlockSpec((tm, LANES), lambda i: (i, 0))] + [tab_spec] * 3,
        out_specs=pl.BlockSpec((tm, 4 * n), lambda i: (i, 0)),
        out_shape=jax.ShapeDtypeStruct((m, 4 * n), BF16),
        compiler_params=_cparams(("arbitrary",)),
        name=name,
    )(x, w, ssq, *tabs)


def matmul_residual(x, w, res, gains, name):
    m, k = x.shape
    n = w.shape[1]
    kg = gains.shape[0]
    tm, tn = _mm_tiles(m, n, MM_TN_FUSED)
    tile = pl.BlockSpec((tm, tn), lambda i, j: (i, j))
    outs = pl.pallas_call(
        _mm_res_kernel,
        grid=(m // tm, n // tn),
        in_specs=[pl.BlockSpec((tm, k), lambda i, j: (i, 0)),
                  pl.BlockSpec((k, tn), lambda i, j: (0, j)),
                  tile,
                  pl.BlockSpec((kg, tn), lambda i, j: (0, j))],
        out_specs=[tile] + [tile] * kg + [_ssq_spec(tm)],
        out_shape=[jax.ShapeDtypeStruct((m, n), F32)] + [jax.ShapeDtypeStruct((m, n), BF16)] * kg
        + [jax.ShapeDtypeStruct((m, LANES), F32)],
        compiler_params=_cparams(("arbitrary", "arbitrary")),
        name=name,
    )(x, w, res, gains)
    return outs[0], outs[1:-1], outs[-1]


def ple_update(n, ssq, wg, p, wp, h, gains, name):
    m, k = n.shape
    d = wg.shape[1]
    kp = p.shape[1]
    kg = 0 if gains is None else gains.shape[0]
    tm, tn = _mm_tiles(m, d, MM_TN_FUSED)
    tile = pl.BlockSpec((tm, tn), lambda i, j: (i, j))
    in_specs = [pl.BlockSpec((tm, k), lambda i, j: (i, 0)),
                pl.BlockSpec((k, tn), lambda i, j: (0, j)),
                _ssq_spec(tm),
                pl.BlockSpec((tm, kp), lambda i, j: (i, 0)),
                pl.BlockSpec((kp, tn), lambda i, j: (0, j)),
                tile]
    args = [n, wg, ssq, p, wp, h]
    out_specs = [tile]
    out_shape = [jax.ShapeDtypeStruct((m, d), F32)]
    if kg:
        in_specs.append(pl.BlockSpec((kg, tn), lambda i, j: (0, j)))
        args.append(gains)
        out_specs += [tile] * kg + [_ssq_spec(tm)]
        out_shape += [jax.ShapeDtypeStruct((m, d), BF16)] * kg + [jax.ShapeDtypeStruct((m, LANES), F32)]
    outs = pl.pallas_call(
        functools.partial(_ple_kernel, n_gain=kg),
        grid=(m // tm, d // tn),
        in_specs=in_specs,
        out_specs=out_specs,
        out_shape=out_shape,
        compiler_params=_cparams(("arbitrary", "arbitrary")),
        name=name,
    )(*args)
    if kg:
        return outs[0], outs[1:-1], outs[-1]
    return outs[0]


def _compress_kernel(x_ref, pos_ref, w1_ref, w2_ref, o_ref, *, n_cmp_pad):
    nc = n_cmp_pad
    lo = jnp.zeros((nc, CMP_HIDDEN), F32)
    hi = jnp.zeros((nc, CMP_HIDDEN), F32)
    for l in range(CMP_STRIDE):
        xl = x_ref[pl.ds(l, nc, stride=CMP_STRIDE), :]
        a = (xl + pos_ref[0, l:l + 1, :]).astype(BF16)
        b = (xl + pos_ref[0, CMP_STRIDE + l:CMP_STRIDE + l + 1, :]).astype(BF16)
        lo = lo + _dot(a, w1_ref[0, l * LANES:(l + 1) * LANES, :])
        hi = hi + _dot(b, w1_ref[0, (CMP_STRIDE + l) * LANES:(CMP_STRIDE + l + 1) * LANES, :])
    hid = lo + pltpu.roll(hi, nc - 1, 0)
    out = _dot(_silu(hid).astype(BF16), w2_ref[0])
    n_idx = lax.broadcasted_iota(jnp.int32, (nc, LANES), 0)
    o_ref[0, 0, 0] = jnp.where(n_idx < nc - 1, out, 0.0).astype(BF16)


def compress_kv(kcvc, pos, w1, w2, batch, seq):
    nc = seq // CMP_STRIDE
    return pl.pallas_call(
        functools.partial(_compress_kernel, n_cmp_pad=nc),
        grid=(batch, 2, NSA_G),
        in_specs=[pl.BlockSpec((seq, LANES), lambda b, s, g: (b, s * NSA_G + g)),
                  pl.BlockSpec((1, CMP_LEN, LANES), lambda b, s, g: (s, 0, 0)),
                  pl.BlockSpec((1, CMP_LEN * LANES, CMP_HIDDEN), lambda b, s, g: (s, 0, 0)),
                  pl.BlockSpec((1, CMP_HIDDEN, LANES), lambda b, s, g: (s, 0, 0))],
        out_specs=pl.BlockSpec((1, 1, 1, nc, LANES), lambda b, s, g: (b, s, g, 0, 0)),
        out_shape=jax.ShapeDtypeStruct((batch, 2, NSA_G, nc, LANES), BF16),
        compiler_params=_cparams(("arbitrary", "arbitrary", "arbitrary")),
        name="compress_kv",
    )(kcvc, pos, w1, w2)


def _nsa_kernel(qp_ref, qr_ref, z0_ref, z1_ref, z2_ref,
                kc_ref, vc_ref, ks_ref, vs_ref, kw_ref, vw_ref, et_ref, mt_ref,
                o_ref, qa_scr, m_scr, acc_scr, mix_scr, *, seq):
    tq, tk, hpg = NSA_TQ, NSA_TK, NSA_HPG
    rows = hpg * tq
    nc = seq // CMP_STRIDE
    ns = seq // SEL_BLOCK
    n_top = min(SEL_TOPK, ns)
    nsp = mt_ref.shape[0]
    c_exp = LOG2E / math.sqrt(NSA_DH)
    t0 = pl.program_id(2) * tq
    ones_k = jnp.ones((tk, LANES), BF16)

    t_col = t0 + lax.broadcasted_iota(jnp.int32, (tq, 1), 0)

    n_idx = lax.broadcasted_iota(jnp.int32, (1, nc), 1)
    valid = (n_idx * CMP_STRIDE + (CMP_LEN - 1) <= t_col) & (n_idx < nc - 1)
    bias_c = jnp.where(valid, 0.0, NEG_BIG)
    row_ok = t0 + lax.broadcasted_iota(jnp.int32, (tq, LANES), 0) >= CMP_LEN - 1
    lane_ok = t0 + lax.broadcasted_iota(jnp.int32, (ns, tq), 1) >= CMP_LEN - 1
    kc = kc_ref[0, 0, 0]
    vc_aug = jnp.concatenate([vc_ref[0, 0, 0], jnp.ones((nc, LANES), BF16)], axis=1)
    span = NSA_WINDOW + tq
    kw0 = pl.multiple_of(jnp.maximum(t0 - NSA_WINDOW, 0), tq)
    kwin = kw_ref[pl.ds(kw0, span), :]
    vw_aug = jnp.concatenate([vw_ref[pl.ds(kw0, span), :], jnp.ones((span, LANES), BF16)], axis=1)
    kpos_w = kw0 + lax.broadcasted_iota(jnp.int32, (1, span), 1)
    bias_w = jnp.where((kpos_w <= t_col) & (kpos_w > t_col - NSA_WINDOW), 0.0, NEG_BIG)
    imp_parts = []
    for h in range(hpg):
        sl = slice(h * LANES, (h + 1) * LANES)
        sc = _dot_nt(qp_ref[:, sl], kc) + bias_c
        m = jnp.max(sc, axis=-1, keepdims=True)
        pe = jnp.exp2((sc - m) * c_exp).astype(BF16)
        acc = _dot(pe, vc_aug)
        oc = jnp.where(row_ok, acc[:, 0:LANES] * (1.0 / acc[:, LANES:2 * LANES]), 0.0)
        imp_t = _dot_nt(mt_ref[...], pe)
        imp_parts.append(imp_t[0:ns, :] * (1.0 / imp_t[nsp - 1:nsp, :]))
        sc = _dot_nt(qr_ref[:, sl], kwin) + bias_w
        m = jnp.max(sc, axis=-1, keepdims=True)
        pe = jnp.exp2((sc - m) * c_exp).astype(BF16)
        acc = _dot(pe, vw_aug)
        ow = acc[:, 0:LANES] * (1.0 / acc[:, LANES:2 * LANES])
        mix_scr[:, sl] = oc * z0_ref[:, sl] + ow * z2_ref[:, sl]
    while len(imp_parts) > 1:
        imp_parts = [imp_parts[i] + imp_parts[i + 1] for i in range(0, len(imp_parts), 2)]
    imp = jnp.where(lane_ok, imp_parts[0], 0.0)

    j_io = lax.broadcasted_iota(jnp.int32, (ns, tq), 0)
    t_io = t0 + lax.broadcasted_iota(jnp.int32, (ns, tq), 1)
    dist = lax.shift_right_logical(t_io, int(math.log2(SEL_BLOCK))) - j_io
    forced = (j_io == 0) | ((dist >= 0) & (dist < SEL_LOCAL))
    imp = jnp.where(forced, FORCE_SCORE, imp)
    imp = jnp.where(dist >= 0, imp, -1.0)
    sub = 8
    ranks = []
    for v in range(ns // sub):
        blk = imp[v * sub:(v + 1) * sub, :]
        j_v = v * sub + lax.broadcasted_iota(jnp.int32, (sub, tq), 0)
        r = jnp.zeros((sub, tq), F32)
        for k in range(ns):
            rk = imp[k:k + 1, :]
            if k < v * sub:
                ahead = rk >= blk
            elif k >= (v + 1) * sub:
                ahead = rk > blk
            else:
                ahead = (rk > blk) | ((rk == blk) & (j_v > k))
            r = r + jnp.where(ahead, 1.0, 0.0)
        ranks.append(r)
    rank = jnp.concatenate(ranks, axis=0)
    not_sel_t = jnp.concatenate([jnp.where(rank < n_top, 0.0, 1.0), jnp.ones((nsp - ns, tq), F32)], axis=0)
    not_sel = not_sel_t.T.astype(BF16)

    for h in range(hpg):
        qa_scr[h * tq:(h + 1) * tq, 0:LANES] = qr_ref[:, h * LANES:(h + 1) * LANES]
        qa_scr[h * tq:(h + 1) * tq, LANES:2 * LANES] = not_sel

    m_scr[...] = jnp.full((rows, LANES), NEG_BIG, F32)
    acc_scr[...] = jnp.zeros((rows, 2 * LANES), F32)
    n_kt = lax.shift_right_logical(t0 + (tq + tk - 1), int(math.log2(tk)))

    def slc_tile(kt, causal):
        k0 = pl.multiple_of(kt * tk, tk)
        k_aug = jnp.concatenate([ks_ref[pl.ds(k0, tk), :], et_ref[pl.ds(k0, tk), :]], axis=1)
        v_aug = jnp.concatenate([vs_ref[pl.ds(k0, tk), :], ones_k], axis=1)
        if causal:
            kpos = k0 + lax.broadcasted_iota(jnp.int32, (1, tk), 1)
            bias = jnp.where(kpos <= t_col, 0.0, NEG_BIG)
        for h in range(hpg):
            rs = slice(h * tq, (h + 1) * tq)
            sc = _dot_nt(qa_scr[rs, :], k_aug)
            if causal:
                sc = sc + bias
            m_old = m_scr[rs, :]
            m_new = jnp.maximum(m_old, jnp.max(sc, axis=-1, keepdims=True))
            alpha = jnp.exp2((m_old - m_new) * c_exp)
            pe = jnp.exp2((sc - _lane_tile(m_new, tk // LANES)) * c_exp)
            acc_scr[rs, :] = _lane_tile(alpha, 2) * acc_scr[rs, :] + _dot(pe.astype(BF16), v_aug)
            m_scr[rs, :] = m_new

    def slc_body(kt, carry):
        slc_tile(kt, causal=False)
        return carry

    lax.fori_loop(0, n_kt - 1, slc_body, 0)
    slc_tile(n_kt - 1, causal=True)

    for h in range(hpg):
        rs = slice(h * tq, (h + 1) * tq)
        sl = slice(h * LANES, (h + 1) * LANES)
        o_slc = acc_scr[rs, 0:LANES] * (1.0 / acc_scr[rs, LANES:2 * LANES])
        o_ref[:, sl] = (mix_scr[:, sl] + o_slc * z1_ref[:, sl]).astype(BF16)


def nsa_attention(qp, qr, zg, cmp_kv, kv, e_t, cmp_to_sel_t, batch, seq):
    t = qp.shape[0]
    tq = NSA_TQ
    assert seq % NSA_TK == 0 and NSA_TK % tq == 0 and seq >= NSA_WINDOW + tq
    qt_per_seq = seq // tq
    nc = seq // CMP_STRIDE
    hw = NSA_HPG * NSA_DH

    qz_spec = pl.BlockSpec((tq, hw), lambda b, g, i: (b * qt_per_seq + i, g))
    cmp_specs = [pl.BlockSpec((1, 1, 1, nc, LANES), functools.partial(
        lambda b, g, i, s: (b, s, g, 0, 0), s=s)) for s in range(2)]
    kv_specs = [pl.BlockSpec((seq, LANES), functools.partial(
        lambda b, g, i, s: (b, s * NSA_G + g), s=s)) for s in range(4)]
    const_specs = [pl.BlockSpec(e_t.shape, lambda b, g, i: (0, 0)),
                   pl.BlockSpec(cmp_to_sel_t.shape, lambda b, g, i: (0, 0))]
    rows = NSA_HPG * tq
    return pl.pallas_call(
        functools.partial(_nsa_kernel, seq=seq),
        grid=(batch, NSA_G, qt_per_seq),
        in_specs=[qz_spec] * 5 + cmp_specs + kv_specs + const_specs,
        out_specs=qz_spec,
        out_shape=jax.ShapeDtypeStruct((t, NSA_WIDTH), BF16),
        scratch_shapes=[pltpu.VMEM((rows, 2 * LANES), BF16),
                        pltpu.VMEM((rows, LANES), F32),
                        pltpu.VMEM((rows, 2 * LANES), F32),
                        pltpu.VMEM((tq, hw), F32)],
        compiler_params=_cparams(("arbitrary", "arbitrary", "arbitrary")),
        name="nsa_attention",
    )(qp, qr, *zg, cmp_kv, cmp_kv, kv, kv, kv, kv, e_t, cmp_to_sel_t)


def _swa_kernel(sink_ref, q_ref, z_ref, *rest):
    kv_refs, o_ref = rest[:-1], rest[-1]
    tq = ATT_TQ
    n_pair = SWA_KVH // 2
    per_kv = SWA_HPG * SWA_DH // LANES
    scale = 1.0 / math.sqrt(SWA_DH)
    c_exp = LOG2E * scale
    t0 = pl.program_id(1) * tq
    ones_k = jnp.ones((2 * tq, LANES), BF16)

    def window(kind, var, pair):
        base = (kind * 4 + var) * 2
        sl = slice(pair * LANES, (pair + 1) * LANES)
        return jnp.concatenate([kv_refs[base][:, sl], kv_refs[base + 1][:, sl]], axis=0)

    t_col = t0 + lax.broadcasted_iota(jnp.int32, (tq, 1), 0)
    kpos = t0 - tq + lax.broadcasted_iota(jnp.int32, (1, 2 * tq), 1)
    ok = (kpos <= t_col) & (kpos > t_col - SWA_WINDOW) & (kpos >= 0)
    bias = jnp.where(ok, 0.0, NEG_BIG)

    for pair in range(n_pair):
        for j in range(2):
            k_var = [window(0, 2 * j + par, pair) for par in range(2)]
            v_var = [jnp.concatenate([window(1, 2 * j + par, pair), ones_k], axis=1) for par in range(2)]
            for c in range(per_kv):
                col = ((pair * 2 + j) * per_kv + c) * LANES
                qc = q_ref[:, col:col + LANES]
                o_chunk = None
                for par in range(2):
                    head = (pair * 2 + j) * SWA_HPG + 2 * c + par
                    sink_raw = sink_ref[head] * math.sqrt(SWA_DH)
                    sc = _dot_nt(qc, k_var[par]) + bias
                    m = jnp.maximum(jnp.max(sc, axis=-1, keepdims=True), sink_raw)
                    e = jnp.exp2((sc - m) * c_exp)
                    acc = _dot(e.astype(BF16), v_var[par])
                    denom = acc[:, LANES:2 * LANES] + jnp.exp2((sink_raw - m) * c_exp)
                    o_par = acc[:, 0:LANES] * (1.0 / denom)
                    o_chunk = o_par if o_chunk is None else o_chunk + o_par
                o_ref[:, col:col + LANES] = (o_chunk * z_ref[:, col:col + LANES]).astype(BF16)


def swa_attention(q, zs, kvb, sinks, batch, seq):
    t = q.shape[0]
    tq = ATT_TQ
    qt_per_seq = seq // tq
    kvw = SWA_KVH * SWA_DH

    def row(b, i):
        return b * qt_per_seq + i

    def prev_row(b, i):
        return b * qt_per_seq + jnp.maximum(i - 1, 0)

    qz_spec = pl.BlockSpec((tq, SWA_WIDTH), lambda b, i: (row(b, i), 0))
    kv_specs = []
    for cb in range(8):
        kv_specs.append(pl.BlockSpec((tq, kvw), functools.partial(lambda b, i, cb: (prev_row(b, i), cb), cb=cb)))
        kv_specs.append(pl.BlockSpec((tq, kvw), functools.partial(lambda b, i, cb: (row(b, i), cb), cb=cb)))
    return pl.pallas_call(
        _swa_kernel,
        grid=(batch, qt_per_seq),
        in_specs=[pl.BlockSpec(memory_space=pltpu.SMEM), qz_spec, qz_spec] + kv_specs,
        out_specs=qz_spec,
        out_shape=jax.ShapeDtypeStruct((t, SWA_WIDTH), BF16),
        compiler_params=_cparams(("arbitrary", "arbitrary")),
        name="swa_attention",
    )(sinks, q, zs, *([kvb] * 16))


def _selection_constants(seq):
    nc = seq // CMP_STRIDE
    ns = seq // SEL_BLOCK
    nsp = -(-ns // LANES) * LANES
    e_t = np.zeros((seq, nsp), np.float32)
    e_t[np.arange(seq), np.arange(seq) // SEL_BLOCK] = NEG_BIG
    c0 = np.arange(nc - 1) * CMP_STRIDE
    s0 = np.arange(ns) * SEL_BLOCK
    overlap = (c0[:, None] < s0[None, :] + SEL_BLOCK) & (c0[:, None] + CMP_LEN > s0[None, :])
    assert ns < nsp
    cmp_to_sel_t = np.zeros((nsp, nc), np.float32)
    cmp_to_sel_t[:ns, :nc - 1] = overlap.T
    cmp_to_sel_t[nsp - 1, :] = 1.0
    return jnp.asarray(e_t, BF16), jnp.asarray(cmp_to_sel_t, BF16)


def kernel(x, p, a_norm, a_w_in, a_w_out, a_cmp_pos_k, a_cmp_w1_k, a_cmp_w2_k, a_cmp_pos_v, a_cmp_w1_v,
           a_cmp_w2_v, kv_norm, w_kv, b_norm, b_w_in, b_w_out, b_sinks, ple_norm, ple_gate_w, ple_proj,
           final_norm):
    batch, seq, d = x.shape
    assert d == D_MODEL and p.shape[0] == 2 and a_w_in.shape[0] == 1 and b_w_in.shape[0] == 1
    t = batch * seq
    xf = x.reshape(t, d)

    wa = a_w_in[0]
    c_q, c_cmp, c_kv = NSA_WIDTH, NSA_WIDTH + 2 * NSA_KVW, NSA_WIDTH + 6 * NSA_KVW
    n_gate = 3 * NSA_G * NSA_HPG
    w_q = wa[:, :c_q].astype(BF16)
    w_cmp = wa[:, c_q:c_cmp].astype(BF16)
    w_kvs = wa[:, c_cmp:c_kv].astype(BF16)
    w_z = [wa[:, c_kv + n_gate + br * NSA_WIDTH:c_kv + n_gate + (br + 1) * NSA_WIDTH].astype(BF16)
           for br in range(3)]
    wg = wa[:, c_kv:c_kv + n_gate].reshape(d, 3, NSA_G, NSA_HPG).transpose(0, 2, 1, 3)
    wg = wg.reshape(d, NSA_G, 3 * NSA_HPG)
    wg = jnp.pad(wg, ((0, 0), (0, 0), (0, LANES - 3 * NSA_HPG))).reshape(d, NSA_G * LANES).astype(BF16)
    pos_kv = jnp.stack([a_cmp_pos_k[0], a_cmp_pos_v[0]])
    w1_kv = jnp.stack([a_cmp_w1_k[0], a_cmp_w1_v[0]]).astype(BF16)
    w2_kv = jnp.stack([a_cmp_w2_k[0], a_cmp_w2_v[0]]).astype(BF16)
    wb = b_w_in[0]
    tabs_a = _rope_tables(seq, NSA_DH)
    tabs_b = _rope_tables(seq, SWA_DH)
    e_t, cmp_to_sel_t = _selection_constants(seq)

    (n_a,) = rms_norm_rows(xf, a_norm, BF16)
    q_rot, q_plain = matmul_rope(n_a, w_q, tabs_a, seq, NSA_DH, MM_TN, True, "a_q_proj")
    (kv_a,) = matmul_rope(n_a, w_kvs, tabs_a, seq, NSA_DH, NSA_KVW, False, "a_kv_proj")
    kcvc = matmul(n_a, w_cmp, "a_cmp_proj")
    gates = matmul(n_a, wg, "a_gate_proj", act="sigmoid")
    zg = [matmul_zgate(n_a, w_z[br], gates, br, "a_z_proj%d" % br) for br in range(3)]
    cmp_kv = compress_kv(kcvc, pos_kv, w1_kv, w2_kv, batch, seq)
    mixed = nsa_attention(q_plain, q_rot, zg, cmp_kv, kv_a, e_t, cmp_to_sel_t, batch, seq)
    h, (n_p,), ssq = matmul_residual(mixed, a_w_out[0].astype(BF16), xf, ple_norm[0:1], "a_out_proj")
    pf = p.reshape(2, t, PLE_DIM)
    h, (n_kv, n_b), ssq = ple_update(n_p, ssq, ple_gate_w[0].astype(BF16), pf[0], ple_proj[0].astype(BF16), h,
                                     jnp.stack([kv_norm, b_norm[0]]), "ple0")

    kvb = matmul_kvb(n_kv, w_kv.astype(BF16), ssq, tabs_b, seq, "kv_proj")
    (q_b,) = matmul_rope(n_b, wb[:, :SWA_WIDTH].astype(BF16), tabs_b, seq, SWA_DH, MM_TN, False, "b_q_proj",
                         ssq=ssq)
    zs_b = matmul(n_b, wb[:, SWA_WIDTH:].astype(BF16), "b_z_proj", act="silu", ssq=ssq)
    ob = swa_attention(q_b, zs_b, kvb, b_sinks[0], batch, seq)
    h, (n_p,), ssq = matmul_residual(ob, b_w_out[0].astype(BF16), h, ple_norm[1:2], "b_out_proj")
    h = ple_update(n_p, ssq, ple_gate_w[1].astype(BF16), pf[1], ple_proj[1].astype(BF16), h, None, "ple1")

    (out,) = rms_norm_rows(h, final_norm[None, :], F32)
    return out.reshape(batch, seq, d)
```

```python
import functools
import math

import numpy as np
import jax
import jax.numpy as jnp
from jax import lax
from jax.experimental import pallas as pl
from jax.experimental.pallas import tpu as pltpu

F32 = jnp.float32
BF16 = jnp.bfloat16

D_MODEL = 4096
RMS_EPS = 1e-6
ROPE_THETA = 500000.0
NEG_BIG = -1e30
FORCE_SCORE = 1e9
PLE_DIM = 256

NSA_DH = 128
NSA_G = 4
NSA_HPG = 8
NSA_WIDTH = 4096
NSA_KVW = 512
CMP_LEN = 32
CMP_STRIDE = 16
CMP_HIDDEN = 256
SEL_BLOCK = 64
SEL_TOPK = 16
SEL_LOCAL = 2
NSA_WINDOW = 512

SWA_DH = 64
SWA_KVH = 8
SWA_HPG = 8
SWA_WIDTH = 4096
SWA_WINDOW = 128

LANES = 128
VMEM_LIMIT_BYTES = 58 * 1024 * 1024

MM_TM = 1024
MM_TN = 1024
MM_TN_FUSED = 512
NORM_ROWS = 256
ATT_TQ = 128
NSA_TQ = 256
NSA_TK = 512

LOG2E = math.log2(math.e)


def _cparams(sem):
    return pltpu.CompilerParams(dimension_semantics=sem, vmem_limit_bytes=VMEM_LIMIT_BYTES)


def _sigmoid(v):
    return 1.0 / (1.0 + jnp.exp(-v))


def _silu(v):
    return v * _sigmoid(v)


def _dot(a, b):
    return jnp.dot(a, b, preferred_element_type=F32)


def _dot_nt(a, b):
    return lax.dot_general(a, b, (((1,), (1,)), ((), ())), preferred_element_type=F32)


def _lane_tile(x, n):
    return jnp.concatenate([x] * n, axis=1)


def _norm_kernel(h_ref, g_ref, *o_refs):
    x = h_ref[...]
    r = lax.rsqrt(jnp.mean(x * x, axis=-1, keepdims=True) + RMS_EPS)
    y = x * r
    for i, o_ref in enumerate(o_refs):
        o_ref[...] = (y * g_ref[i:i + 1, :]).astype(o_ref.dtype)


def rms_norm_rows(h, gains, out_dtype):
    t, d = h.shape
    k = gains.shape[0]
    outs = pl.pallas_call(
        _norm_kernel,
        grid=(t // NORM_ROWS,),
        in_specs=[pl.BlockSpec((NORM_ROWS, d), lambda i: (i, 0)),
                  pl.BlockSpec((k, d), lambda i: (0, 0))],
        out_specs=[pl.BlockSpec((NORM_ROWS, d), lambda i: (i, 0))] * k,
        out_shape=[jax.ShapeDtypeStruct((t, d), out_dtype)] * k,
        compiler_params=_cparams(("arbitrary",)),
        name="rms_norm",
    )(h, gains)
    return outs


def _rope_tables(seq, head_dim):
    rot = head_dim // 4
    half = rot // 2
    inv_freq = ROPE_THETA ** (-jnp.arange(half, dtype=F32) / half)
    ang = jnp.arange(seq, dtype=jnp.int32).astype(F32)[:, None] * inv_freq[None, :]
    cos, sin = jnp.cos(ang), jnp.sin(ang)
    rest = head_dim - rot
    zeros_h = jnp.zeros((seq, half), F32)
    c = jnp.concatenate([cos, cos, jnp.ones((seq, rest), F32)], axis=1)
    s1 = jnp.concatenate([-sin, zeros_h, jnp.zeros((seq, rest), F32)], axis=1)
    s2 = jnp.concatenate([zeros_h, sin, jnp.zeros((seq, rest), F32)], axis=1)
    reps = LANES // head_dim
    return tuple(jnp.tile(a, (1, reps)) for a in (c, s1, s2))


def _rope_lanes(t, c, s1, s2, half):
    return t * c + pltpu.roll(t, LANES - half, 1) * s1 + pltpu.roll(t, half, 1) * s2


def _row_rsqrt(s_ref, k):
    return lax.rsqrt(s_ref[...] * (1.0 / k) + RMS_EPS)


def _scaled_dot(x_ref, w_ref, s_ref):
    acc = _dot(x_ref[...], w_ref[...])
    if s_ref is not None:
        acc = acc * _lane_tile(_row_rsqrt(s_ref, x_ref.shape[1]), acc.shape[1] // LANES)
    return acc


def _write_norm_outputs(h, g_ref, hb_refs, ssq_ref):
    for i, hb_ref in enumerate(hb_refs):
        hb_ref[...] = (h * g_ref[i:i + 1, :]).astype(BF16)
    part = jnp.broadcast_to(jnp.sum(h * h, axis=-1, keepdims=True), ssq_ref.shape)
    j = pl.program_id(1)

    @pl.when(j == 0)
    def _():
        ssq_ref[...] = part

    @pl.when(j > 0)
    def _():
        ssq_ref[...] = ssq_ref[...] + part


def _mm_kernel(*refs, act, scaled):
    x_ref, w_ref = refs[0], refs[1]
    s_ref = refs[2] if scaled else None
    o_ref = refs[-1]
    acc = _scaled_dot(x_ref, w_ref, s_ref)
    if act == "silu":
        acc = _silu(acc)
    elif act == "sigmoid":
        acc = _sigmoid(acc)
    o_ref[...] = acc.astype(o_ref.dtype)


def _mm_zgate_kernel(x_ref, w_ref, g_ref, o_ref, *, first_col):
    acc = _silu(_dot(x_ref[...], w_ref[...]))
    for h in range(acc.shape[1] // LANES):
        sl = slice(h * LANES, (h + 1) * LANES)
        o_ref[:, sl] = acc[:, sl] * g_ref[:, first_col + h:first_col + h + 1]


def _mm_res_kernel(x_ref, w_ref, r_ref, g_ref, h_ref, *rest):
    h = r_ref[...] + _dot(x_ref[...], w_ref[...])
    h_ref[...] = h
    _write_norm_outputs(h, g_ref, rest[:-1], rest[-1])


def _ple_kernel(n_ref, wg_ref, s_ref, p_ref, wp_ref, h_ref, *rest, n_gain):
    gate = _sigmoid(_scaled_dot(n_ref, wg_ref, s_ref))
    emb = _dot(p_ref[...].astype(BF16), wp_ref[...])
    h = h_ref[...] + gate * emb
    if n_gain:
        g_ref, o_ref = rest[0], rest[1]
        o_ref[...] = h
        _write_norm_outputs(h, g_ref, rest[2:-1], rest[-1])
    else:
        rest[0][...] = h


def _mm_rope_kernel(*refs, half, n_rope, plain_too, scaled):
    x_ref, w_ref = refs[0], refs[1]
    s_ref = refs[2] if scaled else None
    c_ref, s1_ref, s2_ref = refs[3:6] if scaled else refs[2:5]
    o_refs = refs[6:] if scaled else refs[5:]
    acc = _scaled_dot(x_ref, w_ref, s_ref)
    c, s1, s2 = c_ref[...], s1_ref[...], s2_ref[...]
    for ch in range(acc.shape[1] // LANES):
        sl = slice(ch * LANES, (ch + 1) * LANES)
        t = acc[:, sl]
        o_refs[0][:, sl] = (_rope_lanes(t, c, s1, s2, half) if ch < n_rope else t).astype(BF16)
        if plain_too:
            o_refs[1][:, sl] = t.astype(BF16)


def _mm_kvb_kernel(x_ref, w_ref, s_ref, c_ref, s1_ref, s2_ref, o_ref, *, half):
    acc = _scaled_dot(x_ref, w_ref, s_ref)
    c, s1, s2 = c_ref[...], s1_ref[...], s2_ref[...]
    kvw = SWA_KVH * SWA_DH
    n_ch = kvw // LANES
    low = lax.broadcasted_iota(jnp.int32, (acc.shape[0], LANES), 1) < SWA_DH
    for kind in range(2):
        for ch in range(n_ch):
            t = acc[:, kind * kvw + ch * LANES: kind * kvw + (ch + 1) * LANES]
            if kind == 0:
                t = _rope_lanes(t, c, s1, s2, half)
            ts = pltpu.roll(t, SWA_DH, 1)
            variants = (jnp.where(low, t, 0.0), jnp.where(low, 0.0, ts),
                        jnp.where(low, ts, 0.0), jnp.where(low, 0.0, t))
            for var, val in enumerate(variants):
                col = (kind * 4 + var) * kvw + ch * LANES
                o_ref[:, col:col + LANES] = val.astype(BF16)


def _mm_tiles(m, n, tn_max=MM_TN):
    tm = min(MM_TM, m)
    tn = min(tn_max, n)
    assert m % tm == 0 and n % tn == 0, (m, n)
    return tm, tn


def _ssq_spec(tm):
    return pl.BlockSpec((tm, LANES), lambda i, j: (i, 0))


def matmul(x, w, name, act=None, out_dtype=F32, ssq=None):
    m, k = x.shape
    n = w.shape[1]
    tm, tn = _mm_tiles(m, n)
    scaled = ssq is not None
    return pl.pallas_call(
        functools.partial(_mm_kernel, act=act, scaled=scaled),
        grid=(m // tm, n // tn),
        in_specs=[pl.BlockSpec((tm, k), lambda i, j: (i, 0)),
                  pl.BlockSpec((k, tn), lambda i, j: (0, j))] + ([_ssq_spec(tm)] if scaled else []),
        out_specs=pl.BlockSpec((tm, tn), lambda i, j: (i, j)),
        out_shape=jax.ShapeDtypeStruct((m, n), out_dtype),
        compiler_params=_cparams(("arbitrary", "arbitrary")),
        name=name,
    )(x, w, *([ssq] if scaled else []))


def matmul_zgate(x, w, gates, branch, name):
    m, k = x.shape
    n = w.shape[1]
    tm, tn = _mm_tiles(m, n)
    assert tn == NSA_HPG * NSA_DH
    return pl.pallas_call(
        functools.partial(_mm_zgate_kernel, first_col=branch * NSA_HPG),
        grid=(m // tm, n // tn),
        in_specs=[pl.BlockSpec((tm, k), lambda i, j: (i, 0)),
                  pl.BlockSpec((k, tn), lambda i, j: (0, j)),
                  pl.BlockSpec((tm, LANES), lambda i, j: (i, j))],
        out_specs=pl.BlockSpec((tm, tn), lambda i, j: (i, j)),
        out_shape=jax.ShapeDtypeStruct((m, n), F32),
        compiler_params=_cparams(("arbitrary", "arbitrary")),
        name=name,
    )(x, w, gates)


def matmul_rope(x, w, tabs, seq, head_dim, rope_cols, plain_too, name, ssq=None):
    m, k = x.shape
    n = w.shape[1]
    tm, tn = _mm_tiles(m, n)
    assert seq % tm == 0
    per_seq = seq // tm
    scaled = ssq is not None
    tab_spec = pl.BlockSpec((tm, LANES), lambda i, j: (i % per_seq, 0))
    n_out = 2 if plain_too else 1
    outs = pl.pallas_call(
        functools.partial(_mm_rope_kernel, half=head_dim // 8, n_rope=rope_cols // LANES,
                          plain_too=plain_too, scaled=scaled),
        grid=(m // tm, n // tn),
        in_specs=[pl.BlockSpec((tm, k), lambda i, j: (i, 0)),
                  pl.BlockSpec((k, tn), lambda i, j: (0, j))]
        + ([_ssq_spec(tm)] if scaled else []) + [tab_spec] * 3,
        out_specs=[pl.BlockSpec((tm, tn), lambda i, j: (i, j))] * n_out,
        out_shape=[jax.ShapeDtypeStruct((m, n), BF16)] * n_out,
        compiler_params=_cparams(("arbitrary", "arbitrary")),
        name=name,
    )(x, w, *([ssq] if scaled else []), *tabs)
    return outs


def matmul_kvb(x, w, ssq, tabs, seq, name):
    m, k = x.shape
    n = w.shape[1]
    tm = min(MM_TM, m)
    assert seq % tm == 0 and n == 2 * SWA_KVH * SWA_DH
    per_seq = seq // tm
    tab_spec = pl.BlockSpec((tm, LANES), lambda i: (i % per_seq, 0))
    return pl.pallas_call(
        functools.partial(_mm_kvb_kernel, half=SWA_DH // 8),
        grid=(m // tm,),
        in_specs=[pl.BlockSpec((tm, k), lambda i: (i, 0)),
                  pl.BlockSpec((k, n), lambda i: (0, 0)),
                  pl.BlockSpec((tm, LANES), lambda i: (i, 0))] + [tab_spec] * 3,
        out_specs=pl.BlockSpec((tm, 4 * n), lambda i: (i, 0)),
        out_shape=jax.ShapeDtypeStruct((m, 4 * n), BF16),
        compiler_params=_cparams(("arbitrary",)),
        name=name,
    )(x, w, ssq, *tabs)


def matmul_residual(x, w, res, gains, name):
    m, k = x.shape
    n = w.shape[1]
    kg = gains.shape[0]
    tm, tn = _mm_tiles(m, n, MM_TN_FUSED)
    tile = pl.BlockSpec((tm, tn), lambda i, j: (i, j))
    outs = pl.pallas_call(
        _mm_res_kernel,
        grid=(m // tm, n // tn),
        in_specs=[pl.BlockSpec((tm, k), lambda i, j: (i, 0)),
                  pl.BlockSpec((k, tn), lambda i, j: (0, j)),
                  tile,
                  pl.BlockSpec((kg, tn), lambda i, j: (0, j))],
        out_specs=[tile] + [tile] * kg + [_ssq_spec(tm)],
        out_shape=[jax.ShapeDtypeStruct((m, n), F32)] + [jax.ShapeDtypeStruct((m, n), BF16)] * kg
        + [jax.ShapeDtypeStruct((m, LANES), F32)],
        compiler_params=_cparams(("arbitrary", "arbitrary")),
        name=name,
    )(x, w, res, gains)
    return outs[0], outs[1:-1], outs[-1]


def ple_update(n, ssq, wg, p, wp, h, gains, name):
    m, k = n.shape
    d = wg.shape[1]
    kp = p.shape[1]
    kg = 0 if gains is None else gains.shape[0]
    tm, tn = _mm_tiles(m, d, MM_TN_FUSED)
    tile = pl.BlockSpec((tm, tn), lambda i, j: (i, j))
    in_specs = [pl.BlockSpec((tm, k), lambda i, j: (i, 0)),
                pl.BlockSpec((k, tn), lambda i, j: (0, j)),
                _ssq_spec(tm),
                pl.BlockSpec((tm, kp), lambda i, j: (i, 0)),
                pl.BlockSpec((kp, tn), lambda i, j: (0, j)),
                tile]
    args = [n, wg, ssq, p, wp, h]
    out_specs = [tile]
    out_shape = [jax.ShapeDtypeStruct((m, d), F32)]
    if kg:
        in_specs.append(pl.BlockSpec((kg, tn), lambda i, j: (0, j)))
        args.append(gains)
        out_specs += [tile] * kg + [_ssq_spec(tm)]
        out_shape += [jax.ShapeDtypeStruct((m, d), BF16)] * kg + [jax.ShapeDtypeStruct((m, LANES), F32)]
    outs = pl.pallas_call(
        functools.partial(_ple_kernel, n_gain=kg),
        grid=(m // tm, d // tn),
        in_specs=in_specs,
        out_specs=out_specs,
        out_shape=out_shape,
        compiler_params=_cparams(("arbitrary", "arbitrary")),
        name=name,
    )(*args)
    if kg:
        return outs[0], outs[1:-1], outs[-1]
    return outs[0]


def _compress_kernel(x_ref, pos_ref, w1_ref, w2_ref, o_ref, *, n_cmp_pad):
    nc = n_cmp_pad
    lo = jnp.zeros((nc, CMP_HIDDEN), F32)
    hi = jnp.zeros((nc, CMP_HIDDEN), F32)
    for l in range(CMP_STRIDE):
        xl = x_ref[pl.ds(l, nc, stride=CMP_STRIDE), :]
        a = (xl + pos_ref[0, l:l + 1, :]).astype(BF16)
        b = (xl + pos_ref[0, CMP_STRIDE + l:CMP_STRIDE + l + 1, :]).astype(BF16)
        lo = lo + _dot(a, w1_ref[0, l * LANES:(l + 1) * LANES, :])
        hi = hi + _dot(b, w1_ref[0, (CMP_STRIDE + l) * LANES:(CMP_STRIDE + l + 1) * LANES, :])
    hid = lo + pltpu.roll(hi, nc - 1, 0)
    out = _dot(_silu(hid).astype(BF16), w2_ref[0])
    n_idx = lax.broadcasted_iota(jnp.int32, (nc, LANES), 0)
    o_ref[0, 0, 0] = jnp.where(n_idx < nc - 1, out, 0.0).astype(BF16)


def compress_kv(kcvc, pos, w1, w2, batch, seq):
    nc = seq // CMP_STRIDE
    return pl.pallas_call(
        functools.partial(_compress_kernel, n_cmp_pad=nc),
        grid=(batch, 2, NSA_G),
        in_specs=[pl.BlockSpec((seq, LANES), lambda b, s, g: (b, s * NSA_G + g)),
                  pl.BlockSpec((1, CMP_LEN, LANES), lambda b, s, g: (s, 0, 0)),
                  pl.BlockSpec((1, CMP_LEN * LANES, CMP_HIDDEN), lambda b, s, g: (s, 0, 0)),
                  pl.BlockSpec((1, CMP_HIDDEN, LANES), lambda b, s, g: (s, 0, 0))],
        out_specs=pl.BlockSpec((1, 1, 1, nc, LANES), lambda b, s, g: (b, s, g, 0, 0)),
        out_shape=jax.ShapeDtypeStruct((batch, 2, NSA_G, nc, LANES), BF16),
        compiler_params=_cparams(("arbitrary", "arbitrary", "arbitrary")),
        name="compress_kv",
    )(kcvc, pos, w1, w2)


def _nsa_kernel(qp_ref, qr_ref, z0_ref, z1_ref, z2_ref,
                kc_ref, vc_ref, ks_ref, vs_ref, kw_ref, vw_ref, et_ref, mt_ref,
                o_ref, qa_scr, m_scr, acc_scr, mix_scr, *, seq):
    tq, tk, hpg = NSA_TQ, NSA_TK, NSA_HPG
    rows = hpg * tq
    nc = seq // CMP_STRIDE
    ns = seq // SEL_BLOCK
    n_top = min(SEL_TOPK, ns)
    nsp = mt_ref.shape[0]
    c_exp = LOG2E / math.sqrt(NSA_DH)
    t0 = pl.program_id(2) * tq
    ones_k = jnp.ones((tk, LANES), BF16)

    t_col = t0 + lax.broadcasted_iota(jnp.int32, (tq, 1), 0)

    n_idx = lax.broadcasted_iota(jnp.int32, (1, nc), 1)
    valid = (n_idx * CMP_STRIDE + (CMP_LEN - 1) <= t_col) & (n_idx < nc - 1)
    bias_c = jnp.where(valid, 0.0, NEG_BIG)
    row_ok = t0 + lax.broadcasted_iota(jnp.int32, (tq, LANES), 0) >= CMP_LEN - 1
    lane_ok = t0 + lax.broadcasted_iota(jnp.int32, (ns, tq), 1) >= CMP_LEN - 1
    kc = kc_ref[0, 0, 0]
    vc_aug = jnp.concatenate([vc_ref[0, 0, 0], jnp.ones((nc, LANES), BF16)], axis=1)
    span = NSA_WINDOW + tq
    kw0 = pl.multiple_of(jnp.maximum(t0 - NSA_WINDOW, 0), tq)
    kwin = kw_ref[pl.ds(kw0, span), :]
    vw_aug = jnp.concatenate([vw_ref[pl.ds(kw0, span), :], jnp.ones((span, LANES), BF16)], axis=1)
    kpos_w = kw0 + lax.broadcasted_iota(jnp.int32, (1, span), 1)
    bias_w = jnp.where((kpos_w <= t_col) & (kpos_w > t_col - NSA_WINDOW), 0.0, NEG_BIG)
    imp_parts = []
    for h in range(hpg):
        sl = slice(h * LANES, (h + 1) * LANES)
        sc = _dot_nt(qp_ref[:, sl], kc) + bias_c
        m = jnp.max(sc, axis=-1, keepdims=True)
        pe = jnp.exp2((sc - m) * c_exp).astype(BF16)
        acc = _dot(pe, vc_aug)
        oc = jnp.where(row_ok, acc[:, 0:LANES] * (1.0 / acc[:, LANES:2 * LANES]), 0.0)
        imp_t = _dot_nt(mt_ref[...], pe)
        imp_parts.append(imp_t[0:ns, :] * (1.0 / imp_t[nsp - 1:nsp, :]))
        sc = _dot_nt(qr_ref[:, sl], kwin) + bias_w
        m = jnp.max(sc, axis=-1, keepdims=True)
        pe = jnp.exp2((sc - m) * c_exp).astype(BF16)
        acc = _dot(pe, vw_aug)
        ow = acc[:, 0:LANES] * (1.0 / acc[:, LANES:2 * LANES])
        mix_scr[:, sl] = oc * z0_ref[:, sl] + ow * z2_ref[:, sl]
    while len(imp_parts) > 1:
        imp_parts = [imp_parts[i] + imp_parts[i + 1] for i in range(0, len(imp_parts), 2)]
    imp = jnp.where(lane_ok, imp_parts[0], 0.0)

    j_io = lax.broadcasted_iota(jnp.int32, (ns, tq), 0)
    t_io = t0 + lax.broadcasted_iota(jnp.int32, (ns, tq), 1)
    dist = lax.shift_right_logical(t_io, int(math.log2(SEL_BLOCK))) - j_io
    forced = (j_io == 0) | ((dist >= 0) & (dist < SEL_LOCAL))
    imp = jnp.where(forced, FORCE_SCORE, imp)
    imp = jnp.where(dist >= 0, imp, -1.0)
    sub = 8
    ranks = []
    for v in range(ns // sub):
        blk = imp[v * sub:(v + 1) * sub, :]
        j_v = v * sub + lax.broadcasted_iota(jnp.int32, (sub, tq), 0)
        r = jnp.zeros((sub, tq), F32)
        for k in range(ns):
            rk = imp[k:k + 1, :]
            if k < v * sub:
                ahead = rk >= blk
            elif k >= (v + 1) * sub:
                ahead = rk > blk
            else:
                ahead = (rk > blk) | ((rk == blk) & (j_v > k))
            r = r + jnp.where(ahead, 1.0, 0.0)
        ranks.append(r)
    rank = jnp.concatenate(ranks, axis=0)
    not_sel_t = jnp.concatenate([jnp.where(rank < n_top, 0.0, 1.0), jnp.ones((nsp - ns, tq), F32)], axis=0)
    not_sel = not_sel_t.T.astype(BF16)

    for h in range(hpg):
        qa_scr[h * tq:(h + 1) * tq, 0:LANES] = qr_ref[:, h * LANES:(h + 1) * LANES]
        qa_scr[h * tq:(h + 1) * tq, LANES:2 * LANES] = not_sel

    m_scr[...] = jnp.full((rows, LANES), NEG_BIG, F32)
    acc_scr[...] = jnp.zeros((rows, 2 * LANES), F32)
    n_kt = lax.shift_right_logical(t0 + (tq + tk - 1), int(math.log2(tk)))

    def slc_tile(kt, causal):
        k0 = pl.multiple_of(kt * tk, tk)
        k_aug = jnp.concatenate([ks_ref[pl.ds(k0, tk), :], et_ref[pl.ds(k0, tk), :]], axis=1)
        v_aug = jnp.concatenate([vs_ref[pl.ds(k0, tk), :], ones_k], axis=1)
        if causal:
            kpos = k0 + lax.broadcasted_iota(jnp.int32, (1, tk), 1)
            bias = jnp.where(kpos <= t_col, 0.0, NEG_BIG)
        for h in range(hpg):
            rs = slice(h * tq, (h + 1) * tq)
            sc = _dot_nt(qa_scr[rs, :], k_aug)
            if causal:
                sc = sc + bias
            m_old = m_scr[rs, :]
            m_new = jnp.maximum(m_old, jnp.max(sc, axis=-1, keepdims=True))
            alpha = jnp.exp2((m_old - m_new) * c_exp)
            pe = jnp.exp2((sc - _lane_tile(m_new, tk // LANES)) * c_exp)
            acc_scr[rs, :] = _lane_tile(alpha, 2) * acc_scr[rs, :] + _dot(pe.astype(BF16), v_aug)
            m_scr[rs, :] = m_new

    def pair_body(i, carry):
        slc_tile(2 * i, causal=False)
        slc_tile(2 * i + 1, causal=False)
        return carry

    lax.fori_loop(0, lax.shift_right_logical(n_kt - 1, 1), pair_body, 0)
    odd = ((n_kt - 1) & 1) == 1

    @pl.when(odd)
    def _():
        slc_tile(n_kt - 2, causal=False)
        slc_tile(n_kt - 1, causal=True)

    @pl.when(jnp.logical_not(odd))
    def _():
        slc_tile(n_kt - 1, causal=True)

    for h in range(hpg):
        rs = slice(h * tq, (h + 1) * tq)
        sl = slice(h * LANES, (h + 1) * LANES)
        o_slc = acc_scr[rs, 0:LANES] * (1.0 / acc_scr[rs, LANES:2 * LANES])
        o_ref[:, sl] = (mix_scr[:, sl] + o_slc * z1_ref[:, sl]).astype(BF16)


def nsa_attention(qp, qr, zg, cmp_kv, kv, e_t, cmp_to_sel_t, batch, seq):
    t = qp.shape[0]
    tq = NSA_TQ
    assert seq % NSA_TK == 0 and NSA_TK % tq == 0 and seq >= NSA_WINDOW + tq
    qt_per_seq = seq // tq
    nc = seq // CMP_STRIDE
    hw = NSA_HPG * NSA_DH

    qz_spec = pl.BlockSpec((tq, hw), lambda b, g, i: (b * qt_per_seq + i, g))
    cmp_specs = [pl.BlockSpec((1, 1, 1, nc, LANES), functools.partial(
        lambda b, g, i, s: (b, s, g, 0, 0), s=s)) for s in range(2)]
    kv_specs = [pl.BlockSpec((seq, LANES), functools.partial(
        lambda b, g, i, s: (b, s * NSA_G + g), s=s)) for s in range(4)]
    const_specs = [pl.BlockSpec(e_t.shape, lambda b, g, i: (0, 0)),
                   pl.BlockSpec(cmp_to_sel_t.shape, lambda b, g, i: (0, 0))]
    rows = NSA_HPG * tq
    return pl.pallas_call(
        functools.partial(_nsa_kernel, seq=seq),
        grid=(batch, NSA_G, qt_per_seq),
        in_specs=[qz_spec] * 5 + cmp_specs + kv_specs + const_specs,
        out_specs=qz_spec,
        out_shape=jax.ShapeDtypeStruct((t, NSA_WIDTH), BF16),
        scratch_shapes=[pltpu.VMEM((rows, 2 * LANES), BF16),
                        pltpu.VMEM((rows, LANES), F32),
                        pltpu.VMEM((rows, 2 * LANES), F32),
                        pltpu.VMEM((tq, hw), F32)],
        compiler_params=_cparams(("arbitrary", "arbitrary", "arbitrary")),
        name="nsa_attention",
    )(qp, qr, *zg, cmp_kv, cmp_kv, kv, kv, kv, kv, e_t, cmp_to_sel_t)


def _swa_kernel(sink_ref, q_ref, z_ref, *rest):
    kv_refs, o_ref = rest[:-1], rest[-1]
    tq = ATT_TQ
    n_pair = SWA_KVH // 2
    per_kv = SWA_HPG * SWA_DH // LANES
    scale = 1.0 / math.sqrt(SWA_DH)
    c_exp = LOG2E * scale
    t0 = pl.program_id(1) * tq
    ones_k = jnp.ones((2 * tq, LANES), BF16)

    def window(kind, var, pair):
        base = (kind * 4 + var) * 2
        sl = slice(pair * LANES, (pair + 1) * LANES)
        return jnp.concatenate([kv_refs[base][:, sl], kv_refs[base + 1][:, sl]], axis=0)

    t_col = t0 + lax.broadcasted_iota(jnp.int32, (tq, 1), 0)
    kpos = t0 - tq + lax.broadcasted_iota(jnp.int32, (1, 2 * tq), 1)
    ok = (kpos <= t_col) & (kpos > t_col - SWA_WINDOW) & (kpos >= 0)
    bias = jnp.where(ok, 0.0, NEG_BIG)

    for pair in range(n_pair):
        for j in range(2):
            k_var = [window(0, 2 * j + par, pair) for par in range(2)]
            v_var = [jnp.concatenate([window(1, 2 * j + par, pair), ones_k], axis=1) for par in range(2)]
            for c in range(per_kv):
                col = ((pair * 2 + j) * per_kv + c) * LANES
                qc = q_ref[:, col:col + LANES]
                o_chunk = None
                for par in range(2):
                    head = (pair * 2 + j) * SWA_HPG + 2 * c + par
                    sink_raw = sink_ref[head] * math.sqrt(SWA_DH)
                    sc = _dot_nt(qc, k_var[par]) + bias
                    m = jnp.maximum(jnp.max(sc, axis=-1, keepdims=True), sink_raw)
                    e = jnp.exp2((sc - m) * c_exp)
                    acc = _dot(e.astype(BF16), v_var[par])
                    denom = acc[:, LANES:2 * LANES] + jnp.exp2((sink_raw - m) * c_exp)
                    o_par = acc[:, 0:LANES] * (1.0 / denom)
                    o_chunk = o_par if o_chunk is None else o_chunk + o_par
                o_ref[:, col:col + LANES] = (o_chunk * z_ref[:, col:col + LANES]).astype(BF16)


def swa_attention(q, zs, kvb, sinks, batch, seq):
    t = q.shape[0]
    tq = ATT_TQ
    qt_per_seq = seq // tq
    kvw = SWA_KVH * SWA_DH

    def row(b, i):
        return b * qt_per_seq + i

    def prev_row(b, i):
        return b * qt_per_seq + jnp.maximum(i - 1, 0)

    qz_spec = pl.BlockSpec((tq, SWA_WIDTH), lambda b, i: (row(b, i), 0))
    kv_specs = []
    for cb in range(8):
        kv_specs.append(pl.BlockSpec((tq, kvw), functools.partial(lambda b, i, cb: (prev_row(b, i), cb), cb=cb)))
        kv_specs.append(pl.BlockSpec((tq, kvw), functools.partial(lambda b, i, cb: (row(b, i), cb), cb=cb)))
    return pl.pallas_call(
        _swa_kernel,
        grid=(batch, qt_per_seq),
        in_specs=[pl.BlockSpec(memory_space=pltpu.SMEM), qz_spec, qz_spec] + kv_specs,
        out_specs=qz_spec,
        out_shape=jax.ShapeDtypeStruct((t, SWA_WIDTH), BF16),
        compiler_params=_cparams(("arbitrary", "arbitrary")),
        name="swa_attention",
    )(sinks, q, zs, *([kvb] * 16))


def _selection_constants(seq):
    nc = seq // CMP_STRIDE
    ns = seq // SEL_BLOCK
    nsp = -(-ns // LANES) * LANES
    e_t = np.zeros((seq, nsp), np.float32)
    e_t[np.arange(seq), np.arange(seq) // SEL_BLOCK] = NEG_BIG
    c0 = np.arange(nc - 1) * CMP_STRIDE
    s0 = np.arange(ns) * SEL_BLOCK
    overlap = (c0[:, None] < s0[None, :] + SEL_BLOCK) & (c0[:, None] + CMP_LEN > s0[None, :])
    assert ns < nsp
    cmp_to_sel_t = np.zeros((nsp, nc), np.float32)
    cmp_to_sel_t[:ns, :nc - 1] = overlap.T
    cmp_to_sel_t[nsp - 1, :] = 1.0
    return jnp.asarray(e_t, BF16), jnp.asarray(cmp_to_sel_t, BF16)


def kernel(x, p, a_norm, a_w_in, a_w_out, a_cmp_pos_k, a_cmp_w1_k, a_cmp_w2_k, a_cmp_pos_v, a_cmp_w1_v,
           a_cmp_w2_v, kv_norm, w_kv, b_norm, b_w_in, b_w_out, b_sinks, ple_norm, ple_gate_w, ple_proj,
           final_norm):
    batch, seq, d = x.shape
    assert d == D_MODEL and p.shape[0] == 2 and a_w_in.shape[0] == 1 and b_w_in.shape[0] == 1
    t = batch * seq
    xf = x.reshape(t, d)

    wa = a_w_in[0]
    c_q, c_cmp, c_kv = NSA_WIDTH, NSA_WIDTH + 2 * NSA_KVW, NSA_WIDTH + 6 * NSA_KVW
    n_gate = 3 * NSA_G * NSA_HPG
    w_q = wa[:, :c_q].astype(BF16)
    w_cmp = wa[:, c_q:c_cmp].astype(BF16)
    w_kvs = wa[:, c_cmp:c_kv].astype(BF16)
    w_z = [wa[:, c_kv + n_gate + br * NSA_WIDTH:c_kv + n_gate + (br + 1) * NSA_WIDTH].astype(BF16)
           for br in range(3)]
    wg = wa[:, c_kv:c_kv + n_gate].reshape(d, 3, NSA_G, NSA_HPG).transpose(0, 2, 1, 3)
    wg = wg.reshape(d, NSA_G, 3 * NSA_HPG)
    wg = jnp.pad(wg, ((0, 0), (0, 0), (0, LANES - 3 * NSA_HPG))).reshape(d, NSA_G * LANES).astype(BF16)
    pos_kv = jnp.stack([a_cmp_pos_k[0], a_cmp_pos_v[0]])
    w1_kv = jnp.stack([a_cmp_w1_k[0], a_cmp_w1_v[0]]).astype(BF16)
    w2_kv = jnp.stack([a_cmp_w2_k[0], a_cmp_w2_v[0]]).astype(BF16)
    wb = b_w_in[0]
    tabs_a = _rope_tables(seq, NSA_DH)
    tabs_b = _rope_tables(seq, SWA_DH)
    e_t, cmp_to_sel_t = _selection_constants(seq)

    (n_a,) = rms_norm_rows(xf, a_norm, BF16)
    q_rot, q_plain = matmul_rope(n_a, w_q, tabs_a, seq, NSA_DH, MM_TN, True, "a_q_proj")
    (kv_a,) = matmul_rope(n_a, w_kvs, tabs_a, seq, NSA_DH, NSA_KVW, False, "a_kv_proj")
    kcvc = matmul(n_a, w_cmp, "a_cmp_proj")
    gates = matmul(n_a, wg, "a_gate_proj", act="sigmoid")
    zg = [matmul_zgate(n_a, w_z[br], gates, br, "a_z_proj%d" % br) for br in range(3)]
    cmp_kv = compress_kv(kcvc, pos_kv, w1_kv, w2_kv, batch, seq)
    mixed = nsa_attention(q_plain, q_rot, zg, cmp_kv, kv_a, e_t, cmp_to_sel_t, batch, seq)
    h, (n_p,), ssq = matmul_residual(mixed, a_w_out[0].astype(BF16), xf, ple_norm[0:1], "a_out_proj")
    pf = p.reshape(2, t, PLE_DIM)
    h, (n_kv, n_b), ssq = ple_update(n_p, ssq, ple_gate_w[0].astype(BF16), pf[0], ple_proj[0].astype(BF16), h,
                                     jnp.stack([kv_norm, b_norm[0]]), "ple0")

    kvb = matmul_kvb(n_kv, w_kv.astype(BF16), ssq, tabs_b, seq, "kv_proj")
    (q_b,) = matmul_rope(n_b, wb[:, :SWA_WIDTH].astype(BF16), tabs_b, seq, SWA_DH, MM_TN, False, "b_q_proj",
                         ssq=ssq)
    zs_b = matmul(n_b, wb[:, SWA_WIDTH:].astype(BF16), "b_z_proj", act="silu", ssq=ssq)
    ob = swa_attention(q_b, zs_b, kvb, b_sinks[0], batch, seq)
    h, (n_p,), ssq = matmul_residual(ob, b_w_out[0].astype(BF16), h, ple_norm[1:2], "b_out_proj")
    h = ple_update(n_p, ssq, ple_gate_w[1].astype(BF16), pf[1], ple_proj[1].astype(BF16), h, None, "ple1")

    (out,) = rms_norm_rows(h, final_norm[None, :], F32)
    return out.reshape(batch, seq, d)
```

```python
import functools
import math

import numpy as np
import jax
import jax.numpy as jnp
from jax import lax
from jax.experimental import pallas as pl
from jax.experimental.pallas import tpu as pltpu

F32 = jnp.float32
BF16 = jnp.bfloat16

D_MODEL = 4096
RMS_EPS = 1e-6
ROPE_THETA = 500000.0
NEG_BIG = -1e30
FORCE_SCORE = 1e9
PLE_DIM = 256

NSA_DH = 128
NSA_G = 4
NSA_HPG = 8
NSA_WIDTH = 4096
NSA_KVW = 512
CMP_LEN = 32
CMP_STRIDE = 16
CMP_HIDDEN = 256
SEL_BLOCK = 64
SEL_TOPK = 16
SEL_LOCAL = 2
NSA_WINDOW = 512

SWA_DH = 64
SWA_KVH = 8
SWA_HPG = 8
SWA_WIDTH = 4096
SWA_WINDOW = 128

LANES = 128
VMEM_LIMIT_BYTES = 58 * 1024 * 1024

MM_TM = 1024
MM_TN = 1024
MM_TN_FUSED = 512
NORM_ROWS = 256
ATT_TQ = 128
NSA_TQ = 256
NSA_TK = 512

LOG2E = math.log2(math.e)


def _cparams(sem):
    return pltpu.CompilerParams(dimension_semantics=sem, vmem_limit_bytes=VMEM_LIMIT_BYTES)


def _sigmoid(v):
    return 1.0 / (1.0 + jnp.exp(-v))


def _silu(v):
    return v * _sigmoid(v)


def _dot(a, b):
    return jnp.dot(a, b, preferred_element_type=F32)


def _dot_nt(a, b):
    return lax.dot_general(a, b, (((1,), (1,)), ((), ())), preferred_element_type=F32)


def _lane_tile(x, n):
    return jnp.concatenate([x] * n, axis=1)


def _norm_kernel(h_ref, g_ref, *o_refs):
    x = h_ref[...]
    r = lax.rsqrt(jnp.mean(x * x, axis=-1, keepdims=True) + RMS_EPS)
    y = x * r
    for i, o_ref in enumerate(o_refs):
        o_ref[...] = (y * g_ref[i:i + 1, :]).astype(o_ref.dtype)


def rms_norm_rows(h, gains, out_dtype):
    t, d = h.shape
    k = gains.shape[0]
    outs = pl.pallas_call(
        _norm_kernel,
        grid=(t // NORM_ROWS,),
        in_specs=[pl.BlockSpec((NORM_ROWS, d), lambda i: (i, 0)),
                  pl.BlockSpec((k, d), lambda i: (0, 0))],
        out_specs=[pl.BlockSpec((NORM_ROWS, d), lambda i: (i, 0))] * k,
        out_shape=[jax.ShapeDtypeStruct((t, d), out_dtype)] * k,
        compiler_params=_cparams(("arbitrary",)),
        name="rms_norm",
    )(h, gains)
    return outs


def _rope_tables(seq, head_dim):
    rot = head_dim // 4
    half = rot // 2
    inv_freq = ROPE_THETA ** (-jnp.arange(half, dtype=F32) / half)
    ang = jnp.arange(seq, dtype=jnp.int32).astype(F32)[:, None] * inv_freq[None, :]
    cos, sin = jnp.cos(ang), jnp.sin(ang)
    rest = head_dim - rot
    zeros_h = jnp.zeros((seq, half), F32)
    c = jnp.concatenate([cos, cos, jnp.ones((seq, rest), F32)], axis=1)
    s1 = jnp.concatenate([-sin, zeros_h, jnp.zeros((seq, rest), F32)], axis=1)
    s2 = jnp.concatenate([zeros_h, sin, jnp.zeros((seq, rest), F32)], axis=1)
    reps = LANES // head_dim
    return tuple(jnp.tile(a, (1, reps)) for a in (c, s1, s2))


def _rope_lanes(t, c, s1, s2, half):
    return t * c + pltpu.roll(t, LANES - half, 1) * s1 + pltpu.roll(t, half, 1) * s2


def _row_rsqrt(s_ref, k):
    return lax.rsqrt(s_ref[...] * (1.0 / k) + RMS_EPS)


def _scaled_dot(x_ref, w_ref, s_ref):
    acc = _dot(x_ref[...], w_ref[...])
    if s_ref is not None:
        acc = acc * _lane_tile(_row_rsqrt(s_ref, x_ref.shape[1]), acc.shape[1] // LANES)
    return acc


def _write_norm_outputs(h, g_ref, hb_refs, ssq_ref):
    for i, hb_ref in enumerate(hb_refs):
        hb_ref[...] = (h * g_ref[i:i + 1, :]).astype(BF16)
    part = jnp.broadcast_to(jnp.sum(h * h, axis=-1, keepdims=True), ssq_ref.shape)
    j = pl.program_id(1)

    @pl.when(j == 0)
    def _():
        ssq_ref[...] = part

    @pl.when(j > 0)
    def _():
        ssq_ref[...] = ssq_ref[...] + part


def _mm_kernel(*refs, act, scaled):
    x_ref, w_ref = refs[0], refs[1]
    s_ref = refs[2] if scaled else None
    o_ref = refs[-1]
    acc = _scaled_dot(x_ref, w_ref, s_ref)
    if act == "silu":
        acc = _silu(acc)
    elif act == "sigmoid":
        acc = _sigmoid(acc)
    o_ref[...] = acc.astype(o_ref.dtype)


def _mm_zgate_kernel(x_ref, w_ref, g_ref, o_ref, *, first_col):
    acc = _silu(_dot(x_ref[...], w_ref[...]))
    for h in range(acc.shape[1] // LANES):
        sl = slice(h * LANES, (h + 1) * LANES)
        o_ref[:, sl] = acc[:, sl] * g_ref[:, first_col + h:first_col + h + 1]


def _mm_res_kernel(x_ref, w_ref, r_ref, g_ref, h_ref, *rest):
    h = r_ref[...] + _dot(x_ref[...], w_ref[...])
    h_ref[...] = h
    _write_norm_outputs(h, g_ref, rest[:-1], rest[-1])


def _ple_kernel(n_ref, wg_ref, s_ref, p_ref, wp_ref, h_ref, *rest, n_gain):
    gate = _sigmoid(_scaled_dot(n_ref, wg_ref, s_ref))
    emb = _dot(p_ref[...].astype(BF16), wp_ref[...])
    h = h_ref[...] + gate * emb
    if n_gain:
        g_ref, o_ref = rest[0], rest[1]
        o_ref[...] = h
        _write_norm_outputs(h, g_ref, rest[2:-1], rest[-1])
    else:
        rest[0][...] = h


def _mm_rope_kernel(*refs, half, n_rope, plain_too, scaled):
    x_ref, w_ref = refs[0], refs[1]
    s_ref = refs[2] if scaled else None
    c_ref, s1_ref, s2_ref = refs[3:6] if scaled else refs[2:5]
    o_refs = refs[6:] if scaled else refs[5:]
    acc = _scaled_dot(x_ref, w_ref, s_ref)
    c, s1, s2 = c_ref[...], s1_ref[...], s2_ref[...]
    for ch in range(acc.shape[1] // LANES):
        sl = slice(ch * LANES, (ch + 1) * LANES)
        t = acc[:, sl]
        o_refs[0][:, sl] = (_rope_lanes(t, c, s1, s2, half) if ch < n_rope else t).astype(BF16)
        if plain_too:
            o_refs[1][:, sl] = t.astype(BF16)


def _mm_kvb_kernel(x_ref, w_ref, s_ref, c_ref, s1_ref, s2_ref, o_ref, *, half):
    acc = _scaled_dot(x_ref, w_ref, s_ref)
    c, s1, s2 = c_ref[...], s1_ref[...], s2_ref[...]
    kvw = SWA_KVH * SWA_DH
    n_ch = kvw // LANES
    low = lax.broadcasted_iota(jnp.int32, (acc.shape[0], LANES), 1) < SWA_DH
    for kind in range(2):
        for ch in range(n_ch):
            t = acc[:, kind * kvw + ch * LANES: kind * kvw + (ch + 1) * LANES]
            if kind == 0:
                t = _rope_lanes(t, c, s1, s2, half)
            ts = pltpu.roll(t, SWA_DH, 1)
            variants = (jnp.where(low, t, 0.0), jnp.where(low, 0.0, ts),
                        jnp.where(low, ts, 0.0), jnp.where(low, 0.0, t))
            for var, val in enumerate(variants):
                col = (kind * 4 + var) * kvw + ch * LANES
                o_ref[:, col:col + LANES] = val.astype(BF16)


def _mm_tiles(m, n, tn_max=MM_TN):
    tm = min(MM_TM, m)
    tn = min(tn_max, n)
    assert m % tm == 0 and n % tn == 0, (m, n)
    return tm, tn


def _ssq_spec(tm):
    return pl.BlockSpec((tm, LANES), lambda i, j: (i, 0))


def matmul(x, w, name, act=None, out_dtype=F32, ssq=None):
    m, k = x.shape
    n = w.shape[1]
    tm, tn = _mm_tiles(m, n)
    scaled = ssq is not None
    return pl.pallas_call(
        functools.partial(_mm_kernel, act=act, scaled=scaled),
        grid=(m // tm, n // tn),
        in_specs=[pl.BlockSpec((tm, k), lambda i, j: (i, 0)),
                  pl.BlockSpec((k, tn), lambda i, j: (0, j))] + ([_ssq_spec(tm)] if scaled else []),
        out_specs=pl.BlockSpec((tm, tn), lambda i, j: (i, j)),
        out_shape=jax.ShapeDtypeStruct((m, n), out_dtype),
        compiler_params=_cparams(("arbitrary", "arbitrary")),
        name=name,
    )(x, w, *([ssq] if scaled else []))


def matmul_zgate(x, w, gates, branch, name):
    m, k = x.shape
    n = w.shape[1]
    tm, tn = _mm_tiles(m, n)
    assert tn == NSA_HPG * NSA_DH
    return pl.pallas_call(
        functools.partial(_mm_zgate_kernel, first_col=branch * NSA_HPG),
        grid=(m // tm, n // tn),
        in_specs=[pl.BlockSpec((tm, k), lambda i, j: (i, 0)),
                  pl.BlockSpec((k, tn), lambda i, j: (0, j)),
                  pl.BlockSpec((tm, LANES), lambda i, j: (i, j))],
        out_specs=pl.BlockSpec((tm, tn), lambda i, j: (i, j)),
        out_shape=jax.ShapeDtypeStruct((m, n), F32),
        compiler_params=_cparams(("arbitrary", "arbitrary")),
        name=name,
    )(x, w, gates)


def matmul_rope(x, w, tabs, seq, head_dim, rope_cols, plain_too, name, ssq=None):
    m, k = x.shape
    n = w.shape[1]
    tm, tn = _mm_tiles(m, n)
    assert seq % tm == 0
    per_seq = seq // tm
    scaled = ssq is not None
    tab_spec = pl.BlockSpec((tm, LANES), lambda i, j: (i % per_seq, 0))
    n_out = 2 if plain_too else 1
    outs = pl.pallas_call(
        functools.partial(_mm_rope_kernel, half=head_dim // 8, n_rope=rope_cols // LANES,
                          plain_too=plain_too, scaled=scaled),
        grid=(m // tm, n // tn),
        in_specs=[pl.BlockSpec((tm, k), lambda i, j: (i, 0)),
                  pl.BlockSpec((k, tn), lambda i, j: (0, j))]
        + ([_ssq_spec(tm)] if scaled else []) + [tab_spec] * 3,
        out_specs=[pl.BlockSpec((tm, tn), lambda i, j: (i, j))] * n_out,
        out_shape=[jax.ShapeDtypeStruct((m, n), BF16)] * n_out,
        compiler_params=_cparams(("arbitrary", "arbitrary")),
        name=name,
    )(x, w, *([ssq] if scaled else []), *tabs)
    return outs


def matmul_kvb(x, w, ssq, tabs, seq, name):
    m, k = x.shape
    n = w.shape[1]
    tm = min(MM_TM, m)
    assert seq % tm == 0 and n == 2 * SWA_KVH * SWA_DH
    per_seq = seq // tm
    tab_spec = pl.BlockSpec((tm, LANES), lambda i: (i % per_seq, 0))
    return pl.pallas_call(
        functools.partial(_mm_kvb_kernel, half=SWA_DH // 8),
        grid=(m // tm,),
        in_specs=[pl.BlockSpec((tm, k), lambda i: (i, 0)),
                  pl.BlockSpec((k, n), lambda i: (0, 0)),
                  pl.BlockSpec((tm, LANES), lambda i: (i, 0))] + [tab_spec] * 3,
        out_specs=pl.BlockSpec((tm, 4 * n), lambda i: (i, 0)),
        out_shape=jax.ShapeDtypeStruct((m, 4 * n), BF16),
        compiler_params=_cparams(("arbitrary",)),
        name=name,
    )(x, w, ssq, *tabs)


def matmul_residual(x, w, res, gains, name):
    m, k = x.shape
    n = w.shape[1]
    kg = gains.shape[0]
    tm, tn = _mm_tiles(m, n, MM_TN_FUSED)
    tile = pl.BlockSpec((tm, tn), lambda i, j: (i, j))
    outs = pl.pallas_call(
        _mm_res_kernel,
        grid=(m // tm, n // tn),
        in_specs=[pl.BlockSpec((tm, k), lambda i, j: (i, 0)),
                  pl.BlockSpec((k, tn), lambda i, j: (0, j)),
                  tile,
                  pl.BlockSpec((kg, tn), lambda i, j: (0, j))],
        out_specs=[tile] + [tile] * kg + [_ssq_spec(tm)],
        out_shape=[jax.ShapeDtypeStruct((m, n), F32)] + [jax.ShapeDtypeStruct((m, n), BF16)] * kg
        + [jax.ShapeDtypeStruct((m, LANES), F32)],
        compiler_params=_cparams(("arbitrary", "arbitrary")),
        name=name,
    )(x, w, res, gains)
    return outs[0], outs[1:-1], outs[-1]


def ple_update(n, ssq, wg, p, wp, h, gains, name):
    m, k = n.shape
    d = wg.shape[1]
    kp = p.shape[1]
    kg = 0 if gains is None else gains.shape[0]
    tm, tn = _mm_tiles(m, d, MM_TN_FUSED)
    tile = pl.BlockSpec((tm, tn), lambda i, j: (i, j))
    in_specs = [pl.BlockSpec((tm, k), lambda i, j: (i, 0)),
                pl.BlockSpec((k, tn), lambda i, j: (0, j)),
                _ssq_spec(tm),
                pl.BlockSpec((tm, kp), lambda i, j: (i, 0)),
                pl.BlockSpec((kp, tn), lambda i, j: (0, j)),
                tile]
    args = [n, wg, ssq, p, wp, h]
    out_specs = [tile]
    out_shape = [jax.ShapeDtypeStruct((m, d), F32)]
    if kg:
        in_specs.append(pl.BlockSpec((kg, tn), lambda i, j: (0, j)))
        args.append(gains)
        out_specs += [tile] * kg + [_ssq_spec(tm)]
        out_shape += [jax.ShapeDtypeStruct((m, d), BF16)] * kg + [jax.ShapeDtypeStruct((m, LANES), F32)]
    outs = pl.pallas_call(
        functools.partial(_ple_kernel, n_gain=kg),
        grid=(m // tm, d // tn),
        in_specs=in_specs,
        out_specs=out_specs,
        out_shape=out_shape,
        compiler_params=_cparams(("arbitrary", "arbitrary")),
        name=name,
    )(*args)
    if kg:
        return outs[0], outs[1:-1], outs[-1]
    return outs[0]


A_W_ROWS = 128


def _a_weights_kernel(w_ref, q_ref, cmp_ref, kvs_ref, g_ref, z0_ref, z1_ref, z2_ref):
    c_q, c_cmp, c_kv = NSA_WIDTH, NSA_WIDTH + 2 * NSA_KVW, NSA_WIDTH + 6 * NSA_KVW
    c_z = c_kv + 3 * NSA_G * NSA_HPG
    q_ref[...] = w_ref[:, 0:c_q].astype(BF16)
    cmp_ref[...] = w_ref[:, c_q:c_cmp].astype(BF16)
    kvs_ref[...] = w_ref[:, c_cmp:c_kv].astype(BF16)
    g_ref[...] = w_ref[:, c_kv:c_kv + LANES].astype(BF16)
    for br, z_ref in enumerate((z0_ref, z1_ref, z2_ref)):
        z_ref[...] = w_ref[:, c_z + br * NSA_WIDTH:c_z + (br + 1) * NSA_WIDTH].astype(BF16)


def split_a_weights(wa):
    d, a_in = wa.shape
    widths = (NSA_WIDTH, 2 * NSA_KVW, 4 * NSA_KVW, LANES, NSA_WIDTH, NSA_WIDTH, NSA_WIDTH)
    return pl.pallas_call(
        _a_weights_kernel,
        grid=(d // A_W_ROWS,),
        in_specs=[pl.BlockSpec((A_W_ROWS, a_in), lambda i: (i, 0))],
        out_specs=[pl.BlockSpec((A_W_ROWS, w), lambda i: (i, 0)) for w in widths],
        out_shape=[jax.ShapeDtypeStruct((d, w), BF16) for w in widths],
        compiler_params=_cparams(("arbitrary",)),
        name="split_a_weights",
    )(wa)


def _compress_kernel(x_ref, pos_ref, w1_ref, w2_ref, o_ref, *, n_cmp_pad):
    nc = n_cmp_pad
    lo = jnp.zeros((nc, CMP_HIDDEN), F32)
    hi = jnp.zeros((nc, CMP_HIDDEN), F32)
    for l in range(CMP_STRIDE):
        xl = x_ref[pl.ds(l, nc, stride=CMP_STRIDE), :]
        a = (xl + pos_ref[0, l:l + 1, :]).astype(BF16)
        b = (xl + pos_ref[0, CMP_STRIDE + l:CMP_STRIDE + l + 1, :]).astype(BF16)
        lo = lo + _dot(a, w1_ref[0, l * LANES:(l + 1) * LANES, :])
        hi = hi + _dot(b, w1_ref[0, (CMP_STRIDE + l) * LANES:(CMP_STRIDE + l + 1) * LANES, :])
    hid = lo + pltpu.roll(hi, nc - 1, 0)
    out = _dot(_silu(hid).astype(BF16), w2_ref[0])
    n_idx = lax.broadcasted_iota(jnp.int32, (nc, LANES), 0)
    o_ref[0, 0, 0] = jnp.where(n_idx < nc - 1, out, 0.0).astype(BF16)


def compress_kv(kcvc, pos, w1, w2, batch, seq):
    nc = seq // CMP_STRIDE
    return pl.pallas_call(
        functools.partial(_compress_kernel, n_cmp_pad=nc),
        grid=(batch, 2, NSA_G),
        in_specs=[pl.BlockSpec((seq, LANES), lambda b, s, g: (b, s * NSA_G + g)),
                  pl.BlockSpec((1, CMP_LEN, LANES), lambda b, s, g: (s, 0, 0)),
                  pl.BlockSpec((1, CMP_LEN * LANES, CMP_HIDDEN), lambda b, s, g: (s, 0, 0)),
                  pl.BlockSpec((1, CMP_HIDDEN, LANES), lambda b, s, g: (s, 0, 0))],
        out_specs=pl.BlockSpec((1, 1, 1, nc, LANES), lambda b, s, g: (b, s, g, 0, 0)),
        out_shape=jax.ShapeDtypeStruct((batch, 2, NSA_G, nc, LANES), BF16),
        compiler_params=_cparams(("arbitrary", "arbitrary", "arbitrary")),
        name="compress_kv",
    )(kcvc, pos, w1, w2)


def _nsa_kernel(qp_ref, qr_ref, z0_ref, z1_ref, z2_ref,
                kc_ref, vc_ref, ks_ref, vs_ref, kw_ref, vw_ref, et_ref, mt_ref,
                o_ref, qa_scr, m_scr, acc_scr, mix_scr, *, seq):
    tq, tk, hpg = NSA_TQ, NSA_TK, NSA_HPG
    rows = hpg * tq
    nc = seq // CMP_STRIDE
    ns = seq // SEL_BLOCK
    n_top = min(SEL_TOPK, ns)
    nsp = mt_ref.shape[0]
    c_exp = LOG2E / math.sqrt(NSA_DH)
    t0 = pl.program_id(2) * tq
    ones_k = jnp.ones((tk, LANES), BF16)

    t_col = t0 + lax.broadcasted_iota(jnp.int32, (tq, 1), 0)

    n_idx = lax.broadcasted_iota(jnp.int32, (1, nc), 1)
    valid = (n_idx * CMP_STRIDE + (CMP_LEN - 1) <= t_col) & (n_idx < nc - 1)
    bias_c = jnp.where(valid, 0.0, NEG_BIG)
    row_ok = t0 + lax.broadcasted_iota(jnp.int32, (tq, LANES), 0) >= CMP_LEN - 1
    lane_ok = t0 + lax.broadcasted_iota(jnp.int32, (ns, tq), 1) >= CMP_LEN - 1
    kc = kc_ref[0, 0, 0]
    vc_aug = jnp.concatenate([vc_ref[0, 0, 0], jnp.ones((nc, LANES), BF16)], axis=1)
    hq = tq // 2
    span = NSA_WINDOW + hq
    win = []
    for hh in range(2):
        kw0 = pl.multiple_of(jnp.maximum(t0 + hh * hq - NSA_WINDOW, 0), hq)
        t_half = t_col[hh * hq:(hh + 1) * hq]
        kpos_w = kw0 + lax.broadcasted_iota(jnp.int32, (1, span), 1)
        win.append((kw_ref[pl.ds(kw0, span), :],
                    jnp.concatenate([vw_ref[pl.ds(kw0, span), :], jnp.ones((span, LANES), BF16)], axis=1),
                    jnp.where((kpos_w <= t_half) & (kpos_w > t_half - NSA_WINDOW), 0.0, NEG_BIG)))
    q_all = jnp.concatenate([qp_ref[:, h * LANES:(h + 1) * LANES] for h in range(hpg)], axis=0)
    sc = _dot_nt(q_all, kc) + jnp.concatenate([bias_c] * hpg, axis=0)
    m = jnp.max(sc, axis=-1, keepdims=True)
    pe = jnp.exp2((sc - m) * c_exp).astype(BF16)
    acc_c = _dot(pe, vc_aug)
    imp_t = _dot_nt(mt_ref[...], pe)
    imp_parts = []
    for h in range(hpg):
        sl = slice(h * LANES, (h + 1) * LANES)
        rs = slice(h * tq, (h + 1) * tq)
        oc = jnp.where(row_ok, acc_c[rs, 0:LANES] * (1.0 / acc_c[rs, LANES:2 * LANES]), 0.0)
        imp_parts.append(imp_t[0:ns, rs] * (1.0 / imp_t[nsp - 1:nsp, rs]))
        ows = []
        for hh, (kwin, vw_aug, bias_w) in enumerate(win):
            sc = _dot_nt(qr_ref[hh * hq:(hh + 1) * hq, sl], kwin) + bias_w
            m = jnp.max(sc, axis=-1, keepdims=True)
            pe_w = jnp.exp2((sc - m) * c_exp).astype(BF16)
            acc = _dot(pe_w, vw_aug)
            ows.append(acc[:, 0:LANES] * (1.0 / acc[:, LANES:2 * LANES]))
        ow = jnp.concatenate(ows, axis=0)
        mix_scr[:, sl] = oc * z0_ref[:, sl] + ow * z2_ref[:, sl]
    while len(imp_parts) > 1:
        imp_parts = [imp_parts[i] + imp_parts[i + 1] for i in range(0, len(imp_parts), 2)]
    imp = jnp.where(lane_ok, imp_parts[0], 0.0)

    j_io = lax.broadcasted_iota(jnp.int32, (ns, tq), 0)
    t_io = t0 + lax.broadcasted_iota(jnp.int32, (ns, tq), 1)
    dist = lax.shift_right_logical(t_io, int(math.log2(SEL_BLOCK))) - j_io
    forced = (j_io == 0) | ((dist >= 0) & (dist < SEL_LOCAL))
    imp = jnp.where(forced, FORCE_SCORE, imp)
    imp = jnp.where(dist >= 0, imp, -1.0)
    sub = 8
    ranks = []
    for v in range(ns // sub):
        blk = imp[v * sub:(v + 1) * sub, :]
        j_v = v * sub + lax.broadcasted_iota(jnp.int32, (sub, tq), 0)
        r = jnp.zeros((sub, tq), F32)
        for k in range(ns):
            rk = imp[k:k + 1, :]
            if k < v * sub:
                ahead = rk >= blk
            elif k >= (v + 1) * sub:
                ahead = rk > blk
            else:
                ahead = (rk > blk) | ((rk == blk) & (j_v > k))
            r = r + jnp.where(ahead, 1.0, 0.0)
        ranks.append(r)
    rank = jnp.concatenate(ranks, axis=0)
    not_sel_t = jnp.concatenate([jnp.where(rank < n_top, 0.0, 1.0), jnp.ones((nsp - ns, tq), F32)], axis=0)
    not_sel = not_sel_t.T.astype(BF16)

    for h in range(hpg):
        qa_scr[h * tq:(h + 1) * tq, 0:LANES] = qr_ref[:, h * LANES:(h + 1) * LANES]
        qa_scr[h * tq:(h + 1) * tq, LANES:2 * LANES] = not_sel

    m_scr[...] = jnp.full((rows, LANES), NEG_BIG, F32)
    acc_scr[...] = jnp.zeros((rows, 2 * LANES), F32)
    n_kt = lax.shift_right_logical(t0 + (tq + tk - 1), int(math.log2(tk)))

    def slc_tile(kt, causal):
        k0 = pl.multiple_of(kt * tk, tk)
        k_aug = jnp.concatenate([ks_ref[pl.ds(k0, tk), :], et_ref[pl.ds(k0, tk), :]], axis=1)
        v_aug = jnp.concatenate([vs_ref[pl.ds(k0, tk), :], ones_k], axis=1)
        if causal:
            kpos = k0 + lax.broadcasted_iota(jnp.int32, (1, tk), 1)
            bias = jnp.where(kpos <= t_col, 0.0, NEG_BIG)
        for h in range(hpg):
            rs = slice(h * tq, (h + 1) * tq)
            sc = _dot_nt(qa_scr[rs, :], k_aug)
            if causal:
                sc = sc + bias
            m_old = m_scr[rs, :]
            m_new = jnp.maximum(m_old, jnp.max(sc, axis=-1, keepdims=True))
            alpha = jnp.exp2((m_old - m_new) * c_exp)
            pe = jnp.exp2((sc - _lane_tile(m_new, tk // LANES)) * c_exp)
            acc_scr[rs, :] = _lane_tile(alpha, 2) * acc_scr[rs, :] + _dot(pe.astype(BF16), v_aug)
            m_scr[rs, :] = m_new

    def pair_body(i, carry):
        slc_tile(2 * i, causal=False)
        slc_tile(2 * i + 1, causal=False)
        return carry

    lax.fori_loop(0, lax.shift_right_logical(n_kt - 1, 1), pair_body, 0)
    odd = ((n_kt - 1) & 1) == 1

    @pl.when(odd)
    def _():
        slc_tile(n_kt - 2, causal=False)
        slc_tile(n_kt - 1, causal=True)

    @pl.when(jnp.logical_not(odd))
    def _():
        slc_tile(n_kt - 1, causal=True)

    for h in range(hpg):
        rs = slice(h * tq, (h + 1) * tq)
        sl = slice(h * LANES, (h + 1) * LANES)
        o_slc = acc_scr[rs, 0:LANES] * (1.0 / acc_scr[rs, LANES:2 * LANES])
        o_ref[:, sl] = (mix_scr[:, sl] + o_slc * z1_ref[:, sl]).astype(BF16)


def nsa_attention(qp, qr, zg, cmp_kv, kv, e_t, cmp_to_sel_t, batch, seq):
    t = qp.shape[0]
    tq = NSA_TQ
    assert seq % NSA_TK == 0 and NSA_TK % tq == 0 and seq >= NSA_WINDOW + tq
    qt_per_seq = seq // tq
    nc = seq // CMP_STRIDE
    hw = NSA_HPG * NSA_DH

    qz_spec = pl.BlockSpec((tq, hw), lambda b, g, i: (b * qt_per_seq + i, g))
    cmp_specs = [pl.BlockSpec((1, 1, 1, nc, LANES), functools.partial(
        lambda b, g, i, s: (b, s, g, 0, 0), s=s)) for s in range(2)]
    kv_specs = [pl.BlockSpec((seq, LANES), functools.partial(
        lambda b, g, i, s: (b, s * NSA_G + g), s=s)) for s in range(4)]
    const_specs = [pl.BlockSpec(e_t.shape, lambda b, g, i: (0, 0)),
                   pl.BlockSpec(cmp_to_sel_t.shape, lambda b, g, i: (0, 0))]
    rows = NSA_HPG * tq
    return pl.pallas_call(
        functools.partial(_nsa_kernel, seq=seq),
        grid=(batch, NSA_G, qt_per_seq),
        in_specs=[qz_spec] * 5 + cmp_specs + kv_specs + const_specs,
        out_specs=qz_spec,
        out_shape=jax.ShapeDtypeStruct((t, NSA_WIDTH), BF16),
        scratch_shapes=[pltpu.VMEM((rows, 2 * LANES), BF16),
                        pltpu.VMEM((rows, LANES), F32),
                        pltpu.VMEM((rows, 2 * LANES), F32),
                        pltpu.VMEM((tq, hw), F32)],
        compiler_params=_cparams(("arbitrary", "arbitrary", "arbitrary")),
        name="nsa_attention",
    )(qp, qr, *zg, cmp_kv, cmp_kv, kv, kv, kv, kv, e_t, cmp_to_sel_t)


def _swa_kernel(sink_ref, q_ref, z_ref, *rest):
    kv_refs, o_ref = rest[:-1], rest[-1]
    tq = ATT_TQ
    n_pair = SWA_KVH // 2
    per_kv = SWA_HPG * SWA_DH // LANES
    scale = 1.0 / math.sqrt(SWA_DH)
    c_exp = LOG2E * scale
    t0 = pl.program_id(1) * tq
    ones_k = jnp.ones((2 * tq, LANES), BF16)

    def window(kind, var, pair):
        base = (kind * 4 + var) * 2
        sl = slice(pair * LANES, (pair + 1) * LANES)
        return jnp.concatenate([kv_refs[base][:, sl], kv_refs[base + 1][:, sl]], axis=0)

    t_col = t0 + lax.broadcasted_iota(jnp.int32, (tq, 1), 0)
    kpos = t0 - tq + lax.broadcasted_iota(jnp.int32, (1, 2 * tq), 1)
    ok = (kpos <= t_col) & (kpos > t_col - SWA_WINDOW) & (kpos >= 0)
    bias = jnp.where(ok, 0.0, NEG_BIG)
    col0 = lax.broadcasted_iota(jnp.int32, (tq, LANES), 1) == 0
    keep = jnp.where(lax.broadcasted_iota(jnp.int32, (2 * tq, LANES), 0) == 0, 0.0, 1.0).astype(BF16)

    for pair in range(n_pair):
        for j in range(2):
            k_var = [window(0, 2 * j + par, pair) * keep for par in range(2)]
            v_var = [jnp.concatenate([window(1, 2 * j + par, pair) * keep, ones_k], axis=1) for par in range(2)]
            for c in range(per_kv):
                col = ((pair * 2 + j) * per_kv + c) * LANES
                qc = q_ref[:, col:col + LANES]
                o_chunk = None
                for par in range(2):
                    head = (pair * 2 + j) * SWA_HPG + 2 * c + par
                    sink_raw = sink_ref[head] * math.sqrt(SWA_DH)
                    bias_h = jnp.concatenate([jnp.where(col0, sink_raw, bias[:, 0:LANES]), bias[:, LANES:]], axis=1)
                    sc = _dot_nt(qc, k_var[par]) + bias_h
                    m = jnp.max(sc, axis=-1, keepdims=True)
                    e = jnp.exp2((sc - m) * c_exp)
                    acc = _dot(e.astype(BF16), v_var[par])
                    o_par = acc[:, 0:LANES] * (1.0 / acc[:, LANES:2 * LANES])
                    o_chunk = o_par if o_chunk is None else o_chunk + o_par
                o_ref[:, col:col + LANES] = (o_chunk * z_ref[:, col:col + LANES]).astype(BF16)


def swa_attention(q, zs, kvb, sinks, batch, seq):
    t = q.shape[0]
    tq = ATT_TQ
    qt_per_seq = seq // tq
    kvw = SWA_KVH * SWA_DH

    def row(b, i):
        return b * qt_per_seq + i

    def prev_row(b, i):
        return b * qt_per_seq + jnp.maximum(i - 1, 0)

    qz_spec = pl.BlockSpec((tq, SWA_WIDTH), lambda b, i: (row(b, i), 0))
    kv_specs = []
    for cb in range(8):
        kv_specs.append(pl.BlockSpec((tq, kvw), functools.partial(lambda b, i, cb: (prev_row(b, i), cb), cb=cb)))
        kv_specs.append(pl.BlockSpec((tq, kvw), functools.partial(lambda b, i, cb: (row(b, i), cb), cb=cb)))
    return pl.pallas_call(
        _swa_kernel,
        grid=(batch, qt_per_seq),
        in_specs=[pl.BlockSpec(memory_space=pltpu.SMEM), qz_spec, qz_spec] + kv_specs,
        out_specs=qz_spec,
        out_shape=jax.ShapeDtypeStruct((t, SWA_WIDTH), BF16),
        compiler_params=_cparams(("arbitrary", "arbitrary")),
        name="swa_attention",
    )(sinks, q, zs, *([kvb] * 16))


def _selection_constants(seq):
    nc = seq // CMP_STRIDE
    ns = seq // SEL_BLOCK
    nsp = -(-ns // LANES) * LANES
    e_t = np.zeros((seq, nsp), np.float32)
    e_t[np.arange(seq), np.arange(seq) // SEL_BLOCK] = NEG_BIG
    c0 = np.arange(nc - 1) * CMP_STRIDE
    s0 = np.arange(ns) * SEL_BLOCK
    overlap = (c0[:, None] < s0[None, :] + SEL_BLOCK) & (c0[:, None] + CMP_LEN > s0[None, :])
    assert ns < nsp
    cmp_to_sel_t = np.zeros((nsp, nc), np.float32)
    cmp_to_sel_t[:ns, :nc - 1] = overlap.T
    cmp_to_sel_t[nsp - 1, :] = 1.0
    return jnp.asarray(e_t, BF16), jnp.asarray(cmp_to_sel_t, BF16)


def kernel(x, p, a_norm, a_w_in, a_w_out, a_cmp_pos_k, a_cmp_w1_k, a_cmp_w2_k, a_cmp_pos_v, a_cmp_w1_v,
           a_cmp_w2_v, kv_norm, w_kv, b_norm, b_w_in, b_w_out, b_sinks, ple_norm, ple_gate_w, ple_proj,
           final_norm):
    batch, seq, d = x.shape
    assert d == D_MODEL and p.shape[0] == 2 and a_w_in.shape[0] == 1 and b_w_in.shape[0] == 1
    t = batch * seq
    xf = x.reshape(t, d)

    wa = a_w_in[0]
    n_gate = 3 * NSA_G * NSA_HPG
    w_q, w_cmp, w_kvs, w_gate, *w_z = split_a_weights(wa)
    wg = w_gate[:, :n_gate].reshape(d, 3, NSA_G, NSA_HPG).transpose(0, 2, 1, 3)
    wg = wg.reshape(d, NSA_G, 3 * NSA_HPG)
    wg = jnp.pad(wg, ((0, 0), (0, 0), (0, LANES - 3 * NSA_HPG))).reshape(d, NSA_G * LANES)
    pos_kv = jnp.stack([a_cmp_pos_k[0], a_cmp_pos_v[0]])
    w1_kv = jnp.stack([a_cmp_w1_k[0], a_cmp_w1_v[0]]).astype(BF16)
    w2_kv = jnp.stack([a_cmp_w2_k[0], a_cmp_w2_v[0]]).astype(BF16)
    wb = b_w_in[0]
    tabs_a = _rope_tables(seq, NSA_DH)
    tabs_b = _rope_tables(seq, SWA_DH)
    e_t, cmp_to_sel_t = _selection_constants(seq)

    (n_a,) = rms_norm_rows(xf, a_norm, BF16)
    q_rot, q_plain = matmul_rope(n_a, w_q, tabs_a, seq, NSA_DH, MM_TN, True, "a_q_proj")
    (kv_a,) = matmul_rope(n_a, w_kvs, tabs_a, seq, NSA_DH, NSA_KVW, False, "a_kv_proj")
    kcvc = matmul(n_a, w_cmp, "a_cmp_proj")
    gates = matmul(n_a, wg, "a_gate_proj", act="sigmoid")
    zg = [matmul_zgate(n_a, w_z[br], gates, br, "a_z_proj%d" % br) for br in range(3)]
    cmp_kv = compress_kv(kcvc, pos_kv, w1_kv, w2_kv, batch, seq)
    mixed = nsa_attention(q_plain, q_rot, zg, cmp_kv, kv_a, e_t, cmp_to_sel_t, batch, seq)
    h, (n_p,), ssq = matmul_residual(mixed, a_w_out[0].astype(BF16), xf, ple_norm[0:1], "a_out_proj")
    pf = p.reshape(2, t, PLE_DIM)
    h, (n_kv, n_b), ssq = ple_update(n_p, ssq, ple_gate_w[0].astype(BF16), pf[0], ple_proj[0].astype(BF16), h,
                                     jnp.stack([kv_norm, b_norm[0]]), "ple0")

    kvb = matmul_kvb(n_kv, w_kv.astype(BF16), ssq, tabs_b, seq, "kv_proj")
    (q_b,) = matmul_rope(n_b, wb[:, :SWA_WIDTH].astype(BF16), tabs_b, seq, SWA_DH, MM_TN, False, "b_q_proj",
                         ssq=ssq)
    zs_b = matmul(n_b, wb[:, SWA_WIDTH:].astype(BF16), "b_z_proj", act="silu", ssq=ssq)
    ob = swa_attention(q_b, zs_b, kvb, b_sinks[0], batch, seq)
    h, (n_p,), ssq = matmul_residual(ob, b_w_out[0].astype(BF16), h, ple_norm[1:2], "b_out_proj")
    h = ple_update(n_p, ssq, ple_gate_w[1].astype(BF16), pf[1], ple_proj[1].astype(BF16), h, None, "ple1")

    (out,) = rms_norm_rows(h, final_norm[None, :], F32)
    return out.reshape(batch, seq, d)
```

```python
import functools
import math

import numpy as np
import jax
import jax.numpy as jnp
from jax import lax
from jax.experimental import pallas as pl
from jax.experimental.pallas import tpu as pltpu

F32 = jnp.float32
BF16 = jnp.bfloat16

D_MODEL = 4096
RMS_EPS = 1e-6
ROPE_THETA = 500000.0
NEG_BIG = -1e30
FORCE_SCORE = 1e9
PLE_DIM = 256

NSA_DH = 128
NSA_G = 4
NSA_HPG = 8
NSA_WIDTH = 4096
NSA_KVW = 512
CMP_LEN = 32
CMP_STRIDE = 16
CMP_HIDDEN = 256
SEL_BLOCK = 64
SEL_TOPK = 16
SEL_LOCAL = 2
NSA_WINDOW = 512

SWA_DH = 64
SWA_KVH = 8
SWA_HPG = 8
SWA_WIDTH = 4096
SWA_WINDOW = 128

LANES = 128
VMEM_LIMIT_BYTES = 58 * 1024 * 1024

MM_TM = 1024
MM_TN = 1024
MM_TN_FUSED = 512
NORM_ROWS = 256
ATT_TQ = 128
NSA_TQ = 256
NSA_TK = 512

LOG2E = math.log2(math.e)


def _cparams(sem):
    return pltpu.CompilerParams(dimension_semantics=sem, vmem_limit_bytes=VMEM_LIMIT_BYTES)


def _sigmoid(v):
    return 1.0 / (1.0 + jnp.exp(-v))


def _silu(v):
    return v * _sigmoid(v)


def _dot(a, b):
    return jnp.dot(a, b, preferred_element_type=F32)


def _dot_nt(a, b):
    return lax.dot_general(a, b, (((1,), (1,)), ((), ())), preferred_element_type=F32)


def _lane_tile(x, n):
    return jnp.concatenate([x] * n, axis=1)


def _norm_kernel(h_ref, g_ref, *o_refs):
    x = h_ref[...]
    r = lax.rsqrt(jnp.mean(x * x, axis=-1, keepdims=True) + RMS_EPS)
    y = x * r
    for i, o_ref in enumerate(o_refs):
        o_ref[...] = (y * g_ref[i:i + 1, :]).astype(o_ref.dtype)


def rms_norm_rows(h, gains, out_dtype):
    t, d = h.shape
    k = gains.shape[0]
    outs = pl.pallas_call(
        _norm_kernel,
        grid=(t // NORM_ROWS,),
        in_specs=[pl.BlockSpec((NORM_ROWS, d), lambda i: (i, 0)),
                  pl.BlockSpec((k, d), lambda i: (0, 0))],
        out_specs=[pl.BlockSpec((NORM_ROWS, d), lambda i: (i, 0))] * k,
        out_shape=[jax.ShapeDtypeStruct((t, d), out_dtype)] * k,
        compiler_params=_cparams(("arbitrary",)),
        name="rms_norm",
    )(h, gains)
    return outs


def _rope_tables(seq, head_dim):
    rot = head_dim // 4
    half = rot // 2
    inv_freq = ROPE_THETA ** (-jnp.arange(half, dtype=F32) / half)
    ang = jnp.arange(seq, dtype=jnp.int32).astype(F32)[:, None] * inv_freq[None, :]
    cos, sin = jnp.cos(ang), jnp.sin(ang)
    rest = head_dim - rot
    zeros_h = jnp.zeros((seq, half), F32)
    c = jnp.concatenate([cos, cos, jnp.ones((seq, rest), F32)], axis=1)
    s1 = jnp.concatenate([-sin, zeros_h, jnp.zeros((seq, rest), F32)], axis=1)
    s2 = jnp.concatenate([zeros_h, sin, jnp.zeros((seq, rest), F32)], axis=1)
    reps = LANES // head_dim
    return tuple(jnp.tile(a, (1, reps)) for a in (c, s1, s2))


def _rope_lanes(t, c, s1, s2, half):
    return t * c + pltpu.roll(t, LANES - half, 1) * s1 + pltpu.roll(t, half, 1) * s2


def _row_rsqrt(s_ref, k):
    return lax.rsqrt(s_ref[...] * (1.0 / k) + RMS_EPS)


def _scaled_dot(x_ref, w_ref, s_ref):
    acc = _dot(x_ref[...], w_ref[...])
    if s_ref is not None:
        acc = acc * _lane_tile(_row_rsqrt(s_ref, x_ref.shape[1]), acc.shape[1] // LANES)
    return acc


def _write_norm_outputs(h, g_ref, hb_refs, ssq_ref):
    for i, hb_ref in enumerate(hb_refs):
        hb_ref[...] = (h * g_ref[i:i + 1, :]).astype(BF16)
    part = jnp.broadcast_to(jnp.sum(h * h, axis=-1, keepdims=True), ssq_ref.shape)
    j = pl.program_id(1)

    @pl.when(j == 0)
    def _():
        ssq_ref[...] = part

    @pl.when(j > 0)
    def _():
        ssq_ref[...] = ssq_ref[...] + part


def _mm_kernel(*refs, act, scaled):
    x_ref, w_ref = refs[0], refs[1]
    s_ref = refs[2] if scaled else None
    o_ref = refs[-1]
    acc = _scaled_dot(x_ref, w_ref, s_ref)
    if act == "silu":
        acc = _silu(acc)
    elif act == "sigmoid":
        acc = _sigmoid(acc)
    o_ref[...] = acc.astype(o_ref.dtype)


def _mm_zgate_kernel(x_ref, w_ref, g_ref, o_ref, *, first_col):
    acc = _silu(_dot(x_ref[...], w_ref[...]))
    for h in range(acc.shape[1] // LANES):
        sl = slice(h * LANES, (h + 1) * LANES)
        o_ref[:, sl] = acc[:, sl] * g_ref[:, first_col + h:first_col + h + 1]


def _mm_res_kernel(x_ref, w_ref, r_ref, g_ref, h_ref, *rest):
    h = r_ref[...] + _dot(x_ref[...], w_ref[...])
    h_ref[...] = h
    _write_norm_outputs(h, g_ref, rest[:-1], rest[-1])


def _ple_kernel(n_ref, wg_ref, s_ref, p_ref, wp_ref, h_ref, *rest, n_gain):
    gate = _sigmoid(_scaled_dot(n_ref, wg_ref, s_ref))
    emb = _dot(p_ref[...].astype(BF16), wp_ref[...])
    h = h_ref[...] + gate * emb
    if n_gain:
        g_ref, o_ref = rest[0], rest[1]
        o_ref[...] = h
        _write_norm_outputs(h, g_ref, rest[2:-1], rest[-1])
    else:
        rest[0][...] = h


def _mm_rope_kernel(*refs, half, n_rope, plain_too, scaled):
    x_ref, w_ref = refs[0], refs[1]
    s_ref = refs[2] if scaled else None
    c_ref, s1_ref, s2_ref = refs[3:6] if scaled else refs[2:5]
    o_refs = refs[6:] if scaled else refs[5:]
    acc = _scaled_dot(x_ref, w_ref, s_ref)
    c, s1, s2 = c_ref[...], s1_ref[...], s2_ref[...]
    for ch in range(acc.shape[1] // LANES):
        sl = slice(ch * LANES, (ch + 1) * LANES)
        t = acc[:, sl]
        o_refs[0][:, sl] = (_rope_lanes(t, c, s1, s2, half) if ch < n_rope else t).astype(BF16)
        if plain_too:
            o_refs[1][:, sl] = t.astype(BF16)


def _mm_kvb_kernel(x_ref, w_ref, s_ref, c_ref, s1_ref, s2_ref, o_ref, *, half):
    acc = _scaled_dot(x_ref, w_ref, s_ref)
    c, s1, s2 = c_ref[...], s1_ref[...], s2_ref[...]
    kvw = SWA_KVH * SWA_DH
    n_ch = kvw // LANES
    low = lax.broadcasted_iota(jnp.int32, (acc.shape[0], LANES), 1) < SWA_DH
    for kind in range(2):
        for ch in range(n_ch):
            t = acc[:, kind * kvw + ch * LANES: kind * kvw + (ch + 1) * LANES]
            if kind == 0:
                t = _rope_lanes(t, c, s1, s2, half)
            ts = pltpu.roll(t, SWA_DH, 1)
            variants = (jnp.where(low, t, 0.0), jnp.where(low, 0.0, ts),
                        jnp.where(low, ts, 0.0), jnp.where(low, 0.0, t))
            for var, val in enumerate(variants):
                col = (kind * 4 + var) * kvw + ch * LANES
                o_ref[:, col:col + LANES] = val.astype(BF16)


def _mm_tiles(m, n, tn_max=MM_TN):
    tm = min(MM_TM, m)
    tn = min(tn_max, n)
    assert m % tm == 0 and n % tn == 0, (m, n)
    return tm, tn


def _ssq_spec(tm):
    return pl.BlockSpec((tm, LANES), lambda i, j: (i, 0))


def matmul(x, w, name, act=None, out_dtype=F32, ssq=None):
    m, k = x.shape
    n = w.shape[1]
    tm, tn = _mm_tiles(m, n)
    scaled = ssq is not None
    return pl.pallas_call(
        functools.partial(_mm_kernel, act=act, scaled=scaled),
        grid=(m // tm, n // tn),
        in_specs=[pl.BlockSpec((tm, k), lambda i, j: (i, 0)),
                  pl.BlockSpec((k, tn), lambda i, j: (0, j))] + ([_ssq_spec(tm)] if scaled else []),
        out_specs=pl.BlockSpec((tm, tn), lambda i, j: (i, j)),
        out_shape=jax.ShapeDtypeStruct((m, n), out_dtype),
        compiler_params=_cparams(("arbitrary", "arbitrary")),
        name=name,
    )(x, w, *([ssq] if scaled else []))


def matmul_zgate(x, w, gates, branch, name):
    m, k = x.shape
    n = w.shape[1]
    tm, tn = _mm_tiles(m, n)
    assert tn == NSA_HPG * NSA_DH
    return pl.pallas_call(
        functools.partial(_mm_zgate_kernel, first_col=branch * NSA_HPG),
        grid=(m // tm, n // tn),
        in_specs=[pl.BlockSpec((tm, k), lambda i, j: (i, 0)),
                  pl.BlockSpec((k, tn), lambda i, j: (0, j)),
                  pl.BlockSpec((tm, LANES), lambda i, j: (i, j))],
        out_specs=pl.BlockSpec((tm, tn), lambda i, j: (i, j)),
        out_shape=jax.ShapeDtypeStruct((m, n), F32),
        compiler_params=_cparams(("arbitrary", "arbitrary")),
        name=name,
    )(x, w, gates)


def matmul_rope(x, w, tabs, seq, head_dim, rope_cols, plain_too, name, ssq=None):
    m, k = x.shape
    n = w.shape[1]
    tm, tn = _mm_tiles(m, n)
    assert seq % tm == 0
    per_seq = seq // tm
    scaled = ssq is not None
    tab_spec = pl.BlockSpec((tm, LANES), lambda i, j: (i % per_seq, 0))
    n_out = 2 if plain_too else 1
    outs = pl.pallas_call(
        functools.partial(_mm_rope_kernel, half=head_dim // 8, n_rope=rope_cols // LANES,
                          plain_too=plain_too, scaled=scaled),
        grid=(m // tm, n // tn),
        in_specs=[pl.BlockSpec((tm, k), lambda i, j: (i, 0)),
                  pl.BlockSpec((k, tn), lambda i, j: (0, j))]
        + ([_ssq_spec(tm)] if scaled else []) + [tab_spec] * 3,
        out_specs=[pl.BlockSpec((tm, tn), lambda i, j: (i, j))] * n_out,
        out_shape=[jax.ShapeDtypeStruct((m, n), BF16)] * n_out,
        compiler_params=_cparams(("arbitrary", "arbitrary")),
        name=name,
    )(x, w, *([ssq] if scaled else []), *tabs)
    return outs


def matmul_kvb(x, w, ssq, tabs, seq, name):
    m, k = x.shape
    n = w.shape[1]
    tm = min(MM_TM, m)
    assert seq % tm == 0 and n == 2 * SWA_KVH * SWA_DH
    per_seq = seq // tm
    tab_spec = pl.BlockSpec((tm, LANES), lambda i: (i % per_seq, 0))
    return pl.pallas_call(
        functools.partial(_mm_kvb_kernel, half=SWA_DH // 8),
        grid=(m // tm,),
        in_specs=[pl.BlockSpec((tm, k), lambda i: (i, 0)),
                  pl.BlockSpec((k, n), lambda i: (0, 0)),
                  pl.BlockSpec((tm, LANES), lambda i: (i, 0))] + [tab_spec] * 3,
        out_specs=pl.BlockSpec((tm, 4 * n), lambda i: (i, 0)),
        out_shape=jax.ShapeDtypeStruct((m, 4 * n), BF16),
        compiler_params=_cparams(("arbitrary",)),
        name=name,
    )(x, w, ssq, *tabs)


def matmul_residual(x, w, res, gains, name):
    m, k = x.shape
    n = w.shape[1]
    kg = gains.shape[0]
    tm, tn = _mm_tiles(m, n, MM_TN_FUSED)
    tile = pl.BlockSpec((tm, tn), lambda i, j: (i, j))
    outs = pl.pallas_call(
        _mm_res_kernel,
        grid=(m // tm, n // tn),
        in_specs=[pl.BlockSpec((tm, k), lambda i, j: (i, 0)),
                  pl.BlockSpec((k, tn), lambda i, j: (0, j)),
                  tile,
                  pl.BlockSpec((kg, tn), lambda i, j: (0, j))],
        out_specs=[tile] + [tile] * kg + [_ssq_spec(tm)],
        out_shape=[jax.ShapeDtypeStruct((m, n), F32)] + [jax.ShapeDtypeStruct((m, n), BF16)] * kg
        + [jax.ShapeDtypeStruct((m, LANES), F32)],
        compiler_params=_cparams(("arbitrary", "arbitrary")),
        name=name,
    )(x, w, res, gains)
    return outs[0], outs[1:-1], outs[-1]


def ple_update(n, ssq, wg, p, wp, h, gains, name):
    m, k = n.shape
    d = wg.shape[1]
    kp = p.shape[1]
    kg = 0 if gains is None else gains.shape[0]
    tm, tn = _mm_tiles(m, d, MM_TN_FUSED)
    tile = pl.BlockSpec((tm, tn), lambda i, j: (i, j))
    in_specs = [pl.BlockSpec((tm, k), lambda i, j: (i, 0)),
                pl.BlockSpec((k, tn), lambda i, j: (0, j)),
                _ssq_spec(tm),
                pl.BlockSpec((tm, kp), lambda i, j: (i, 0)),
                pl.BlockSpec((kp, tn), lambda i, j: (0, j)),
                tile]
    args = [n, wg, ssq, p, wp, h]
    out_specs = [tile]
    out_shape = [jax.ShapeDtypeStruct((m, d), F32)]
    if kg:
        in_specs.append(pl.BlockSpec((kg, tn), lambda i, j: (0, j)))
        args.append(gains)
        out_specs += [tile] * kg + [_ssq_spec(tm)]
        out_shape += [jax.ShapeDtypeStruct((m, d), BF16)] * kg + [jax.ShapeDtypeStruct((m, LANES), F32)]
    outs = pl.pallas_call(
        functools.partial(_ple_kernel, n_gain=kg),
        grid=(m // tm, d // tn),
        in_specs=in_specs,
        out_specs=out_specs,
        out_shape=out_shape,
        compiler_params=_cparams(("arbitrary", "arbitrary")),
        name=name,
    )(*args)
    if kg:
        return outs[0], outs[1:-1], outs[-1]
    return outs[0]


A_W_COLS = 512


def _wt_cast_kernel(wt_ref, o_ref):
    o_ref[...] = wt_ref[...].T.astype(BF16)


def _cast_weight_cols(wt, first, width, name):
    k = wt.shape[1]
    cols = min(A_W_COLS, width)
    assert width % cols == 0
    return pl.pallas_call(
        _wt_cast_kernel,
        grid=(width // cols,),
        in_specs=[pl.BlockSpec((pl.Element(cols), pl.Element(k)), lambda i: (pl.multiple_of(first + i * cols, 8), 0))],
        out_specs=pl.BlockSpec((k, cols), lambda i: (0, i)),
        out_shape=jax.ShapeDtypeStruct((k, width), BF16),
        compiler_params=_cparams(("arbitrary",)),
        name=name,
    )(wt)


def split_a_weights(wa):
    wt = wa[0].T
    c_q, c_cmp, c_kv = NSA_WIDTH, NSA_WIDTH + 2 * NSA_KVW, NSA_WIDTH + 6 * NSA_KVW
    c_z = c_kv + 3 * NSA_G * NSA_HPG
    segs = [(0, c_q), (c_q, c_cmp - c_q), (c_cmp, c_kv - c_cmp), (c_kv, LANES)]
    segs += [(c_z + br * NSA_WIDTH, NSA_WIDTH) for br in range(3)]
    return [_cast_weight_cols(wt, first, width, "a_w_cast%d" % i) for i, (first, width) in enumerate(segs)]


def _compress_kernel(x_ref, pos_ref, w1_ref, w2_ref, o_ref, *, n_cmp_pad):
    nc = n_cmp_pad
    lo = jnp.zeros((nc, CMP_HIDDEN), F32)
    hi = jnp.zeros((nc, CMP_HIDDEN), F32)
    for l in range(CMP_STRIDE):
        xl = x_ref[pl.ds(l, nc, stride=CMP_STRIDE), :]
        a = (xl + pos_ref[0, l:l + 1, :]).astype(BF16)
        b = (xl + pos_ref[0, CMP_STRIDE + l:CMP_STRIDE + l + 1, :]).astype(BF16)
        lo = lo + _dot(a, w1_ref[0, l * LANES:(l + 1) * LANES, :])
        hi = hi + _dot(b, w1_ref[0, (CMP_STRIDE + l) * LANES:(CMP_STRIDE + l + 1) * LANES, :])
    hid = lo + pltpu.roll(hi, nc - 1, 0)
    out = _dot(_silu(hid).astype(BF16), w2_ref[0])
    n_idx = lax.broadcasted_iota(jnp.int32, (nc, LANES), 0)
    o_ref[0, 0, 0] = jnp.where(n_idx < nc - 1, out, 0.0).astype(BF16)


def compress_kv(kcvc, pos, w1, w2, batch, seq):
    nc = seq // CMP_STRIDE
    return pl.pallas_call(
        functools.partial(_compress_kernel, n_cmp_pad=nc),
        grid=(batch, 2, NSA_G),
        in_specs=[pl.BlockSpec((seq, LANES), lambda b, s, g: (b, s * NSA_G + g)),
                  pl.BlockSpec((1, CMP_LEN, LANES), lambda b, s, g: (s, 0, 0)),
                  pl.BlockSpec((1, CMP_LEN * LANES, CMP_HIDDEN), lambda b, s, g: (s, 0, 0)),
                  pl.BlockSpec((1, CMP_HIDDEN, LANES), lambda b, s, g: (s, 0, 0))],
        out_specs=pl.BlockSpec((1, 1, 1, nc, LANES), lambda b, s, g: (b, s, g, 0, 0)),
        out_shape=jax.ShapeDtypeStruct((batch, 2, NSA_G, nc, LANES), BF16),
        compiler_params=_cparams(("arbitrary", "arbitrary", "arbitrary")),
        name="compress_kv",
    )(kcvc, pos, w1, w2)


def _nsa_kernel(qp_ref, qr_ref, z0_ref, z1_ref, z2_ref,
                kc_ref, vc_ref, ks_ref, vs_ref, kw_ref, vw_ref, et_ref, mt_ref,
                o_ref, qa_scr, m_scr, acc_scr, mix_scr, *, seq):
    tq, tk, hpg = NSA_TQ, NSA_TK, NSA_HPG
    rows = hpg * tq
    nc = seq // CMP_STRIDE
    ns = seq // SEL_BLOCK
    n_top = min(SEL_TOPK, ns)
    nsp = mt_ref.shape[0]
    c_exp = LOG2E / math.sqrt(NSA_DH)
    t0 = pl.program_id(2) * tq
    ones_k = jnp.ones((tk, LANES), BF16)

    t_col = t0 + lax.broadcasted_iota(jnp.int32, (tq, 1), 0)

    n_idx = lax.broadcasted_iota(jnp.int32, (1, nc), 1)
    valid = (n_idx * CMP_STRIDE + (CMP_LEN - 1) <= t_col) & (n_idx < nc - 1)
    bias_c = jnp.where(valid, 0.0, NEG_BIG)
    row_ok = t0 + lax.broadcasted_iota(jnp.int32, (tq, LANES), 0) >= CMP_LEN - 1
    lane_ok = t0 + lax.broadcasted_iota(jnp.int32, (ns, tq), 1) >= CMP_LEN - 1
    kc = kc_ref[0, 0, 0]
    vc_aug = jnp.concatenate([vc_ref[0, 0, 0], jnp.ones((nc, LANES), BF16)], axis=1)
    hq = tq // 2
    span = NSA_WINDOW + hq
    win = []
    for hh in range(2):
        kw0 = pl.multiple_of(jnp.maximum(t0 + hh * hq - NSA_WINDOW, 0), hq)
        t_half = t_col[hh * hq:(hh + 1) * hq]
        kpos_w = kw0 + lax.broadcasted_iota(jnp.int32, (1, span), 1)
        win.append((kw_ref[pl.ds(kw0, span), :],
                    jnp.concatenate([vw_ref[pl.ds(kw0, span), :], jnp.ones((span, LANES), BF16)], axis=1),
                    jnp.where((kpos_w <= t_half) & (kpos_w > t_half - NSA_WINDOW), 0.0, NEG_BIG)))
    q_all = jnp.concatenate([qp_ref[:, h * LANES:(h + 1) * LANES] for h in range(hpg)], axis=0)
    sc = _dot_nt(q_all, kc) + jnp.concatenate([bias_c] * hpg, axis=0)
    m = jnp.max(sc, axis=-1, keepdims=True)
    pe = jnp.exp2((sc - m) * c_exp).astype(BF16)
    acc_c = _dot(pe, vc_aug)
    imp_t = _dot_nt(mt_ref[...], pe)
    imp_parts = []
    for h in range(hpg):
        sl = slice(h * LANES, (h + 1) * LANES)
        rs = slice(h * tq, (h + 1) * tq)
        oc = jnp.where(row_ok, acc_c[rs, 0:LANES] * (1.0 / acc_c[rs, LANES:2 * LANES]), 0.0)
        imp_parts.append(imp_t[0:ns, rs] * (1.0 / imp_t[nsp - 1:nsp, rs]))
        ows = []
        for hh, (kwin, vw_aug, bias_w) in enumerate(win):
            sc = _dot_nt(qr_ref[hh * hq:(hh + 1) * hq, sl], kwin) + bias_w
            m = jnp.max(sc, axis=-1, keepdims=True)
            pe_w = jnp.exp2((sc - m) * c_exp).astype(BF16)
            acc = _dot(pe_w, vw_aug)
            ows.append(acc[:, 0:LANES] * (1.0 / acc[:, LANES:2 * LANES]))
        ow = jnp.concatenate(ows, axis=0)
        mix_scr[:, sl] = oc * z0_ref[:, sl] + ow * z2_ref[:, sl]
    while len(imp_parts) > 1:
        imp_parts = [imp_parts[i] + imp_parts[i + 1] for i in range(0, len(imp_parts), 2)]
    imp = jnp.where(lane_ok, imp_parts[0], 0.0)

    j_io = lax.broadcasted_iota(jnp.int32, (ns, tq), 0)
    t_io = t0 + lax.broadcasted_iota(jnp.int32, (ns, tq), 1)
    dist = lax.shift_right_logical(t_io, int(math.log2(SEL_BLOCK))) - j_io
    forced = (j_io == 0) | ((dist >= 0) & (dist < SEL_LOCAL))
    imp = jnp.where(forced, FORCE_SCORE, imp)
    imp = jnp.where(dist >= 0, imp, -1.0)
    sub = 8
    ranks = []
    for v in range(ns // sub):
        blk = imp[v * sub:(v + 1) * sub, :]
        j_v = v * sub + lax.broadcasted_iota(jnp.int32, (sub, tq), 0)
        r = jnp.zeros((sub, tq), F32)
        for k in range(ns):
            rk = imp[k:k + 1, :]
            if k < v * sub:
                ahead = rk >= blk
            elif k >= (v + 1) * sub:
                ahead = rk > blk
            else:
                ahead = (rk > blk) | ((rk == blk) & (j_v > k))
            r = r + jnp.where(ahead, 1.0, 0.0)
        ranks.append(r)
    rank = jnp.concatenate(ranks, axis=0)
    not_sel_t = jnp.concatenate([jnp.where(rank < n_top, 0.0, 1.0), jnp.ones((nsp - ns, tq), F32)], axis=0)
    not_sel = not_sel_t.T.astype(BF16)

    for h in range(hpg):
        qa_scr[h * tq:(h + 1) * tq, 0:LANES] = qr_ref[:, h * LANES:(h + 1) * LANES]
        qa_scr[h * tq:(h + 1) * tq, LANES:2 * LANES] = not_sel

    m_scr[...] = jnp.full((rows, LANES), NEG_BIG, F32)
    acc_scr[...] = jnp.zeros((rows, 2 * LANES), F32)
    n_kt = lax.shift_right_logical(t0 + (tq + tk - 1), int(math.log2(tk)))

    def slc_tile(kt, causal):
        k0 = pl.multiple_of(kt * tk, tk)
        k_aug = jnp.concatenate([ks_ref[pl.ds(k0, tk), :], et_ref[pl.ds(k0, tk), :]], axis=1)
        v_aug = jnp.concatenate([vs_ref[pl.ds(k0, tk), :], ones_k], axis=1)
        if causal:
            kpos = k0 + lax.broadcasted_iota(jnp.int32, (1, tk), 1)
            bias = jnp.where(kpos <= t_col, 0.0, NEG_BIG)
        for h in range(hpg):
            rs = slice(h * tq, (h + 1) * tq)
            sc = _dot_nt(qa_scr[rs, :], k_aug)
            if causal:
                sc = sc + bias
            m_old = m_scr[rs, :]
            m_new = jnp.maximum(m_old, jnp.max(sc, axis=-1, keepdims=True))
            alpha = jnp.exp2((m_old - m_new) * c_exp)
            pe = jnp.exp2((sc - _lane_tile(m_new, tk // LANES)) * c_exp)
            acc_scr[rs, :] = _lane_tile(alpha, 2) * acc_scr[rs, :] + _dot(pe.astype(BF16), v_aug)
            m_scr[rs, :] = m_new

    def pair_body(i, carry):
        slc_tile(2 * i, causal=False)
        slc_tile(2 * i + 1, causal=False)
        return carry

    lax.fori_loop(0, lax.shift_right_logical(n_kt - 1, 1), pair_body, 0)
    odd = ((n_kt - 1) & 1) == 1

    @pl.when(odd)
    def _():
        slc_tile(n_kt - 2, causal=False)
        slc_tile(n_kt - 1, causal=True)

    @pl.when(jnp.logical_not(odd))
    def _():
        slc_tile(n_kt - 1, causal=True)

    for h in range(hpg):
        rs = slice(h * tq, (h + 1) * tq)
        sl = slice(h * LANES, (h + 1) * LANES)
        o_slc = acc_scr[rs, 0:LANES] * (1.0 / acc_scr[rs, LANES:2 * LANES])
        o_ref[:, sl] = (mix_scr[:, sl] + o_slc * z1_ref[:, sl]).astype(BF16)


def nsa_attention(qp, qr, zg, cmp_kv, kv, e_t, cmp_to_sel_t, batch, seq):
    t = qp.shape[0]
    tq = NSA_TQ
    assert seq % NSA_TK == 0 and NSA_TK % tq == 0 and seq >= NSA_WINDOW + tq
    qt_per_seq = seq // tq
    nc = seq // CMP_STRIDE
    hw = NSA_HPG * NSA_DH

    qz_spec = pl.BlockSpec((tq, hw), lambda b, g, i: (b * qt_per_seq + i, g))
    cmp_specs = [pl.BlockSpec((1, 1, 1, nc, LANES), functools.partial(
        lambda b, g, i, s: (b, s, g, 0, 0), s=s)) for s in range(2)]
    kv_specs = [pl.BlockSpec((seq, LANES), functools.partial(
        lambda b, g, i, s: (b, s * NSA_G + g), s=s)) for s in range(4)]
    const_specs = [pl.BlockSpec(e_t.shape, lambda b, g, i: (0, 0)),
                   pl.BlockSpec(cmp_to_sel_t.shape, lambda b, g, i: (0, 0))]
    rows = NSA_HPG * tq
    return pl.pallas_call(
        functools.partial(_nsa_kernel, seq=seq),
        grid=(batch, NSA_G, qt_per_seq),
        in_specs=[qz_spec] * 5 + cmp_specs + kv_specs + const_specs,
        out_specs=qz_spec,
        out_shape=jax.ShapeDtypeStruct((t, NSA_WIDTH), BF16),
        scratch_shapes=[pltpu.VMEM((rows, 2 * LANES), BF16),
                        pltpu.VMEM((rows, LANES), F32),
                        pltpu.VMEM((rows, 2 * LANES), F32),
                        pltpu.VMEM((tq, hw), F32)],
        compiler_params=_cparams(("arbitrary", "arbitrary", "arbitrary")),
        name="nsa_attention",
    )(qp, qr, *zg, cmp_kv, cmp_kv, kv, kv, kv, kv, e_t, cmp_to_sel_t)


def _swa_kernel(sink_ref, q_ref, z_ref, *rest):
    kv_refs, o_ref = rest[:-1], rest[-1]
    tq = ATT_TQ
    n_pair = SWA_KVH // 2
    per_kv = SWA_HPG * SWA_DH // LANES
    scale = 1.0 / math.sqrt(SWA_DH)
    c_exp = LOG2E * scale
    t0 = pl.program_id(1) * tq
    ones_k = jnp.ones((2 * tq, LANES), BF16)

    def window(kind, var, pair):
        base = (kind * 4 + var) * 2
        sl = slice(pair * LANES, (pair + 1) * LANES)
        return jnp.concatenate([kv_refs[base][:, sl], kv_refs[base + 1][:, sl]], axis=0)

    t_col = t0 + lax.broadcasted_iota(jnp.int32, (tq, 1), 0)
    kpos = t0 - tq + lax.broadcasted_iota(jnp.int32, (1, 2 * tq), 1)
    ok = (kpos <= t_col) & (kpos > t_col - SWA_WINDOW) & (kpos >= 0)
    bias = jnp.where(ok, 0.0, NEG_BIG)
    col0 = lax.broadcasted_iota(jnp.int32, (tq, LANES), 1) == 0
    keep = jnp.where(lax.broadcasted_iota(jnp.int32, (2 * tq, LANES), 0) == 0, 0.0, 1.0).astype(BF16)

    for pair in range(n_pair):
        for j in range(2):
            k_var = [window(0, 2 * j + par, pair) * keep for par in range(2)]
            v_var = [jnp.concatenate([window(1, 2 * j + par, pair) * keep, ones_k], axis=1) for par in range(2)]
            for c in range(per_kv):
                col = ((pair * 2 + j) * per_kv + c) * LANES
                qc = q_ref[:, col:col + LANES]
                o_chunk = None
                for par in range(2):
                    head = (pair * 2 + j) * SWA_HPG + 2 * c + par
                    sink_raw = sink_ref[head] * math.sqrt(SWA_DH)
                    bias_h = jnp.concatenate([jnp.where(col0, sink_raw, bias[:, 0:LANES]), bias[:, LANES:]], axis=1)
                    sc = _dot_nt(qc, k_var[par]) + bias_h
                    m = jnp.max(sc, axis=-1, keepdims=True)
                    e = jnp.exp2((sc - m) * c_exp)
                    acc = _dot(e.astype(BF16), v_var[par])
                    o_par = acc[:, 0:LANES] * (1.0 / acc[:, LANES:2 * LANES])
                    o_chunk = o_par if o_chunk is None else o_chunk + o_par
                o_ref[:, col:col + LANES] = (o_chunk * z_ref[:, col:col + LANES]).astype(BF16)


def swa_attention(q, zs, kvb, sinks, batch, seq):
    t = q.shape[0]
    tq = ATT_TQ
    qt_per_seq = seq // tq
    kvw = SWA_KVH * SWA_DH

    def row(b, i):
        return b * qt_per_seq + i

    def prev_row(b, i):
        return b * qt_per_seq + jnp.maximum(i - 1, 0)

    qz_spec = pl.BlockSpec((tq, SWA_WIDTH), lambda b, i: (row(b, i), 0))
    kv_specs = []
    for cb in range(8):
        kv_specs.append(pl.BlockSpec((tq, kvw), functools.partial(lambda b, i, cb: (prev_row(b, i), cb), cb=cb)))
        kv_specs.append(pl.BlockSpec((tq, kvw), functools.partial(lambda b, i, cb: (row(b, i), cb), cb=cb)))
    return pl.pallas_call(
        _swa_kernel,
        grid=(batch, qt_per_seq),
        in_specs=[pl.BlockSpec(memory_space=pltpu.SMEM), qz_spec, qz_spec] + kv_specs,
        out_specs=qz_spec,
        out_shape=jax.ShapeDtypeStruct((t, SWA_WIDTH), BF16),
        compiler_params=_cparams(("arbitrary", "arbitrary")),
        name="swa_attention",
    )(sinks, q, zs, *([kvb] * 16))


def _selection_constants(seq):
    nc = seq // CMP_STRIDE
    ns = seq // SEL_BLOCK
    nsp = -(-ns // LANES) * LANES
    e_t = np.zeros((seq, nsp), np.float32)
    e_t[np.arange(seq), np.arange(seq) // SEL_BLOCK] = NEG_BIG
    c0 = np.arange(nc - 1) * CMP_STRIDE
    s0 = np.arange(ns) * SEL_BLOCK
    overlap = (c0[:, None] < s0[None, :] + SEL_BLOCK) & (c0[:, None] + CMP_LEN > s0[None, :])
    assert ns < nsp
    cmp_to_sel_t = np.zeros((nsp, nc), np.float32)
    cmp_to_sel_t[:ns, :nc - 1] = overlap.T
    cmp_to_sel_t[nsp - 1, :] = 1.0
    return jnp.asarray(e_t, BF16), jnp.asarray(cmp_to_sel_t, BF16)


def kernel(x, p, a_norm, a_w_in, a_w_out, a_cmp_pos_k, a_cmp_w1_k, a_cmp_w2_k, a_cmp_pos_v, a_cmp_w1_v,
           a_cmp_w2_v, kv_norm, w_kv, b_norm, b_w_in, b_w_out, b_sinks, ple_norm, ple_gate_w, ple_proj,
           final_norm):
    batch, seq, d = x.shape
    assert d == D_MODEL and p.shape[0] == 2 and a_w_in.shape[0] == 1 and b_w_in.shape[0] == 1
    t = batch * seq
    xf = x.reshape(t, d)

    n_gate = 3 * NSA_G * NSA_HPG
    w_q, w_cmp, w_kvs, w_gate, *w_z = split_a_weights(a_w_in)
    wg = w_gate[:, :n_gate].reshape(d, 3, NSA_G, NSA_HPG).transpose(0, 2, 1, 3)
    wg = wg.reshape(d, NSA_G, 3 * NSA_HPG)
    wg = jnp.pad(wg, ((0, 0), (0, 0), (0, LANES - 3 * NSA_HPG))).reshape(d, NSA_G * LANES)
    pos_kv = jnp.stack([a_cmp_pos_k[0], a_cmp_pos_v[0]])
    w1_kv = jnp.stack([a_cmp_w1_k[0], a_cmp_w1_v[0]]).astype(BF16)
    w2_kv = jnp.stack([a_cmp_w2_k[0], a_cmp_w2_v[0]]).astype(BF16)
    wb = b_w_in[0]
    tabs_a = _rope_tables(seq, NSA_DH)
    tabs_b = _rope_tables(seq, SWA_DH)
    e_t, cmp_to_sel_t = _selection_constants(seq)

    (n_a,) = rms_norm_rows(xf, a_norm, BF16)
    q_rot, q_plain = matmul_rope(n_a, w_q, tabs_a, seq, NSA_DH, MM_TN, True, "a_q_proj")
    (kv_a,) = matmul_rope(n_a, w_kvs, tabs_a, seq, NSA_DH, NSA_KVW, False, "a_kv_proj")
    kcvc = matmul(n_a, w_cmp, "a_cmp_proj")
    gates = matmul(n_a, wg, "a_gate_proj", act="sigmoid")
    zg = [matmul_zgate(n_a, w_z[br], gates, br, "a_z_proj%d" % br) for br in range(3)]
    cmp_kv = compress_kv(kcvc, pos_kv, w1_kv, w2_kv, batch, seq)
    mixed = nsa_attention(q_plain, q_rot, zg, cmp_kv, kv_a, e_t, cmp_to_sel_t, batch, seq)
    h, (n_p,), ssq = matmul_residual(mixed, a_w_out[0].astype(BF16), xf, ple_norm[0:1], "a_out_proj")
    pf = p.reshape(2, t, PLE_DIM)
    h, (n_kv, n_b), ssq = ple_update(n_p, ssq, ple_gate_w[0].astype(BF16), pf[0], ple_proj[0].astype(BF16), h,
                                     jnp.stack([kv_norm, b_norm[0]]), "ple0")

    kvb = matmul_kvb(n_kv, w_kv.astype(BF16), ssq, tabs_b, seq, "kv_proj")
    (q_b,) = matmul_rope(n_b, wb[:, :SWA_WIDTH].astype(BF16), tabs_b, seq, SWA_DH, MM_TN, False, "b_q_proj",
                         ssq=ssq)
    zs_b = matmul(n_b, wb[:, SWA_WIDTH:].astype(BF16), "b_z_proj", act="silu", ssq=ssq)
    ob = swa_attention(q_b, zs_b, kvb, b_sinks[0], batch, seq)
    h, (n_p,), ssq = matmul_residual(ob, b_w_out[0].astype(BF16), h, ple_norm[1:2], "b_out_proj")
    h = ple_update(n_p, ssq, ple_gate_w[1].astype(BF16), pf[1], ple_proj[1].astype(BF16), h, None, "ple1")

    (out,) = rms_norm_rows(h, final_norm[None, :], F32)
    return out.reshape(batch, seq, d)
```

```python
import functools
import math

import numpy as np
import jax
import jax.numpy as jnp
from jax import lax
from jax.experimental import pallas as pl
from jax.experimental.pallas import tpu as pltpu

F32 = jnp.float32
BF16 = jnp.bfloat16

D_MODEL = 4096
RMS_EPS = 1e-6
ROPE_THETA = 500000.0
NEG_BIG = -1e30
FORCE_SCORE = 1e9
PLE_DIM = 256

NSA_DH = 128
NSA_G = 4
NSA_HPG = 8
NSA_WIDTH = 4096
NSA_KVW = 512
CMP_LEN = 32
CMP_STRIDE = 16
CMP_HIDDEN = 256
SEL_BLOCK = 64
SEL_TOPK = 16
SEL_LOCAL = 2
NSA_WINDOW = 512

SWA_DH = 64
SWA_KVH = 8
SWA_HPG = 8
SWA_WIDTH = 4096
SWA_WINDOW = 128

LANES = 128
VMEM_LIMIT_BYTES = 58 * 1024 * 1024

MM_TM = 1024
MM_TN = 1024
MM_TN_FUSED = 512
MM_ROW_CHUNKS = 8
NORM_ROWS = 256
ATT_TQ = 128
NSA_TQ = 256
NSA_TK = 512

LOG2E = math.log2(math.e)


def _cparams(sem):
    return pltpu.CompilerParams(dimension_semantics=sem, vmem_limit_bytes=VMEM_LIMIT_BYTES)


def _sigmoid(v):
    return 1.0 / (1.0 + jnp.exp(-v))


def _silu(v):
    return v * _sigmoid(v)


def _dot(a, b):
    return jnp.dot(a, b, preferred_element_type=F32)


def _dot_nt(a, b):
    return lax.dot_general(a, b, (((1,), (1,)), ((), ())), preferred_element_type=F32)


def _lane_tile(x, n):
    return jnp.concatenate([x] * n, axis=1)


def _norm_kernel(h_ref, g_ref, *o_refs):
    x = h_ref[...]
    r = lax.rsqrt(jnp.mean(x * x, axis=-1, keepdims=True) + RMS_EPS)
    y = x * r
    for i, o_ref in enumerate(o_refs):
        o_ref[...] = (y * g_ref[i:i + 1, :]).astype(o_ref.dtype)


def rms_norm_rows(h, gains, out_dtype):
    t, d = h.shape
    k = gains.shape[0]
    outs = pl.pallas_call(
        _norm_kernel,
        grid=(t // NORM_ROWS,),
        in_specs=[pl.BlockSpec((NORM_ROWS, d), lambda i: (i, 0)),
                  pl.BlockSpec((k, d), lambda i: (0, 0))],
        out_specs=[pl.BlockSpec((NORM_ROWS, d), lambda i: (i, 0))] * k,
        out_shape=[jax.ShapeDtypeStruct((t, d), out_dtype)] * k,
        compiler_params=_cparams(("arbitrary",)),
        name="rms_norm",
    )(h, gains)
    return outs


def _rope_tables(seq, head_dim):
    rot = head_dim // 4
    half = rot // 2
    inv_freq = ROPE_THETA ** (-jnp.arange(half, dtype=F32) / half)
    ang = jnp.arange(seq, dtype=jnp.int32).astype(F32)[:, None] * inv_freq[None, :]
    cos, sin = jnp.cos(ang), jnp.sin(ang)
    rest = head_dim - rot
    zeros_h = jnp.zeros((seq, half), F32)
    c = jnp.concatenate([cos, cos, jnp.ones((seq, rest), F32)], axis=1)
    s1 = jnp.concatenate([-sin, zeros_h, jnp.zeros((seq, rest), F32)], axis=1)
    s2 = jnp.concatenate([zeros_h, sin, jnp.zeros((seq, rest), F32)], axis=1)
    reps = LANES // head_dim
    return tuple(jnp.tile(a, (1, reps)) for a in (c, s1, s2))


def _rope_lanes(t, c, s1, s2, half):
    return t * c + pltpu.roll(t, LANES - half, 1) * s1 + pltpu.roll(t, half, 1) * s2


def _row_chunks(ref):
    rows = ref.shape[0] // MM_ROW_CHUNKS
    return [slice(c * rows, (c + 1) * rows) for c in range(MM_ROW_CHUNKS)]


def _scaled_dot(x_ref, w_ref, s_ref, rs):
    acc = _dot(x_ref[rs, :], w_ref[...])
    if s_ref is not None:
        r = lax.rsqrt(s_ref[rs, :] * (1.0 / x_ref.shape[1]) + RMS_EPS)
        acc = acc * _lane_tile(r, acc.shape[1] // LANES)
    return acc


def _write_scaled_copies(h, rs, g_ref, hb_refs):
    for i, hb_ref in enumerate(hb_refs):
        hb_ref[rs, :] = (h * g_ref[i:i + 1, :]).astype(BF16)
    return jnp.broadcast_to(jnp.sum(h * h, axis=-1, keepdims=True), (h.shape[0], LANES))


def _accumulate_ssq(parts, ssq_ref):
    part = jnp.concatenate(parts, axis=0)
    j = pl.program_id(1)

    @pl.when(j == 0)
    def _():
        ssq_ref[...] = part

    @pl.when(j > 0)
    def _():
        ssq_ref[...] = ssq_ref[...] + part


def _mm_kernel(*refs, act, scaled):
    x_ref, w_ref = refs[0], refs[1]
    s_ref = refs[2] if scaled else None
    o_ref = refs[-1]
    for rs in _row_chunks(o_ref):
        acc = _scaled_dot(x_ref, w_ref, s_ref, rs)
        if act == "silu":
            acc = _silu(acc)
        elif act == "sigmoid":
            acc = _sigmoid(acc)
        o_ref[rs, :] = acc.astype(o_ref.dtype)


def _mm_zgate_kernel(x_ref, w_ref, g_ref, o_ref, *, first_col):
    for rs in _row_chunks(o_ref):
        acc = _silu(_dot(x_ref[rs, :], w_ref[...]))
        for h in range(acc.shape[1] // LANES):
            sl = slice(h * LANES, (h + 1) * LANES)
            o_ref[rs, sl] = acc[:, sl] * g_ref[rs, first_col + h:first_col + h + 1]


def _mm_res_kernel(x_ref, w_ref, r_ref, g_ref, h_ref, *rest):
    parts = []
    for rs in _row_chunks(h_ref):
        h = r_ref[rs, :] + _dot(x_ref[rs, :], w_ref[...])
        h_ref[rs, :] = h
        parts.append(_write_scaled_copies(h, rs, g_ref, rest[:-1]))
    _accumulate_ssq(parts, rest[-1])


def _ple_kernel(n_ref, wg_ref, s_ref, p_ref, wp_ref, h_ref, *rest, n_gain):
    o_ref = rest[1] if n_gain else rest[0]
    parts = []
    for rs in _row_chunks(o_ref):
        gate = _sigmoid(_scaled_dot(n_ref, wg_ref, s_ref, rs))
        emb = _dot(p_ref[rs, :].astype(BF16), wp_ref[...])
        h = h_ref[rs, :] + gate * emb
        o_ref[rs, :] = h
        if n_gain:
            parts.append(_write_scaled_copies(h, rs, rest[0], rest[2:-1]))
    if n_gain:
        _accumulate_ssq(parts, rest[-1])


def _mm_rope_kernel(*refs, half, n_rope, plain_too, scaled):
    x_ref, w_ref = refs[0], refs[1]
    s_ref = refs[2] if scaled else None
    c_ref, s1_ref, s2_ref = refs[3:6] if scaled else refs[2:5]
    o_refs = refs[6:] if scaled else refs[5:]
    for rs in _row_chunks(o_refs[0]):
        acc = _scaled_dot(x_ref, w_ref, s_ref, rs)
        c, s1, s2 = c_ref[rs, :], s1_ref[rs, :], s2_ref[rs, :]
        for ch in range(acc.shape[1] // LANES):
            sl = slice(ch * LANES, (ch + 1) * LANES)
            t = acc[:, sl]
            o_refs[0][rs, sl] = (_rope_lanes(t, c, s1, s2, half) if ch < n_rope else t).astype(BF16)
            if plain_too:
                o_refs[1][rs, sl] = t.astype(BF16)


def _mm_kvb_kernel(x_ref, w_ref, s_ref, c_ref, s1_ref, s2_ref, o_ref, *, half):
    kvw = SWA_KVH * SWA_DH
    n_ch = kvw // LANES
    for rs in _row_chunks(o_ref):
        acc = _scaled_dot(x_ref, w_ref, s_ref, rs)
        c, s1, s2 = c_ref[rs, :], s1_ref[rs, :], s2_ref[rs, :]
        low = lax.broadcasted_iota(jnp.int32, (acc.shape[0], LANES), 1) < SWA_DH
        for kind in range(2):
            for ch in range(n_ch):
                t = acc[:, kind * kvw + ch * LANES: kind * kvw + (ch + 1) * LANES]
                if kind == 0:
                    t = _rope_lanes(t, c, s1, s2, half)
                ts = pltpu.roll(t, SWA_DH, 1)
                variants = (jnp.where(low, t, 0.0), jnp.where(low, 0.0, ts),
                            jnp.where(low, ts, 0.0), jnp.where(low, 0.0, t))
                for var, val in enumerate(variants):
                    col = (kind * 4 + var) * kvw + ch * LANES
                    o_ref[rs, col:col + LANES] = val.astype(BF16)


def _mm_tiles(m, n, tn_max=MM_TN):
    tm = min(MM_TM, m)
    tn = min(tn_max, n)
    assert m % tm == 0 and n % tn == 0, (m, n)
    return tm, tn


def _ssq_spec(tm):
    return pl.BlockSpec((tm, LANES), lambda i, j: (i, 0))


def matmul(x, w, name, act=None, out_dtype=F32, ssq=None):
    m, k = x.shape
    n = w.shape[1]
    tm, tn = _mm_tiles(m, n)
    scaled = ssq is not None
    return pl.pallas_call(
        functools.partial(_mm_kernel, act=act, scaled=scaled),
        grid=(m // tm, n // tn),
        in_specs=[pl.BlockSpec((tm, k), lambda i, j: (i, 0)),
                  pl.BlockSpec((k, tn), lambda i, j: (0, j))] + ([_ssq_spec(tm)] if scaled else []),
        out_specs=pl.BlockSpec((tm, tn), lambda i, j: (i, j)),
        out_shape=jax.ShapeDtypeStruct((m, n), out_dtype),
        compiler_params=_cparams(("arbitrary", "arbitrary")),
        name=name,
    )(x, w, *([ssq] if scaled else []))


def matmul_zgate(x, w, gates, branch, name):
    m, k = x.shape
    n = w.shape[1]
    tm, tn = _mm_tiles(m, n)
    assert tn == NSA_HPG * NSA_DH
    return pl.pallas_call(
        functools.partial(_mm_zgate_kernel, first_col=branch * NSA_HPG),
        grid=(m // tm, n // tn),
        in_specs=[pl.BlockSpec((tm, k), lambda i, j: (i, 0)),
                  pl.BlockSpec((k, tn), lambda i, j: (0, j)),
                  pl.BlockSpec((tm, LANES), lambda i, j: (i, j))],
        out_specs=pl.BlockSpec((tm, tn), lambda i, j: (i, j)),
        out_shape=jax.ShapeDtypeStruct((m, n), F32),
        compiler_params=_cparams(("arbitrary", "arbitrary")),
        name=name,
    )(x, w, gates)


def matmul_rope(x, w, tabs, seq, head_dim, rope_cols, plain_too, name, ssq=None):
    m, k = x.shape
    n = w.shape[1]
    tm, tn = _mm_tiles(m, n)
    assert seq % tm == 0
    per_seq = seq // tm
    scaled = ssq is not None
    tab_spec = pl.BlockSpec((tm, LANES), lambda i, j: (i % per_seq, 0))
    n_out = 2 if plain_too else 1
    outs = pl.pallas_call(
        functools.partial(_mm_rope_kernel, half=head_dim // 8, n_rope=rope_cols // LANES,
                          plain_too=plain_too, scaled=scaled),
        grid=(m // tm, n // tn),
        in_specs=[pl.BlockSpec((tm, k), lambda i, j: (i, 0)),
                  pl.BlockSpec((k, tn), lambda i, j: (0, j))]
        + ([_ssq_spec(tm)] if scaled else []) + [tab_spec] * 3,
        out_specs=[pl.BlockSpec((tm, tn), lambda i, j: (i, j))] * n_out,
        out_shape=[jax.ShapeDtypeStruct((m, n), BF16)] * n_out,
        compiler_params=_cparams(("arbitrary", "arbitrary")),
        name=name,
    )(x, w, *([ssq] if scaled else []), *tabs)
    return outs


def matmul_kvb(x, w, ssq, tabs, seq, name):
    m, k = x.shape
    n = w.shape[1]
    tm = min(MM_TM, m)
    assert seq % tm == 0 and n == 2 * SWA_KVH * SWA_DH
    per_seq = seq // tm
    tab_spec = pl.BlockSpec((tm, LANES), lambda i: (i % per_seq, 0))
    return pl.pallas_call(
        functools.partial(_mm_kvb_kernel, half=SWA_DH // 8),
        grid=(m // tm,),
        in_specs=[pl.BlockSpec((tm, k), lambda i: (i, 0)),
                  pl.BlockSpec((k, n), lambda i: (0, 0)),
                  pl.BlockSpec((tm, LANES), lambda i: (i, 0))] + [tab_spec] * 3,
        out_specs=pl.BlockSpec((tm, 4 * n), lambda i: (i, 0)),
        out_shape=jax.ShapeDtypeStruct((m, 4 * n), BF16),
        compiler_params=_cparams(("arbitrary",)),
        name=name,
    )(x, w, ssq, *tabs)


def matmul_residual(x, w, res, gains, name):
    m, k = x.shape
    n = w.shape[1]
    kg = gains.shape[0]
    tm, tn = _mm_tiles(m, n, MM_TN_FUSED)
    tile = pl.BlockSpec((tm, tn), lambda i, j: (i, j))
    outs = pl.pallas_call(
        _mm_res_kernel,
        grid=(m // tm, n // tn),
        in_specs=[pl.BlockSpec((tm, k), lambda i, j: (i, 0)),
                  pl.BlockSpec((k, tn), lambda i, j: (0, j)),
                  tile,
                  pl.BlockSpec((kg, tn), lambda i, j: (0, j))],
        out_specs=[tile] + [tile] * kg + [_ssq_spec(tm)],
        out_shape=[jax.ShapeDtypeStruct((m, n), F32)] + [jax.ShapeDtypeStruct((m, n), BF16)] * kg
        + [jax.ShapeDtypeStruct((m, LANES), F32)],
        compiler_params=_cparams(("arbitrary", "arbitrary")),
        name=name,
    )(x, w, res, gains)
    return outs[0], outs[1:-1], outs[-1]


def ple_update(n, ssq, wg, p, wp, h, gains, name):
    m, k = n.shape
    d = wg.shape[1]
    kp = p.shape[1]
    kg = 0 if gains is None else gains.shape[0]
    tm, tn = _mm_tiles(m, d, MM_TN_FUSED)
    tile = pl.BlockSpec((tm, tn), lambda i, j: (i, j))
    in_specs = [pl.BlockSpec((tm, k), lambda i, j: (i, 0)),
                pl.BlockSpec((k, tn), lambda i, j: (0, j)),
                _ssq_spec(tm),
                pl.BlockSpec((tm, kp), lambda i, j: (i, 0)),
                pl.BlockSpec((kp, tn), lambda i, j: (0, j)),
                tile]
    args = [n, wg, ssq, p, wp, h]
    out_specs = [tile]
    out_shape = [jax.ShapeDtypeStruct((m, d), F32)]
    if kg:
        in_specs.append(pl.BlockSpec((kg, tn), lambda i, j: (0, j)))
        args.append(gains)
        out_specs += [tile] * kg + [_ssq_spec(tm)]
        out_shape += [jax.ShapeDtypeStruct((m, d), BF16)] * kg + [jax.ShapeDtypeStruct((m, LANES), F32)]
    outs = pl.pallas_call(
        functools.partial(_ple_kernel, n_gain=kg),
        grid=(m // tm, d // tn),
        in_specs=in_specs,
        out_specs=out_specs,
        out_shape=out_shape,
        compiler_params=_cparams(("arbitrary", "arbitrary")),
        name=name,
    )(*args)
    if kg:
        return outs[0], outs[1:-1], outs[-1]
    return outs[0]


A_W_COLS = 512


def _wt_cast_kernel(wt_ref, o_ref):
    o_ref[...] = wt_ref[...].T.astype(BF16)


def _cast_weight_cols(wt, first, width, name):
    k = wt.shape[1]
    cols = min(A_W_COLS, width)
    assert width % cols == 0
    return pl.pallas_call(
        _wt_cast_kernel,
        grid=(width // cols,),
        in_specs=[pl.BlockSpec((pl.Element(cols), pl.Element(k)), lambda i: (pl.multiple_of(first + i * cols, 8), 0))],
        out_specs=pl.BlockSpec((k, cols), lambda i: (0, i)),
        out_shape=jax.ShapeDtypeStruct((k, width), BF16),
        compiler_params=_cparams(("arbitrary",)),
        name=name,
    )(wt)


def split_a_weights(wa):
    wt = wa[0].T
    c_q, c_cmp, c_kv = NSA_WIDTH, NSA_WIDTH + 2 * NSA_KVW, NSA_WIDTH + 6 * NSA_KVW
    c_z = c_kv + 3 * NSA_G * NSA_HPG
    segs = [(0, c_q), (c_q, c_cmp - c_q), (c_cmp, c_kv - c_cmp), (c_kv, LANES)]
    segs += [(c_z + br * NSA_WIDTH, NSA_WIDTH) for br in range(3)]
    return [_cast_weight_cols(wt, first, width, "a_w_cast%d" % i) for i, (first, width) in enumerate(segs)]


def _compress_kernel(x_ref, pos_ref, w1_ref, w2_ref, o_ref, *, n_cmp_pad):
    nc = n_cmp_pad
    lo = jnp.zeros((nc, CMP_HIDDEN), F32)
    hi = jnp.zeros((nc, CMP_HIDDEN), F32)
    for l in range(CMP_STRIDE):
        xl = x_ref[pl.ds(l, nc, stride=CMP_STRIDE), :]
        a = (xl + pos_ref[0, l:l + 1, :]).astype(BF16)
        b = (xl + pos_ref[0, CMP_STRIDE + l:CMP_STRIDE + l + 1, :]).astype(BF16)
        lo = lo + _dot(a, w1_ref[0, l * LANES:(l + 1) * LANES, :])
        hi = hi + _dot(b, w1_ref[0, (CMP_STRIDE + l) * LANES:(CMP_STRIDE + l + 1) * LANES, :])
    hid = lo + pltpu.roll(hi, nc - 1, 0)
    out = _dot(_silu(hid).astype(BF16), w2_ref[0])
    n_idx = lax.broadcasted_iota(jnp.int32, (nc, LANES), 0)
    o_ref[0, 0, 0] = jnp.where(n_idx < nc - 1, out, 0.0).astype(BF16)


def compress_kv(kcvc, pos, w1, w2, batch, seq):
    nc = seq // CMP_STRIDE
    return pl.pallas_call(
        functools.partial(_compress_kernel, n_cmp_pad=nc),
        grid=(batch, 2, NSA_G),
        in_specs=[pl.BlockSpec((seq, LANES), lambda b, s, g: (b, s * NSA_G + g)),
                  pl.BlockSpec((1, CMP_LEN, LANES), lambda b, s, g: (s, 0, 0)),
                  pl.BlockSpec((1, CMP_LEN * LANES, CMP_HIDDEN), lambda b, s, g: (s, 0, 0)),
                  pl.BlockSpec((1, CMP_HIDDEN, LANES), lambda b, s, g: (s, 0, 0))],
        out_specs=pl.BlockSpec((1, 1, 1, nc, LANES), lambda b, s, g: (b, s, g, 0, 0)),
        out_shape=jax.ShapeDtypeStruct((batch, 2, NSA_G, nc, LANES), BF16),
        compiler_params=_cparams(("arbitrary", "arbitrary", "arbitrary")),
        name="compress_kv",
    )(kcvc, pos, w1, w2)


def _nsa_kernel(qp_ref, qr_ref, z0_ref, z1_ref, z2_ref,
                kc_ref, vc_ref, ks_ref, vs_ref, kw_ref, vw_ref, et_ref, mt_ref,
                o_ref, qa_scr, m_scr, acc_scr, mix_scr, *, seq):
    tq, tk, hpg = NSA_TQ, NSA_TK, NSA_HPG
    rows = hpg * tq
    nc = seq // CMP_STRIDE
    ns = seq // SEL_BLOCK
    n_top = min(SEL_TOPK, ns)
    nsp = mt_ref.shape[0]
    c_exp = LOG2E / math.sqrt(NSA_DH)
    t0 = pl.program_id(2) * tq
    ones_k = jnp.ones((tk, LANES), BF16)

    t_col = t0 + lax.broadcasted_iota(jnp.int32, (tq, 1), 0)

    n_idx = lax.broadcasted_iota(jnp.int32, (1, nc), 1)
    valid = (n_idx * CMP_STRIDE + (CMP_LEN - 1) <= t_col) & (n_idx < nc - 1)
    bias_c = jnp.where(valid, 0.0, NEG_BIG)
    row_ok = t0 + lax.broadcasted_iota(jnp.int32, (tq, LANES), 0) >= CMP_LEN - 1
    lane_ok = t0 + lax.broadcasted_iota(jnp.int32, (ns, tq), 1) >= CMP_LEN - 1
    kc = kc_ref[0, 0, 0]
    vc_aug = jnp.concatenate([vc_ref[0, 0, 0], jnp.ones((nc, LANES), BF16)], axis=1)
    hq = tq // 2
    span = NSA_WINDOW + hq
    win = []
    for hh in range(2):
        kw0 = pl.multiple_of(jnp.maximum(t0 + hh * hq - NSA_WINDOW, 0), hq)
        t_half = t_col[hh * hq:(hh + 1) * hq]
        kpos_w = kw0 + lax.broadcasted_iota(jnp.int32, (1, span), 1)
        win.append((kw_ref[pl.ds(kw0, span), :],
                    jnp.concatenate([vw_ref[pl.ds(kw0, span), :], jnp.ones((span, LANES), BF16)], axis=1),
                    jnp.where((kpos_w <= t_half) & (kpos_w > t_half - NSA_WINDOW), 0.0, NEG_BIG)))
    q_all = jnp.concatenate([qp_ref[:, h * LANES:(h + 1) * LANES] for h in range(hpg)], axis=0)
    sc = _dot_nt(q_all, kc) + jnp.concatenate([bias_c] * hpg, axis=0)
    m = jnp.max(sc, axis=-1, keepdims=True)
    pe = jnp.exp2((sc - m) * c_exp).astype(BF16)
    acc_c = _dot(pe, vc_aug)
    imp_t = _dot_nt(mt_ref[...], pe)
    imp_parts = []
    for h in range(hpg):
        sl = slice(h * LANES, (h + 1) * LANES)
        rs = slice(h * tq, (h + 1) * tq)
        oc = jnp.where(row_ok, acc_c[rs, 0:LANES] * (1.0 / acc_c[rs, LANES:2 * LANES]), 0.0)
        imp_parts.append(imp_t[0:ns, rs] * (1.0 / imp_t[nsp - 1:nsp, rs]))
        ows = []
        for hh, (kwin, vw_aug, bias_w) in enumerate(win):
            sc = _dot_nt(qr_ref[hh * hq:(hh + 1) * hq, sl], kwin) + bias_w
            m = jnp.max(sc, axis=-1, keepdims=True)
            pe_w = jnp.exp2((sc - m) * c_exp).astype(BF16)
            acc = _dot(pe_w, vw_aug)
            ows.append(acc[:, 0:LANES] * (1.0 / acc[:, LANES:2 * LANES]))
        ow = jnp.concatenate(ows, axis=0)
        mix_scr[:, sl] = oc * z0_ref[:, sl] + ow * z2_ref[:, sl]
    while len(imp_parts) > 1:
        imp_parts = [imp_parts[i] + imp_parts[i + 1] for i in range(0, len(imp_parts), 2)]
    imp = jnp.where(lane_ok, imp_parts[0], 0.0)

    j_io = lax.broadcasted_iota(jnp.int32, (ns, tq), 0)
    t_io = t0 + lax.broadcasted_iota(jnp.int32, (ns, tq), 1)
    dist = lax.shift_right_logical(t_io, int(math.log2(SEL_BLOCK))) - j_io
    forced = (j_io == 0) | ((dist >= 0) & (dist < SEL_LOCAL))
    imp = jnp.where(forced, FORCE_SCORE, imp)
    imp = jnp.where(dist >= 0, imp, -1.0)
    sub = 8
    ranks = []
    for v in range(ns // sub):
        blk = imp[v * sub:(v + 1) * sub, :]
        j_v = v * sub + lax.broadcasted_iota(jnp.int32, (sub, tq), 0)
        r = jnp.zeros((sub, tq), F32)
        for k in range(ns):
            rk = imp[k:k + 1, :]
            if k < v * sub:
                ahead = rk >= blk
            elif k >= (v + 1) * sub:
                ahead = rk > blk
            else:
                ahead = (rk > blk) | ((rk == blk) & (j_v > k))
            r = r + jnp.where(ahead, 1.0, 0.0)
        ranks.append(r)
    rank = jnp.concatenate(ranks, axis=0)
    not_sel_t = jnp.concatenate([jnp.where(rank < n_top, 0.0, 1.0), jnp.ones((nsp - ns, tq), F32)], axis=0)
    not_sel = not_sel_t.T.astype(BF16)

    for h in range(hpg):
        qa_scr[h * tq:(h + 1) * tq, 0:LANES] = qr_ref[:, h * LANES:(h + 1) * LANES]
        qa_scr[h * tq:(h + 1) * tq, LANES:2 * LANES] = not_sel

    m_scr[...] = jnp.full((rows, LANES), NEG_BIG, F32)
    acc_scr[...] = jnp.zeros((rows, 2 * LANES), F32)
    n_kt = lax.shift_right_logical(t0 + (tq + tk - 1), int(math.log2(tk)))

    def slc_tile(kt, causal):
        k0 = pl.multiple_of(kt * tk, tk)
        k_aug = jnp.concatenate([ks_ref[pl.ds(k0, tk), :], et_ref[pl.ds(k0, tk), :]], axis=1)
        v_aug = jnp.concatenate([vs_ref[pl.ds(k0, tk), :], ones_k], axis=1)
        if causal:
            kpos = k0 + lax.broadcasted_iota(jnp.int32, (1, tk), 1)
            bias = jnp.where(kpos <= t_col, 0.0, NEG_BIG)
        for h in range(hpg):
            rs = slice(h * tq, (h + 1) * tq)
            sc = _dot_nt(qa_scr[rs, :], k_aug)
            if causal:
                sc = sc + bias
            m_old = m_scr[rs, :]
            m_new = jnp.maximum(m_old, jnp.max(sc, axis=-1, keepdims=True))
            alpha = jnp.exp2((m_old - m_new) * c_exp)
            pe = jnp.exp2((sc - _lane_tile(m_new, tk // LANES)) * c_exp)
            acc_scr[rs, :] = _lane_tile(alpha, 2) * acc_scr[rs, :] + _dot(pe.astype(BF16), v_aug)
            m_scr[rs, :] = m_new

    def pair_body(i, carry):
        slc_tile(2 * i, causal=False)
        slc_tile(2 * i + 1, causal=False)
        return carry

    lax.fori_loop(0, lax.shift_right_logical(n_kt - 1, 1), pair_body, 0)
    odd = ((n_kt - 1) & 1) == 1

    @pl.when(odd)
    def _():
        slc_tile(n_kt - 2, causal=False)
        slc_tile(n_kt - 1, causal=True)

    @pl.when(jnp.logical_not(odd))
    def _():
        slc_tile(n_kt - 1, causal=True)

    for h in range(hpg):
        rs = slice(h * tq, (h + 1) * tq)
        sl = slice(h * LANES, (h + 1) * LANES)
        o_slc = acc_scr[rs, 0:LANES] * (1.0 / acc_scr[rs, LANES:2 * LANES])
        o_ref[:, sl] = (mix_scr[:, sl] + o_slc * z1_ref[:, sl]).astype(BF16)


def nsa_attention(qp, qr, zg, cmp_kv, kv, e_t, cmp_to_sel_t, batch, seq):
    t = qp.shape[0]
    tq = NSA_TQ
    assert seq % NSA_TK == 0 and NSA_TK % tq == 0 and seq >= NSA_WINDOW + tq
    qt_per_seq = seq // tq
    nc = seq // CMP_STRIDE
    hw = NSA_HPG * NSA_DH

    qz_spec = pl.BlockSpec((tq, hw), lambda b, g, i: (b * qt_per_seq + i, g))
    cmp_specs = [pl.BlockSpec((1, 1, 1, nc, LANES), functools.partial(
        lambda b, g, i, s: (b, s, g, 0, 0), s=s)) for s in range(2)]
    kv_specs = [pl.BlockSpec((seq, LANES), functools.partial(
        lambda b, g, i, s: (b, s * NSA_G + g), s=s)) for s in range(4)]
    const_specs = [pl.BlockSpec(e_t.shape, lambda b, g, i: (0, 0)),
                   pl.BlockSpec(cmp_to_sel_t.shape, lambda b, g, i: (0, 0))]
    rows = NSA_HPG * tq
    return pl.pallas_call(
        functools.partial(_nsa_kernel, seq=seq),
        grid=(batch, NSA_G, qt_per_seq),
        in_specs=[qz_spec] * 5 + cmp_specs + kv_specs + const_specs,
        out_specs=qz_spec,
        out_shape=jax.ShapeDtypeStruct((t, NSA_WIDTH), BF16),
        scratch_shapes=[pltpu.VMEM((rows, 2 * LANES), BF16),
                        pltpu.VMEM((rows, LANES), F32),
                        pltpu.VMEM((rows, 2 * LANES), F32),
                        pltpu.VMEM((tq, hw), F32)],
        compiler_params=_cparams(("arbitrary", "arbitrary", "arbitrary")),
        name="nsa_attention",
    )(qp, qr, *zg, cmp_kv, cmp_kv, kv, kv, kv, kv, e_t, cmp_to_sel_t)


def _swa_kernel(sink_ref, q_ref, z_ref, *rest):
    kv_refs, o_ref = rest[:-1], rest[-1]
    tq = ATT_TQ
    n_pair = SWA_KVH // 2
    per_kv = SWA_HPG * SWA_DH // LANES
    scale = 1.0 / math.sqrt(SWA_DH)
    c_exp = LOG2E * scale
    t0 = pl.program_id(1) * tq
    ones_k = jnp.ones((2 * tq, LANES), BF16)

    def window(kind, var, pair):
        base = (kind * 4 + var) * 2
        sl = slice(pair * LANES, (pair + 1) * LANES)
        return jnp.concatenate([kv_refs[base][:, sl], kv_refs[base + 1][:, sl]], axis=0)

    t_col = t0 + lax.broadcasted_iota(jnp.int32, (tq, 1), 0)
    kpos = t0 - tq + lax.broadcasted_iota(jnp.int32, (1, 2 * tq), 1)
    ok = (kpos <= t_col) & (kpos > t_col - SWA_WINDOW) & (kpos >= 0)
    bias = jnp.where(ok, 0.0, NEG_BIG)
    col0 = lax.broadcasted_iota(jnp.int32, (tq, LANES), 1) == 0
    keep = jnp.where(lax.broadcasted_iota(jnp.int32, (2 * tq, LANES), 0) == 0, 0.0, 1.0).astype(BF16)

    for pair in range(n_pair):
        for j in range(2):
            k_var = [window(0, 2 * j + par, pair) * keep for par in range(2)]
            v_var = [jnp.concatenate([window(1, 2 * j + par, pair) * keep, ones_k], axis=1) for par in range(2)]
            for c in range(per_kv):
                col = ((pair * 2 + j) * per_kv + c) * LANES
                qc = q_ref[:, col:col + LANES]
                o_chunk = None
                for par in range(2):
                    head = (pair * 2 + j) * SWA_HPG + 2 * c + par
                    sink_raw = sink_ref[head] * math.sqrt(SWA_DH)
                    bias_h = jnp.concatenate([jnp.where(col0, sink_raw, bias[:, 0:LANES]), bias[:, LANES:]], axis=1)
                    sc = _dot_nt(qc, k_var[par]) + bias_h
                    m = jnp.max(sc, axis=-1, keepdims=True)
                    e = jnp.exp2((sc - m) * c_exp)
                    acc = _dot(e.astype(BF16), v_var[par])
                    o_par = acc[:, 0:LANES] * (1.0 / acc[:, LANES:2 * LANES])
                    o_chunk = o_par if o_chunk is None else o_chunk + o_par
                o_ref[:, col:col + LANES] = (o_chunk * z_ref[:, col:col + LANES]).astype(BF16)


def swa_attention(q, zs, kvb, sinks, batch, seq):
    t = q.shape[0]
    tq = ATT_TQ
    qt_per_seq = seq // tq
    kvw = SWA_KVH * SWA_DH

    def row(b, i):
        return b * qt_per_seq + i

    def prev_row(b, i):
        return b * qt_per_seq + jnp.maximum(i - 1, 0)

    qz_spec = pl.BlockSpec((tq, SWA_WIDTH), lambda b, i: (row(b, i), 0))
    kv_specs = []
    for cb in range(8):
        kv_specs.append(pl.BlockSpec((tq, kvw), functools.partial(lambda b, i, cb: (prev_row(b, i), cb), cb=cb)))
        kv_specs.append(pl.BlockSpec((tq, kvw), functools.partial(lambda b, i, cb: (row(b, i), cb), cb=cb)))
    return pl.pallas_call(
        _swa_kernel,
        grid=(batch, qt_per_seq),
        in_specs=[pl.BlockSpec(memory_space=pltpu.SMEM), qz_spec, qz_spec] + kv_specs,
        out_specs=qz_spec,
        out_shape=jax.ShapeDtypeStruct((t, SWA_WIDTH), BF16),
        compiler_params=_cparams(("arbitrary", "arbitrary")),
        name="swa_attention",
    )(sinks, q, zs, *([kvb] * 16))


def _selection_constants(seq):
    nc = seq // CMP_STRIDE
    ns = seq // SEL_BLOCK
    nsp = -(-ns // LANES) * LANES
    e_t = np.zeros((seq, nsp), np.float32)
    e_t[np.arange(seq), np.arange(seq) // SEL_BLOCK] = NEG_BIG
    c0 = np.arange(nc - 1) * CMP_STRIDE
    s0 = np.arange(ns) * SEL_BLOCK
    overlap = (c0[:, None] < s0[None, :] + SEL_BLOCK) & (c0[:, None] + CMP_LEN > s0[None, :])
    assert ns < nsp
    cmp_to_sel_t = np.zeros((nsp, nc), np.float32)
    cmp_to_sel_t[:ns, :nc - 1] = overlap.T
    cmp_to_sel_t[nsp - 1, :] = 1.0
    return jnp.asarray(e_t, BF16), jnp.asarray(cmp_to_sel_t, BF16)


def kernel(x, p, a_norm, a_w_in, a_w_out, a_cmp_pos_k, a_cmp_w1_k, a_cmp_w2_k, a_cmp_pos_v, a_cmp_w1_v,
           a_cmp_w2_v, kv_norm, w_kv, b_norm, b_w_in, b_w_out, b_sinks, ple_norm, ple_gate_w, ple_proj,
           final_norm):
    batch, seq, d = x.shape
    assert d == D_MODEL and p.shape[0] == 2 and a_w_in.shape[0] == 1 and b_w_in.shape[0] == 1
    t = batch * seq
    xf = x.reshape(t, d)

    n_gate = 3 * NSA_G * NSA_HPG
    w_q, w_cmp, w_kvs, w_gate, *w_z = split_a_weights(a_w_in)
    wg = w_gate[:, :n_gate].reshape(d, 3, NSA_G, NSA_HPG).transpose(0, 2, 1, 3)
    wg = wg.reshape(d, NSA_G, 3 * NSA_HPG)
    wg = jnp.pad(wg, ((0, 0), (0, 0), (0, LANES - 3 * NSA_HPG))).reshape(d, NSA_G * LANES)
    pos_kv = jnp.stack([a_cmp_pos_k[0], a_cmp_pos_v[0]])
    w1_kv = jnp.stack([a_cmp_w1_k[0], a_cmp_w1_v[0]]).astype(BF16)
    w2_kv = jnp.stack([a_cmp_w2_k[0], a_cmp_w2_v[0]]).astype(BF16)
    wb = b_w_in[0]
    tabs_a = _rope_tables(seq, NSA_DH)
    tabs_b = _rope_tables(seq, SWA_DH)
    e_t, cmp_to_sel_t = _selection_constants(seq)

    (n_a,) = rms_norm_rows(xf, a_norm, BF16)
    q_rot, q_plain = matmul_rope(n_a, w_q, tabs_a, seq, NSA_DH, MM_TN, True, "a_q_proj")
    (kv_a,) = matmul_rope(n_a, w_kvs, tabs_a, seq, NSA_DH, NSA_KVW, False, "a_kv_proj")
    kcvc = matmul(n_a, w_cmp, "a_cmp_proj")
    gates = matmul(n_a, wg, "a_gate_proj", act="sigmoid")
    zg = [matmul_zgate(n_a, w_z[br], gates, br, "a_z_proj%d" % br) for br in range(3)]
    cmp_kv = compress_kv(kcvc, pos_kv, w1_kv, w2_kv, batch, seq)
    mixed = nsa_attention(q_plain, q_rot, zg, cmp_kv, kv_a, e_t, cmp_to_sel_t, batch, seq)
    h, (n_p,), ssq = matmul_residual(mixed, a_w_out[0].astype(BF16), xf, ple_norm[0:1], "a_out_proj")
    pf = p.reshape(2, t, PLE_DIM)
    h, (n_kv, n_b), ssq = ple_update(n_p, ssq, ple_gate_w[0].astype(BF16), pf[0], ple_proj[0].astype(BF16), h,
                                     jnp.stack([kv_norm, b_norm[0]]), "ple0")

    kvb = matmul_kvb(n_kv, w_kv.astype(BF16), ssq, tabs_b, seq, "kv_proj")
    (q_b,) = matmul_rope(n_b, wb[:, :SWA_WIDTH].astype(BF16), tabs_b, seq, SWA_DH, MM_TN, False, "b_q_proj",
                         ssq=ssq)
    zs_b = matmul(n_b, wb[:, SWA_WIDTH:].astype(BF16), "b_z_proj", act="silu", ssq=ssq)
    ob = swa_attention(q_b, zs_b, kvb, b_sinks[0], batch, seq)
    h, (n_p,), ssq = matmul_residual(ob, b_w_out[0].astype(BF16), h, ple_norm[1:2], "b_out_proj")
    h = ple_update(n_p, ssq, ple_gate_w[1].astype(BF16), pf[1], ple_proj[1].astype(BF16), h, None, "ple1")

    (out,) = rms_norm_rows(h, final_norm[None, :], F32)
    return out.reshape(batch, seq, d)
```

```python
import functools
import math

import numpy as np
import jax
import jax.numpy as jnp
from jax import lax
from jax.experimental import pallas as pl
from jax.experimental.pallas import tpu as pltpu

F32 = jnp.float32
BF16 = jnp.bfloat16

D_MODEL = 4096
RMS_EPS = 1e-6
ROPE_THETA = 500000.0
NEG_BIG = -1e30
FORCE_SCORE = 1e9
PLE_DIM = 256

NSA_DH = 128
NSA_G = 4
NSA_HPG = 8
NSA_WIDTH = 4096
NSA_KVW = 512
CMP_LEN = 32
CMP_STRIDE = 16
CMP_HIDDEN = 256
SEL_BLOCK = 64
SEL_TOPK = 16
SEL_LOCAL = 2
NSA_WINDOW = 512

SWA_DH = 64
SWA_KVH = 8
SWA_HPG = 8
SWA_WIDTH = 4096
SWA_WINDOW = 128

LANES = 128
VMEM_LIMIT_BYTES = 58 * 1024 * 1024

MM_TM = 1024
MM_TN = 1024
MM_TN_FUSED = 512
MM_ROW_CHUNKS = 8
NORM_ROWS = 256
ATT_TQ = 128
NSA_TQ = 256
NSA_TK = 512

LOG2E = math.log2(math.e)


def _cparams(sem):
    return pltpu.CompilerParams(dimension_semantics=sem, vmem_limit_bytes=VMEM_LIMIT_BYTES)


def _sigmoid(v):
    return 1.0 / (1.0 + jnp.exp(-v))


def _silu(v):
    return v * _sigmoid(v)


def _dot(a, b):
    return jnp.dot(a, b, preferred_element_type=F32)


def _dot_nt(a, b):
    return lax.dot_general(a, b, (((1,), (1,)), ((), ())), preferred_element_type=F32)


def _lane_tile(x, n):
    return jnp.concatenate([x] * n, axis=1)


def _norm_kernel(h_ref, g_ref, *o_refs):
    x = h_ref[...]
    r = lax.rsqrt(jnp.mean(x * x, axis=-1, keepdims=True) + RMS_EPS)
    y = x * r
    for i, o_ref in enumerate(o_refs):
        o_ref[...] = (y * g_ref[i:i + 1, :]).astype(o_ref.dtype)


def rms_norm_rows(h, gains, out_dtype):
    t, d = h.shape
    k = gains.shape[0]
    outs = pl.pallas_call(
        _norm_kernel,
        grid=(t // NORM_ROWS,),
        in_specs=[pl.BlockSpec((NORM_ROWS, d), lambda i: (i, 0)),
                  pl.BlockSpec((k, d), lambda i: (0, 0))],
        out_specs=[pl.BlockSpec((NORM_ROWS, d), lambda i: (i, 0))] * k,
        out_shape=[jax.ShapeDtypeStruct((t, d), out_dtype)] * k,
        compiler_params=_cparams(("arbitrary",)),
        name="rms_norm",
    )(h, gains)
    return outs


def _rope_tables(seq, head_dim):
    rot = head_dim // 4
    half = rot // 2
    inv_freq = ROPE_THETA ** (-jnp.arange(half, dtype=F32) / half)
    ang = jnp.arange(seq, dtype=jnp.int32).astype(F32)[:, None] * inv_freq[None, :]
    cos, sin = jnp.cos(ang), jnp.sin(ang)
    rest = head_dim - rot
    zeros_h = jnp.zeros((seq, half), F32)
    c = jnp.concatenate([cos, cos, jnp.ones((seq, rest), F32)], axis=1)
    s1 = jnp.concatenate([-sin, zeros_h, jnp.zeros((seq, rest), F32)], axis=1)
    s2 = jnp.concatenate([zeros_h, sin, jnp.zeros((seq, rest), F32)], axis=1)
    reps = LANES // head_dim
    return tuple(jnp.tile(a, (1, reps)) for a in (c, s1, s2))


def _rope_lanes(t, c, s1, s2, half):
    return t * c + pltpu.roll(t, LANES - half, 1) * s1 + pltpu.roll(t, half, 1) * s2


def _row_chunks(ref, n=MM_ROW_CHUNKS):
    rows = ref.shape[0] // n
    return [slice(c * rows, (c + 1) * rows) for c in range(n)]


def _scaled_dot(x_ref, w_ref, s_ref, rs):
    acc = _dot(x_ref[rs, :], w_ref[...])
    if s_ref is not None:
        r = lax.rsqrt(s_ref[rs, :] * (1.0 / x_ref.shape[1]) + RMS_EPS)
        acc = acc * _lane_tile(r, acc.shape[1] // LANES)
    return acc


def _write_scaled_copies(h, rs, g_ref, hb_refs):
    for i, hb_ref in enumerate(hb_refs):
        hb_ref[rs, :] = (h * g_ref[i:i + 1, :]).astype(BF16)
    return jnp.broadcast_to(jnp.sum(h * h, axis=-1, keepdims=True), (h.shape[0], LANES))


def _accumulate_ssq(parts, ssq_ref):
    part = jnp.concatenate(parts, axis=0)
    j = pl.program_id(1)

    @pl.when(j == 0)
    def _():
        ssq_ref[...] = part

    @pl.when(j > 0)
    def _():
        ssq_ref[...] = ssq_ref[...] + part


def _mm_kernel(*refs, act, scaled):
    x_ref, w_ref = refs[0], refs[1]
    s_ref = refs[2] if scaled else None
    o_ref = refs[-1]
    for rs in _row_chunks(o_ref):
        acc = _scaled_dot(x_ref, w_ref, s_ref, rs)
        if act == "silu":
            acc = _silu(acc)
        elif act == "sigmoid":
            acc = _sigmoid(acc)
        o_ref[rs, :] = acc.astype(o_ref.dtype)


def _mm_zgate_kernel(x_ref, w_ref, g_ref, o_ref, *, first_col):
    for rs in _row_chunks(o_ref):
        acc = _silu(_dot(x_ref[rs, :], w_ref[...]))
        for h in range(acc.shape[1] // LANES):
            sl = slice(h * LANES, (h + 1) * LANES)
            o_ref[rs, sl] = acc[:, sl] * g_ref[rs, first_col + h:first_col + h + 1]


def _mm_res_kernel(x_ref, w_ref, r_ref, g_ref, h_ref, *rest):
    parts = []
    for rs in _row_chunks(h_ref, 1):
        h = r_ref[rs, :] + _dot(x_ref[rs, :], w_ref[...])
        h_ref[rs, :] = h
        parts.append(_write_scaled_copies(h, rs, g_ref, rest[:-1]))
    _accumulate_ssq(parts, rest[-1])


def _ple_kernel(n_ref, wg_ref, s_ref, p_ref, wp_ref, h_ref, *rest, n_gain):
    o_ref = rest[1] if n_gain else rest[0]
    parts = []
    for rs in _row_chunks(o_ref):
        gate = _sigmoid(_scaled_dot(n_ref, wg_ref, s_ref, rs))
        emb = _dot(p_ref[rs, :].astype(BF16), wp_ref[...])
        h = h_ref[rs, :] + gate * emb
        o_ref[rs, :] = h
        if n_gain:
            parts.append(_write_scaled_copies(h, rs, rest[0], rest[2:-1]))
    if n_gain:
        _accumulate_ssq(parts, rest[-1])


def _mm_rope_kernel(*refs, half, n_rope, plain_too, scaled):
    x_ref, w_ref = refs[0], refs[1]
    s_ref = refs[2] if scaled else None
    c_ref, s1_ref, s2_ref = refs[3:6] if scaled else refs[2:5]
    o_refs = refs[6:] if scaled else refs[5:]
    for rs in _row_chunks(o_refs[0]):
        acc = _scaled_dot(x_ref, w_ref, s_ref, rs)
        c, s1, s2 = c_ref[rs, :], s1_ref[rs, :], s2_ref[rs, :]
        for ch in range(acc.shape[1] // LANES):
            sl = slice(ch * LANES, (ch + 1) * LANES)
            t = acc[:, sl]
            o_refs[0][rs, sl] = (_rope_lanes(t, c, s1, s2, half) if ch < n_rope else t).astype(BF16)
            if plain_too:
                o_refs[1][rs, sl] = t.astype(BF16)


def _mm_kvb_kernel(x_ref, w_ref, s_ref, c_ref, s1_ref, s2_ref, o_ref, *, half):
    kvw = SWA_KVH * SWA_DH
    n_ch = kvw // LANES
    for rs in _row_chunks(o_ref):
        acc = _scaled_dot(x_ref, w_ref, s_ref, rs)
        c, s1, s2 = c_ref[rs, :], s1_ref[rs, :], s2_ref[rs, :]
        low = lax.broadcasted_iota(jnp.int32, (acc.shape[0], LANES), 1) < SWA_DH
        for kind in range(2):
            for ch in range(n_ch):
                t = acc[:, kind * kvw + ch * LANES: kind * kvw + (ch + 1) * LANES]
                if kind == 0:
                    t = _rope_lanes(t, c, s1, s2, half)
                ts = pltpu.roll(t, SWA_DH, 1)
                variants = (jnp.where(low, t, 0.0), jnp.where(low, 0.0, ts),
                            jnp.where(low, ts, 0.0), jnp.where(low, 0.0, t))
                for var, val in enumerate(variants):
                    col = (kind * 4 + var) * kvw + ch * LANES
                    o_ref[rs, col:col + LANES] = val.astype(BF16)


def _mm_tiles(m, n, tn_max=MM_TN):
    tm = min(MM_TM, m)
    tn = min(tn_max, n)
    assert m % tm == 0 and n % tn == 0, (m, n)
    return tm, tn


def _ssq_spec(tm):
    return pl.BlockSpec((tm, LANES), lambda i, j: (i, 0))


def matmul(x, w, name, act=None, out_dtype=F32, ssq=None):
    m, k = x.shape
    n = w.shape[1]
    tm, tn = _mm_tiles(m, n)
    scaled = ssq is not None
    return pl.pallas_call(
        functools.partial(_mm_kernel, act=act, scaled=scaled),
        grid=(m // tm, n // tn),
        in_specs=[pl.BlockSpec((tm, k), lambda i, j: (i, 0)),
                  pl.BlockSpec((k, tn), lambda i, j: (0, j))] + ([_ssq_spec(tm)] if scaled else []),
        out_specs=pl.BlockSpec((tm, tn), lambda i, j: (i, j)),
        out_shape=jax.ShapeDtypeStruct((m, n), out_dtype),
        compiler_params=_cparams(("arbitrary", "arbitrary")),
        name=name,
    )(x, w, *([ssq] if scaled else []))


def matmul_zgate(x, w, gates, branch, name):
    m, k = x.shape
    n = w.shape[1]
    tm, tn = _mm_tiles(m, n)
    assert tn == NSA_HPG * NSA_DH
    return pl.pallas_call(
        functools.partial(_mm_zgate_kernel, first_col=branch * NSA_HPG),
        grid=(m // tm, n // tn),
        in_specs=[pl.BlockSpec((tm, k), lambda i, j: (i, 0)),
                  pl.BlockSpec((k, tn), lambda i, j: (0, j)),
                  pl.BlockSpec((tm, LANES), lambda i, j: (i, j))],
        out_specs=pl.BlockSpec((tm, tn), lambda i, j: (i, j)),
        out_shape=jax.ShapeDtypeStruct((m, n), F32),
        compiler_params=_cparams(("arbitrary", "arbitrary")),
        name=name,
    )(x, w, gates)


def matmul_rope(x, w, tabs, seq, head_dim, rope_cols, plain_too, name, ssq=None):
    m, k = x.shape
    n = w.shape[1]
    tm, tn = _mm_tiles(m, n)
    assert seq % tm == 0
    per_seq = seq // tm
    scaled = ssq is not None
    tab_spec = pl.BlockSpec((tm, LANES), lambda i, j: (i % per_seq, 0))
    n_out = 2 if plain_too else 1
    outs = pl.pallas_call(
        functools.partial(_mm_rope_kernel, half=head_dim // 8, n_rope=rope_cols // LANES,
                          plain_too=plain_too, scaled=scaled),
        grid=(m // tm, n // tn),
        in_specs=[pl.BlockSpec((tm, k), lambda i, j: (i, 0)),
                  pl.BlockSpec((k, tn), lambda i, j: (0, j))]
        + ([_ssq_spec(tm)] if scaled else []) + [tab_spec] * 3,
        out_specs=[pl.BlockSpec((tm, tn), lambda i, j: (i, j))] * n_out,
        out_shape=[jax.ShapeDtypeStruct((m, n), BF16)] * n_out,
        compiler_params=_cparams(("arbitrary", "arbitrary")),
        name=name,
    )(x, w, *([ssq] if scaled else []), *tabs)
    return outs


def matmul_kvb(x, w, ssq, tabs, seq, name):
    m, k = x.shape
    n = w.shape[1]
    tm = min(MM_TM, m)
    assert seq % tm == 0 and n == 2 * SWA_KVH * SWA_DH
    per_seq = seq // tm
    tab_spec = pl.BlockSpec((tm, LANES), lambda i: (i % per_seq, 0))
    return pl.pallas_call(
        functools.partial(_mm_kvb_kernel, half=SWA_DH // 8),
        grid=(m // tm,),
        in_specs=[pl.BlockSpec((tm, k), lambda i: (i, 0)),
                  pl.BlockSpec((k, n), lambda i: (0, 0)),
                  pl.BlockSpec((tm, LANES), lambda i: (i, 0))] + [tab_spec] * 3,
        out_specs=pl.BlockSpec((tm, 4 * n), lambda i: (i, 0)),
        out_shape=jax.ShapeDtypeStruct((m, 4 * n), BF16),
        compiler_params=_cparams(("arbitrary",)),
        name=name,
    )(x, w, ssq, *tabs)


def matmul_residual(x, w, res, gains, name):
    m, k = x.shape
    n = w.shape[1]
    kg = gains.shape[0]
    tm, tn = _mm_tiles(m, n, MM_TN_FUSED)
    tile = pl.BlockSpec((tm, tn), lambda i, j: (i, j))
    outs = pl.pallas_call(
        _mm_res_kernel,
        grid=(m // tm, n // tn),
        in_specs=[pl.BlockSpec((tm, k), lambda i, j: (i, 0)),
                  pl.BlockSpec((k, tn), lambda i, j: (0, j)),
                  tile,
                  pl.BlockSpec((kg, tn), lambda i, j: (0, j))],
        out_specs=[tile] + [tile] * kg + [_ssq_spec(tm)],
        out_shape=[jax.ShapeDtypeStruct((m, n), F32)] + [jax.ShapeDtypeStruct((m, n), BF16)] * kg
        + [jax.ShapeDtypeStruct((m, LANES), F32)],
        compiler_params=_cparams(("arbitrary", "arbitrary")),
        name=name,
    )(x, w, res, gains)
    return outs[0], outs[1:-1], outs[-1]


def ple_update(n, ssq, wg, p, wp, h, gains, name):
    m, k = n.shape
    d = wg.shape[1]
    kp = p.shape[1]
    kg = 0 if gains is None else gains.shape[0]
    tm, tn = _mm_tiles(m, d, MM_TN_FUSED)
    tile = pl.BlockSpec((tm, tn), lambda i, j: (i, j))
    in_specs = [pl.BlockSpec((tm, k), lambda i, j: (i, 0)),
                pl.BlockSpec((k, tn), lambda i, j: (0, j)),
                _ssq_spec(tm),
                pl.BlockSpec((tm, kp), lambda i, j: (i, 0)),
                pl.BlockSpec((kp, tn), lambda i, j: (0, j)),
                tile]
    args = [n, wg, ssq, p, wp, h]
    out_specs = [tile]
    out_shape = [jax.ShapeDtypeStruct((m, d), F32)]
    if kg:
        in_specs.append(pl.BlockSpec((kg, tn), lambda i, j: (0, j)))
        args.append(gains)
        out_specs += [tile] * kg + [_ssq_spec(tm)]
        out_shape += [jax.ShapeDtypeStruct((m, d), BF16)] * kg + [jax.ShapeDtypeStruct((m, LANES), F32)]
    outs = pl.pallas_call(
        functools.partial(_ple_kernel, n_gain=kg),
        grid=(m // tm, d // tn),
        in_specs=in_specs,
        out_specs=out_specs,
        out_shape=out_shape,
        compiler_params=_cparams(("arbitrary", "arbitrary")),
        name=name,
    )(*args)
    if kg:
        return outs[0], outs[1:-1], outs[-1]
    return outs[0]


A_W_COLS = 512


def _wt_cast_kernel(wt_ref, o_ref):
    o_ref[...] = wt_ref[...].T.astype(BF16)


def _cast_weight_cols(wt, first, width, name):
    k = wt.shape[1]
    cols = min(A_W_COLS, width)
    assert width % cols == 0
    return pl.pallas_call(
        _wt_cast_kernel,
        grid=(width // cols,),
        in_specs=[pl.BlockSpec((pl.Element(cols), pl.Element(k)), lambda i: (pl.multiple_of(first + i * cols, 8), 0))],
        out_specs=pl.BlockSpec((k, cols), lambda i: (0, i)),
        out_shape=jax.ShapeDtypeStruct((k, width), BF16),
        compiler_params=_cparams(("arbitrary",)),
        name=name,
    )(wt)


def split_a_weights(wa):
    wt = wa[0].T
    c_q, c_cmp, c_kv = NSA_WIDTH, NSA_WIDTH + 2 * NSA_KVW, NSA_WIDTH + 6 * NSA_KVW
    c_z = c_kv + 3 * NSA_G * NSA_HPG
    segs = [(0, c_q), (c_q, c_cmp - c_q), (c_cmp, c_kv - c_cmp), (c_kv, LANES)]
    segs += [(c_z + br * NSA_WIDTH, NSA_WIDTH) for br in range(3)]
    return [_cast_weight_cols(wt, first, width, "a_w_cast%d" % i) for i, (first, width) in enumerate(segs)]


def _compress_kernel(x_ref, pos_ref, w1_ref, w2_ref, o_ref, *, n_cmp_pad):
    nc = n_cmp_pad
    lo = jnp.zeros((nc, CMP_HIDDEN), F32)
    hi = jnp.zeros((nc, CMP_HIDDEN), F32)
    for l in range(CMP_STRIDE):
        xl = x_ref[pl.ds(l, nc, stride=CMP_STRIDE), :]
        a = (xl + pos_ref[0, l:l + 1, :]).astype(BF16)
        b = (xl + pos_ref[0, CMP_STRIDE + l:CMP_STRIDE + l + 1, :]).astype(BF16)
        lo = lo + _dot(a, w1_ref[0, l * LANES:(l + 1) * LANES, :])
        hi = hi + _dot(b, w1_ref[0, (CMP_STRIDE + l) * LANES:(CMP_STRIDE + l + 1) * LANES, :])
    hid = lo + pltpu.roll(hi, nc - 1, 0)
    out = _dot(_silu(hid).astype(BF16), w2_ref[0])
    n_idx = lax.broadcasted_iota(jnp.int32, (nc, LANES), 0)
    o_ref[0, 0, 0] = jnp.where(n_idx < nc - 1, out, 0.0).astype(BF16)


def compress_kv(kcvc, pos, w1, w2, batch, seq):
    nc = seq // CMP_STRIDE
    return pl.pallas_call(
        functools.partial(_compress_kernel, n_cmp_pad=nc),
        grid=(batch, 2, NSA_G),
        in_specs=[pl.BlockSpec((seq, LANES), lambda b, s, g: (b, s * NSA_G + g)),
                  pl.BlockSpec((1, CMP_LEN, LANES), lambda b, s, g: (s, 0, 0)),
                  pl.BlockSpec((1, CMP_LEN * LANES, CMP_HIDDEN), lambda b, s, g: (s, 0, 0)),
                  pl.BlockSpec((1, CMP_HIDDEN, LANES), lambda b, s, g: (s, 0, 0))],
        out_specs=pl.BlockSpec((1, 1, 1, nc, LANES), lambda b, s, g: (b, s, g, 0, 0)),
        out_shape=jax.ShapeDtypeStruct((batch, 2, NSA_G, nc, LANES), BF16),
        compiler_params=_cparams(("arbitrary", "arbitrary", "arbitrary")),
        name="compress_kv",
    )(kcvc, pos, w1, w2)


def _nsa_kernel(qp_ref, qr_ref, z0_ref, z1_ref, z2_ref,
                kc_ref, vc_ref, ks_ref, vs_ref, kw_ref, vw_ref, et_ref, mt_ref,
                o_ref, qa_scr, m_scr, acc_scr, mix_scr, *, seq):
    tq, tk, hpg = NSA_TQ, NSA_TK, NSA_HPG
    rows = hpg * tq
    nc = seq // CMP_STRIDE
    ns = seq // SEL_BLOCK
    n_top = min(SEL_TOPK, ns)
    nsp = mt_ref.shape[0]
    c_exp = LOG2E / math.sqrt(NSA_DH)
    t0 = pl.program_id(2) * tq
    ones_k = jnp.ones((tk, LANES), BF16)

    t_col = t0 + lax.broadcasted_iota(jnp.int32, (tq, 1), 0)

    n_idx = lax.broadcasted_iota(jnp.int32, (1, nc), 1)
    valid = (n_idx * CMP_STRIDE + (CMP_LEN - 1) <= t_col) & (n_idx < nc - 1)
    bias_c = jnp.where(valid, 0.0, NEG_BIG)
    row_ok = t0 + lax.broadcasted_iota(jnp.int32, (tq, LANES), 0) >= CMP_LEN - 1
    lane_ok = t0 + lax.broadcasted_iota(jnp.int32, (ns, tq), 1) >= CMP_LEN - 1
    kc = kc_ref[0, 0, 0]
    vc_aug = jnp.concatenate([vc_ref[0, 0, 0], jnp.ones((nc, LANES), BF16)], axis=1)
    hq = tq // 2
    span = NSA_WINDOW + hq
    win = []
    for hh in range(2):
        kw0 = pl.multiple_of(jnp.maximum(t0 + hh * hq - NSA_WINDOW, 0), hq)
        t_half = t_col[hh * hq:(hh + 1) * hq]
        kpos_w = kw0 + lax.broadcasted_iota(jnp.int32, (1, span), 1)
        win.append((kw_ref[pl.ds(kw0, span), :],
                    jnp.concatenate([vw_ref[pl.ds(kw0, span), :], jnp.ones((span, LANES), BF16)], axis=1),
                    jnp.where((kpos_w <= t_half) & (kpos_w > t_half - NSA_WINDOW), 0.0, NEG_BIG)))
    q_all = jnp.concatenate([qp_ref[:, h * LANES:(h + 1) * LANES] for h in range(hpg)], axis=0)
    sc = _dot_nt(q_all, kc) + jnp.concatenate([bias_c] * hpg, axis=0)
    m = jnp.max(sc, axis=-1, keepdims=True)
    pe = jnp.exp2((sc - m) * c_exp).astype(BF16)
    acc_c = _dot(pe, vc_aug)
    imp_t = _dot_nt(mt_ref[...], pe)
    imp_parts = []
    for h in range(hpg):
        sl = slice(h * LANES, (h + 1) * LANES)
        rs = slice(h * tq, (h + 1) * tq)
        oc = jnp.where(row_ok, acc_c[rs, 0:LANES] * (1.0 / acc_c[rs, LANES:2 * LANES]), 0.0)
        imp_parts.append(imp_t[0:ns, rs] * (1.0 / imp_t[nsp - 1:nsp, rs]))
        ows = []
        for hh, (kwin, vw_aug, bias_w) in enumerate(win):
            sc = _dot_nt(qr_ref[hh * hq:(hh + 1) * hq, sl], kwin) + bias_w
            m = jnp.max(sc, axis=-1, keepdims=True)
            pe_w = jnp.exp2((sc - m) * c_exp).astype(BF16)
            acc = _dot(pe_w, vw_aug)
            ows.append(acc[:, 0:LANES] * (1.0 / acc[:, LANES:2 * LANES]))
        ow = jnp.concatenate(ows, axis=0)
        mix_scr[:, sl] = oc * z0_ref[:, sl] + ow * z2_ref[:, sl]
    while len(imp_parts) > 1:
        imp_parts = [imp_parts[i] + imp_parts[i + 1] for i in range(0, len(imp_parts), 2)]
    imp = jnp.where(lane_ok, imp_parts[0], 0.0)

    j_io = lax.broadcasted_iota(jnp.int32, (ns, tq), 0)
    t_io = t0 + lax.broadcasted_iota(jnp.int32, (ns, tq), 1)
    dist = lax.shift_right_logical(t_io, int(math.log2(SEL_BLOCK))) - j_io
    forced = (j_io == 0) | ((dist >= 0) & (dist < SEL_LOCAL))
    imp = jnp.where(forced, FORCE_SCORE, imp)
    imp = jnp.where(dist >= 0, imp, -1.0)
    sub = 8
    ranks = []
    for v in range(ns // sub):
        blk = imp[v * sub:(v + 1) * sub, :]
        j_v = v * sub + lax.broadcasted_iota(jnp.int32, (sub, tq), 0)
        r = jnp.zeros((sub, tq), F32)
        for k in range(ns):
            rk = imp[k:k + 1, :]
            if k < v * sub:
                ahead = rk >= blk
            elif k >= (v + 1) * sub:
                ahead = rk > blk
            else:
                ahead = (rk > blk) | ((rk == blk) & (j_v > k))
            r = r + jnp.where(ahead, 1.0, 0.0)
        ranks.append(r)
    rank = jnp.concatenate(ranks, axis=0)
    not_sel_t = jnp.concatenate([jnp.where(rank < n_top, 0.0, 1.0), jnp.ones((nsp - ns, tq), F32)], axis=0)
    not_sel = not_sel_t.T.astype(BF16)

    for h in range(hpg):
        qa_scr[h * tq:(h + 1) * tq, 0:LANES] = qr_ref[:, h * LANES:(h + 1) * LANES]
        qa_scr[h * tq:(h + 1) * tq, LANES:2 * LANES] = not_sel

    m_scr[...] = jnp.full((rows, LANES), NEG_BIG, F32)
    acc_scr[...] = jnp.zeros((rows, 2 * LANES), F32)
    n_kt = lax.shift_right_logical(t0 + (tq + tk - 1), int(math.log2(tk)))

    def slc_tile(kt, causal, width=tk):
        k0 = pl.multiple_of(kt * tk, tk)
        k_aug = jnp.concatenate([ks_ref[pl.ds(k0, width), :], et_ref[pl.ds(k0, width), :]], axis=1)
        v_aug = jnp.concatenate([vs_ref[pl.ds(k0, width), :], ones_k[0:width]], axis=1)
        if causal:
            kpos = k0 + lax.broadcasted_iota(jnp.int32, (1, width), 1)
            bias = jnp.where(kpos <= t_col, 0.0, NEG_BIG)
        for h in range(hpg):
            rs = slice(h * tq, (h + 1) * tq)
            sc = _dot_nt(qa_scr[rs, :], k_aug)
            if causal:
                sc = sc + bias
            m_old = m_scr[rs, :]
            m_new = jnp.maximum(m_old, jnp.max(sc, axis=-1, keepdims=True))
            alpha = jnp.exp2((m_old - m_new) * c_exp)
            pe = jnp.exp2((sc - _lane_tile(m_new, width // LANES)) * c_exp)
            acc_scr[rs, :] = _lane_tile(alpha, 2) * acc_scr[rs, :] + _dot(pe.astype(BF16), v_aug)
            m_scr[rs, :] = m_new

    def pair_body(i, carry):
        slc_tile(2 * i, causal=False)
        slc_tile(2 * i + 1, causal=False)
        return carry

    lax.fori_loop(0, lax.shift_right_logical(n_kt - 1, 1), pair_body, 0)
    odd = ((n_kt - 1) & 1) == 1
    short = (t0 & (tk - 1)) == 0
    for odd_case in (True, False):
        for short_case in (True, False):
            @pl.when((odd if odd_case else jnp.logical_not(odd)) & (short if short_case else jnp.logical_not(short)))
            def _(odd_case=odd_case, short_case=short_case):
                if odd_case:
                    slc_tile(n_kt - 2, causal=False)
                slc_tile(n_kt - 1, causal=True, width=tq if short_case else tk)

    for h in range(hpg):
        rs = slice(h * tq, (h + 1) * tq)
        sl = slice(h * LANES, (h + 1) * LANES)
        o_slc = acc_scr[rs, 0:LANES] * (1.0 / acc_scr[rs, LANES:2 * LANES])
        o_ref[:, sl] = (mix_scr[:, sl] + o_slc * z1_ref[:, sl]).astype(BF16)


def nsa_attention(qp, qr, zg, cmp_kv, kv, e_t, cmp_to_sel_t, batch, seq):
    t = qp.shape[0]
    tq = NSA_TQ
    assert seq % NSA_TK == 0 and NSA_TK % tq == 0 and seq >= NSA_WINDOW + tq
    qt_per_seq = seq // tq
    nc = seq // CMP_STRIDE
    hw = NSA_HPG * NSA_DH

    qz_spec = pl.BlockSpec((tq, hw), lambda b, g, i: (b * qt_per_seq + i, g))
    cmp_specs = [pl.BlockSpec((1, 1, 1, nc, LANES), functools.partial(
        lambda b, g, i, s: (b, s, g, 0, 0), s=s)) for s in range(2)]
    kv_specs = [pl.BlockSpec((seq, LANES), functools.partial(
        lambda b, g, i, s: (b, s * NSA_G + g), s=s)) for s in range(4)]
    const_specs = [pl.BlockSpec(e_t.shape, lambda b, g, i: (0, 0)),
                   pl.BlockSpec(cmp_to_sel_t.shape, lambda b, g, i: (0, 0))]
    rows = NSA_HPG * tq
    return pl.pallas_call(
        functools.partial(_nsa_kernel, seq=seq),
        grid=(batch, NSA_G, qt_per_seq),
        in_specs=[qz_spec] * 5 + cmp_specs + kv_specs + const_specs,
        out_specs=qz_spec,
        out_shape=jax.ShapeDtypeStruct((t, NSA_WIDTH), BF16),
        scratch_shapes=[pltpu.VMEM((rows, 2 * LANES), BF16),
                        pltpu.VMEM((rows, LANES), F32),
                        pltpu.VMEM((rows, 2 * LANES), F32),
                        pltpu.VMEM((tq, hw), F32)],
        compiler_params=_cparams(("arbitrary", "arbitrary", "arbitrary")),
        name="nsa_attention",
    )(qp, qr, *zg, cmp_kv, cmp_kv, kv, kv, kv, kv, e_t, cmp_to_sel_t)


def _swa_kernel(sink_ref, q_ref, z_ref, *rest):
    kv_refs, o_ref = rest[:-1], rest[-1]
    tq = ATT_TQ
    n_pair = SWA_KVH // 2
    per_kv = SWA_HPG * SWA_DH // LANES
    scale = 1.0 / math.sqrt(SWA_DH)
    c_exp = LOG2E * scale
    t0 = pl.program_id(1) * tq
    ones_k = jnp.ones((2 * tq, LANES), BF16)

    def window(kind, var, pair):
        base = (kind * 4 + var) * 2
        sl = slice(pair * LANES, (pair + 1) * LANES)
        return jnp.concatenate([kv_refs[base][:, sl], kv_refs[base + 1][:, sl]], axis=0)

    t_col = t0 + lax.broadcasted_iota(jnp.int32, (tq, 1), 0)
    kpos = t0 - tq + lax.broadcasted_iota(jnp.int32, (1, 2 * tq), 1)
    ok = (kpos <= t_col) & (kpos > t_col - SWA_WINDOW) & (kpos >= 0)
    bias = jnp.where(ok, 0.0, NEG_BIG)
    col0 = lax.broadcasted_iota(jnp.int32, (tq, LANES), 1) == 0
    keep = jnp.where(lax.broadcasted_iota(jnp.int32, (2 * tq, LANES), 0) == 0, 0.0, 1.0).astype(BF16)

    for pair in range(n_pair):
        for j in range(2):
            k_var = [window(0, 2 * j + par, pair) * keep for par in range(2)]
            v_var = [jnp.concatenate([window(1, 2 * j + par, pair) * keep, ones_k], axis=1) for par in range(2)]
            for c in range(per_kv):
                col = ((pair * 2 + j) * per_kv + c) * LANES
                qc = q_ref[:, col:col + LANES]
                o_chunk = None
                for par in range(2):
                    head = (pair * 2 + j) * SWA_HPG + 2 * c + par
                    sink_raw = sink_ref[head] * math.sqrt(SWA_DH)
                    bias_h = jnp.concatenate([jnp.where(col0, sink_raw, bias[:, 0:LANES]), bias[:, LANES:]], axis=1)
                    sc = _dot_nt(qc, k_var[par]) + bias_h
                    m = jnp.max(sc, axis=-1, keepdims=True)
                    e = jnp.exp2((sc - m) * c_exp)
                    acc = _dot(e.astype(BF16), v_var[par])
                    o_par = acc[:, 0:LANES] * (1.0 / acc[:, LANES:2 * LANES])
                    o_chunk = o_par if o_chunk is None else o_chunk + o_par
                o_ref[:, col:col + LANES] = (o_chunk * z_ref[:, col:col + LANES]).astype(BF16)


def swa_attention(q, zs, kvb, sinks, batch, seq):
    t = q.shape[0]
    tq = ATT_TQ
    qt_per_seq = seq // tq
    kvw = SWA_KVH * SWA_DH

    def row(b, i):
        return b * qt_per_seq + i

    def prev_row(b, i):
        return b * qt_per_seq + jnp.maximum(i - 1, 0)

    qz_spec = pl.BlockSpec((tq, SWA_WIDTH), lambda b, i: (row(b, i), 0))
    kv_specs = []
    for cb in range(8):
        kv_specs.append(pl.BlockSpec((tq, kvw), functools.partial(lambda b, i, cb: (prev_row(b, i), cb), cb=cb)))
        kv_specs.append(pl.BlockSpec((tq, kvw), functools.partial(lambda b, i, cb: (row(b, i), cb), cb=cb)))
    return pl.pallas_call(
        _swa_kernel,
        grid=(batch, qt_per_seq),
        in_specs=[pl.BlockSpec(memory_space=pltpu.SMEM), qz_spec, qz_spec] + kv_specs,
        out_specs=qz_spec,
        out_shape=jax.ShapeDtypeStruct((t, SWA_WIDTH), BF16),
        compiler_params=_cparams(("arbitrary", "arbitrary")),
        name="swa_attention",
    )(sinks, q, zs, *([kvb] * 16))


def _selection_constants(seq):
    nc = seq // CMP_STRIDE
    ns = seq // SEL_BLOCK
    nsp = -(-ns // LANES) * LANES
    e_t = np.zeros((seq, nsp), np.float32)
    e_t[np.arange(seq), np.arange(seq) // SEL_BLOCK] = NEG_BIG
    c0 = np.arange(nc - 1) * CMP_STRIDE
    s0 = np.arange(ns) * SEL_BLOCK
    overlap = (c0[:, None] < s0[None, :] + SEL_BLOCK) & (c0[:, None] + CMP_LEN > s0[None, :])
    assert ns < nsp
    cmp_to_sel_t = np.zeros((nsp, nc), np.float32)
    cmp_to_sel_t[:ns, :nc - 1] = overlap.T
    cmp_to_sel_t[nsp - 1, :] = 1.0
    return jnp.asarray(e_t, BF16), jnp.asarray(cmp_to_sel_t, BF16)


def kernel(x, p, a_norm, a_w_in, a_w_out, a_cmp_pos_k, a_cmp_w1_k, a_cmp_w2_k, a_cmp_pos_v, a_cmp_w1_v,
           a_cmp_w2_v, kv_norm, w_kv, b_norm, b_w_in, b_w_out, b_sinks, ple_norm, ple_gate_w, ple_proj,
           final_norm):
    batch, seq, d = x.shape
    assert d == D_MODEL and p.shape[0] == 2 and a_w_in.shape[0] == 1 and b_w_in.shape[0] == 1
    t = batch * seq
    xf = x.reshape(t, d)

    n_gate = 3 * NSA_G * NSA_HPG
    w_q, w_cmp, w_kvs, w_gate, *w_z = split_a_weights(a_w_in)
    wg = w_gate[:, :n_gate].reshape(d, 3, NSA_G, NSA_HPG).transpose(0, 2, 1, 3)
    wg = wg.reshape(d, NSA_G, 3 * NSA_HPG)
    wg = jnp.pad(wg, ((0, 0), (0, 0), (0, LANES - 3 * NSA_HPG))).reshape(d, NSA_G * LANES)
    pos_kv = jnp.stack([a_cmp_pos_k[0], a_cmp_pos_v[0]])
    w1_kv = jnp.stack([a_cmp_w1_k[0], a_cmp_w1_v[0]]).astype(BF16)
    w2_kv = jnp.stack([a_cmp_w2_k[0], a_cmp_w2_v[0]]).astype(BF16)
    wb = b_w_in[0]
    tabs_a = _rope_tables(seq, NSA_DH)
    tabs_b = _rope_tables(seq, SWA_DH)
    e_t, cmp_to_sel_t = _selection_constants(seq)

    (n_a,) = rms_norm_rows(xf, a_norm, BF16)
    q_rot, q_plain = matmul_rope(n_a, w_q, tabs_a, seq, NSA_DH, MM_TN, True, "a_q_proj")
    (kv_a,) = matmul_rope(n_a, w_kvs, tabs_a, seq, NSA_DH, NSA_KVW, False, "a_kv_proj")
    kcvc = matmul(n_a, w_cmp, "a_cmp_proj")
    gates = matmul(n_a, wg, "a_gate_proj", act="sigmoid")
    zg = [matmul_zgate(n_a, w_z[br], gates, br, "a_z_proj%d" % br) for br in range(3)]
    cmp_kv = compress_kv(kcvc, pos_kv, w1_kv, w2_kv, batch, seq)
    mixed = nsa_attention(q_plain, q_rot, zg, cmp_kv, kv_a, e_t, cmp_to_sel_t, batch, seq)
    h, (n_p,), ssq = matmul_residual(mixed, a_w_out[0].astype(BF16), xf, ple_norm[0:1], "a_out_proj")
    pf = p.reshape(2, t, PLE_DIM)
    h, (n_kv, n_b), ssq = ple_update(n_p, ssq, ple_gate_w[0].astype(BF16), pf[0], ple_proj[0].astype(BF16), h,
                                     jnp.stack([kv_norm, b_norm[0]]), "ple0")

    kvb = matmul_kvb(n_kv, w_kv.astype(BF16), ssq, tabs_b, seq, "kv_proj")
    (q_b,) = matmul_rope(n_b, wb[:, :SWA_WIDTH].astype(BF16), tabs_b, seq, SWA_DH, MM_TN, False, "b_q_proj",
                         ssq=ssq)
    zs_b = matmul(n_b, wb[:, SWA_WIDTH:].astype(BF16), "b_z_proj", act="silu", ssq=ssq)
    ob = swa_attention(q_b, zs_b, kvb, b_sinks[0], batch, seq)
    h, (n_p,), ssq = matmul_residual(ob, b_w_out[0].astype(BF16), h, ple_norm[1:2], "b_out_proj")
    h = ple_update(n_p, ssq, ple_gate_w[1].astype(BF16), pf[1], ple_proj[1].astype(BF16), h, None, "ple1")

    (out,) = rms_norm_rows(h, final_norm[None, :], F32)
    return out.reshape(batch, seq, d)
```

```python
import functools
import math

import numpy as np
import jax
import jax.numpy as jnp
from jax import lax
from jax.experimental import pallas as pl
from jax.experimental.pallas import tpu as pltpu

F32 = jnp.float32
BF16 = jnp.bfloat16

D_MODEL = 4096
RMS_EPS = 1e-6
ROPE_THETA = 500000.0
NEG_BIG = -1e30
FORCE_SCORE = 1e9
PLE_DIM = 256

NSA_DH = 128
NSA_G = 4
NSA_HPG = 8
NSA_WIDTH = 4096
NSA_KVW = 512
CMP_LEN = 32
CMP_STRIDE = 16
CMP_HIDDEN = 256
SEL_BLOCK = 64
SEL_TOPK = 16
SEL_LOCAL = 2
NSA_WINDOW = 512

SWA_DH = 64
SWA_KVH = 8
SWA_HPG = 8
SWA_WIDTH = 4096
SWA_WINDOW = 128

LANES = 128
VMEM_LIMIT_BYTES = 58 * 1024 * 1024

MM_TM = 1024
MM_TN = 1024
MM_TN_FUSED = 512
MM_ROW_CHUNKS = 8
NORM_ROWS = 256
ATT_TQ = 128
NSA_TQ = 256
NSA_TK = 512

LOG2E = math.log2(math.e)


def _cparams(sem):
    return pltpu.CompilerParams(dimension_semantics=sem, vmem_limit_bytes=VMEM_LIMIT_BYTES)


def _sigmoid(v):
    return 1.0 / (1.0 + jnp.exp(-v))


def _silu(v):
    return v * _sigmoid(v)


def _dot(a, b):
    return jnp.dot(a, b, preferred_element_type=F32)


def _dot_nt(a, b):
    return lax.dot_general(a, b, (((1,), (1,)), ((), ())), preferred_element_type=F32)


def _lane_tile(x, n):
    return jnp.concatenate([x] * n, axis=1)


def _norm_kernel(h_ref, g_ref, *o_refs):
    x = h_ref[...]
    r = lax.rsqrt(jnp.mean(x * x, axis=-1, keepdims=True) + RMS_EPS)
    y = x * r
    for i, o_ref in enumerate(o_refs):
        o_ref[...] = (y * g_ref[i:i + 1, :]).astype(o_ref.dtype)


def rms_norm_rows(h, gains, out_dtype):
    t, d = h.shape
    k = gains.shape[0]
    outs = pl.pallas_call(
        _norm_kernel,
        grid=(t // NORM_ROWS,),
        in_specs=[pl.BlockSpec((NORM_ROWS, d), lambda i: (i, 0)),
                  pl.BlockSpec((k, d), lambda i: (0, 0))],
        out_specs=[pl.BlockSpec((NORM_ROWS, d), lambda i: (i, 0))] * k,
        out_shape=[jax.ShapeDtypeStruct((t, d), out_dtype)] * k,
        compiler_params=_cparams(("arbitrary",)),
        name="rms_norm",
    )(h, gains)
    return outs


def _rope_tables(seq, head_dim):
    rot = head_dim // 4
    half = rot // 2
    inv_freq = ROPE_THETA ** (-jnp.arange(half, dtype=F32) / half)
    ang = jnp.arange(seq, dtype=jnp.int32).astype(F32)[:, None] * inv_freq[None, :]
    cos, sin = jnp.cos(ang), jnp.sin(ang)
    rest = head_dim - rot
    zeros_h = jnp.zeros((seq, half), F32)
    c = jnp.concatenate([cos, cos, jnp.ones((seq, rest), F32)], axis=1)
    s1 = jnp.concatenate([-sin, zeros_h, jnp.zeros((seq, rest), F32)], axis=1)
    s2 = jnp.concatenate([zeros_h, sin, jnp.zeros((seq, rest), F32)], axis=1)
    reps = LANES // head_dim
    return tuple(jnp.tile(a, (1, reps)) for a in (c, s1, s2))


def _rope_lanes(t, c, s1, s2, half):
    return t * c + pltpu.roll(t, LANES - half, 1) * s1 + pltpu.roll(t, half, 1) * s2


def _row_chunks(ref, n=MM_ROW_CHUNKS):
    rows = ref.shape[0] // n
    return [slice(c * rows, (c + 1) * rows) for c in range(n)]


def _scaled_dot(x_ref, w_ref, s_ref, rs):
    acc = _dot(x_ref[rs, :], w_ref[...])
    if s_ref is not None:
        r = lax.rsqrt(s_ref[rs, :] * (1.0 / x_ref.shape[1]) + RMS_EPS)
        acc = acc * _lane_tile(r, acc.shape[1] // LANES)
    return acc


def _write_scaled_copies(h, rs, g_ref, hb_refs):
    for i, hb_ref in enumerate(hb_refs):
        hb_ref[rs, :] = (h * g_ref[i:i + 1, :]).astype(BF16)
    return jnp.broadcast_to(jnp.sum(h * h, axis=-1, keepdims=True), (h.shape[0], LANES))


def _accumulate_ssq(parts, ssq_ref):
    part = jnp.concatenate(parts, axis=0)
    j = pl.program_id(1)

    @pl.when(j == 0)
    def _():
        ssq_ref[...] = part

    @pl.when(j > 0)
    def _():
        ssq_ref[...] = ssq_ref[...] + part


def _mm_kernel(*refs, act, scaled):
    x_ref, w_ref = refs[0], refs[1]
    s_ref = refs[2] if scaled else None
    o_ref = refs[-1]
    for rs in _row_chunks(o_ref):
        acc = _scaled_dot(x_ref, w_ref, s_ref, rs)
        if act == "silu":
            acc = _silu(acc)
        elif act == "sigmoid":
            acc = _sigmoid(acc)
        o_ref[rs, :] = acc.astype(o_ref.dtype)


def _mm_zgate_kernel(x_ref, w_ref, g_ref, o_ref, *, first_col):
    for rs in _row_chunks(o_ref):
        acc = _silu(_dot(x_ref[rs, :], w_ref[...]))
        for h in range(acc.shape[1] // LANES):
            sl = slice(h * LANES, (h + 1) * LANES)
            o_ref[rs, sl] = acc[:, sl] * g_ref[rs, first_col + h:first_col + h + 1]


def _mm_res_kernel(x_ref, w_ref, r_ref, g_ref, h_ref, *rest):
    parts = []
    for rs in _row_chunks(h_ref, 1):
        h = r_ref[rs, :] + _dot(x_ref[rs, :], w_ref[...])
        h_ref[rs, :] = h
        parts.append(_write_scaled_copies(h, rs, g_ref, rest[:-1]))
    _accumulate_ssq(parts, rest[-1])


def _ple_kernel(n_ref, wg_ref, s_ref, p_ref, wp_ref, h_ref, *rest, n_gain):
    o_ref = rest[1] if n_gain else rest[0]
    parts = []
    for rs in _row_chunks(o_ref):
        gate = _sigmoid(_scaled_dot(n_ref, wg_ref, s_ref, rs))
        emb = _dot(p_ref[rs, :].astype(BF16), wp_ref[...])
        h = h_ref[rs, :] + gate * emb
        o_ref[rs, :] = h
        if n_gain:
            parts.append(_write_scaled_copies(h, rs, rest[0], rest[2:-1]))
    if n_gain:
        _accumulate_ssq(parts, rest[-1])


def _mm_rope_kernel(*refs, half, n_rope, plain_too, scaled):
    x_ref, w_ref = refs[0], refs[1]
    s_ref = refs[2] if scaled else None
    c_ref, s1_ref, s2_ref = refs[3:6] if scaled else refs[2:5]
    o_refs = refs[6:] if scaled else refs[5:]
    for rs in _row_chunks(o_refs[0]):
        acc = _scaled_dot(x_ref, w_ref, s_ref, rs)
        c, s1, s2 = c_ref[rs, :], s1_ref[rs, :], s2_ref[rs, :]
        for ch in range(acc.shape[1] // LANES):
            sl = slice(ch * LANES, (ch + 1) * LANES)
            t = acc[:, sl]
            o_refs[0][rs, sl] = (_rope_lanes(t, c, s1, s2, half) if ch < n_rope else t).astype(BF16)
            if plain_too:
                o_refs[1][rs, sl] = t.astype(BF16)


def _mm_kvb_kernel(x_ref, w_ref, s_ref, c_ref, s1_ref, s2_ref, o_ref, *, half):
    kvw = SWA_KVH * SWA_DH
    n_ch = kvw // LANES
    for rs in _row_chunks(o_ref):
        acc = _scaled_dot(x_ref, w_ref, s_ref, rs)
        c, s1, s2 = c_ref[rs, :], s1_ref[rs, :], s2_ref[rs, :]
        low = lax.broadcasted_iota(jnp.int32, (acc.shape[0], LANES), 1) < SWA_DH
        for kind in range(2):
            for ch in range(n_ch):
                t = acc[:, kind * kvw + ch * LANES: kind * kvw + (ch + 1) * LANES]
                if kind == 0:
                    t = _rope_lanes(t, c, s1, s2, half)
                ts = pltpu.roll(t, SWA_DH, 1)
                variants = (jnp.where(low, t, 0.0), jnp.where(low, 0.0, ts),
                            jnp.where(low, ts, 0.0), jnp.where(low, 0.0, t))
                for var, val in enumerate(variants):
                    col = (kind * 4 + var) * kvw + ch * LANES
                    o_ref[rs, col:col + LANES] = val.astype(BF16)


def _mm_tiles(m, n, tn_max=MM_TN):
    tm = min(MM_TM, m)
    tn = min(tn_max, n)
    assert m % tm == 0 and n % tn == 0, (m, n)
    return tm, tn


def _ssq_spec(tm):
    return pl.BlockSpec((tm, LANES), lambda i, j: (i, 0))


def _w_cols(w, cols, m):
    first, n = (0, w.shape[1]) if cols is None else cols
    tm, tn = _mm_tiles(m, n)
    assert first % tn == 0
    return first // tn, n, tm, tn


def matmul(x, w, name, act=None, out_dtype=F32, ssq=None, cols=None):
    m, k = x.shape
    jb, n, tm, tn = _w_cols(w, cols, m)
    scaled = ssq is not None
    return pl.pallas_call(
        functools.partial(_mm_kernel, act=act, scaled=scaled),
        grid=(m // tm, n // tn),
        in_specs=[pl.BlockSpec((tm, k), lambda i, j: (i, 0)),
                  pl.BlockSpec((k, tn), lambda i, j: (0, jb + j))] + ([_ssq_spec(tm)] if scaled else []),
        out_specs=pl.BlockSpec((tm, tn), lambda i, j: (i, j)),
        out_shape=jax.ShapeDtypeStruct((m, n), out_dtype),
        compiler_params=_cparams(("arbitrary", "arbitrary")),
        name=name,
    )(x, w, *([ssq] if scaled else []))


def matmul_zgate(x, w, gates, branch, name):
    m, k = x.shape
    n = w.shape[1]
    tm, tn = _mm_tiles(m, n)
    assert tn == NSA_HPG * NSA_DH
    return pl.pallas_call(
        functools.partial(_mm_zgate_kernel, first_col=branch * NSA_HPG),
        grid=(m // tm, n // tn),
        in_specs=[pl.BlockSpec((tm, k), lambda i, j: (i, 0)),
                  pl.BlockSpec((k, tn), lambda i, j: (0, j)),
                  pl.BlockSpec((tm, LANES), lambda i, j: (i, j))],
        out_specs=pl.BlockSpec((tm, tn), lambda i, j: (i, j)),
        out_shape=jax.ShapeDtypeStruct((m, n), F32),
        compiler_params=_cparams(("arbitrary", "arbitrary")),
        name=name,
    )(x, w, gates)


def matmul_rope(x, w, tabs, seq, head_dim, rope_cols, plain_too, name, ssq=None, cols=None):
    m, k = x.shape
    jb, n, tm, tn = _w_cols(w, cols, m)
    assert seq % tm == 0
    per_seq = seq // tm
    scaled = ssq is not None
    tab_spec = pl.BlockSpec((tm, LANES), lambda i, j: (i % per_seq, 0))
    n_out = 2 if plain_too else 1
    outs = pl.pallas_call(
        functools.partial(_mm_rope_kernel, half=head_dim // 8, n_rope=rope_cols // LANES,
                          plain_too=plain_too, scaled=scaled),
        grid=(m // tm, n // tn),
        in_specs=[pl.BlockSpec((tm, k), lambda i, j: (i, 0)),
                  pl.BlockSpec((k, tn), lambda i, j: (0, jb + j))]
        + ([_ssq_spec(tm)] if scaled else []) + [tab_spec] * 3,
        out_specs=[pl.BlockSpec((tm, tn), lambda i, j: (i, j))] * n_out,
        out_shape=[jax.ShapeDtypeStruct((m, n), BF16)] * n_out,
        compiler_params=_cparams(("arbitrary", "arbitrary")),
        name=name,
    )(x, w, *([ssq] if scaled else []), *tabs)
    return outs


def matmul_kvb(x, w, ssq, tabs, seq, name):
    m, k = x.shape
    n = w.shape[1]
    tm = min(MM_TM, m)
    assert seq % tm == 0 and n == 2 * SWA_KVH * SWA_DH
    per_seq = seq // tm
    tab_spec = pl.BlockSpec((tm, LANES), lambda i: (i % per_seq, 0))
    return pl.pallas_call(
        functools.partial(_mm_kvb_kernel, half=SWA_DH // 8),
        grid=(m // tm,),
        in_specs=[pl.BlockSpec((tm, k), lambda i: (i, 0)),
                  pl.BlockSpec((k, n), lambda i: (0, 0)),
                  pl.BlockSpec((tm, LANES), lambda i: (i, 0))] + [tab_spec] * 3,
        out_specs=pl.BlockSpec((tm, 4 * n), lambda i: (i, 0)),
        out_shape=jax.ShapeDtypeStruct((m, 4 * n), BF16),
        compiler_params=_cparams(("arbitrary",)),
        name=name,
    )(x, w, ssq, *tabs)


def matmul_residual(x, w, res, gains, name):
    m, k = x.shape
    n = w.shape[1]
    kg = gains.shape[0]
    tm, tn = _mm_tiles(m, n, MM_TN_FUSED)
    tile = pl.BlockSpec((tm, tn), lambda i, j: (i, j))
    outs = pl.pallas_call(
        _mm_res_kernel,
        grid=(m // tm, n // tn),
        in_specs=[pl.BlockSpec((tm, k), lambda i, j: (i, 0)),
                  pl.BlockSpec((k, tn), lambda i, j: (0, j)),
                  tile,
                  pl.BlockSpec((kg, tn), lambda i, j: (0, j))],
        out_specs=[tile] + [tile] * kg + [_ssq_spec(tm)],
        out_shape=[jax.ShapeDtypeStruct((m, n), F32)] + [jax.ShapeDtypeStruct((m, n), BF16)] * kg
        + [jax.ShapeDtypeStruct((m, LANES), F32)],
        compiler_params=_cparams(("arbitrary", "arbitrary")),
        name=name,
    )(x, w, res, gains)
    return outs[0], outs[1:-1], outs[-1]


def ple_update(n, ssq, wg, p, wp, h, gains, name):
    m, k = n.shape
    d = wg.shape[1]
    kp = p.shape[1]
    kg = 0 if gains is None else gains.shape[0]
    tm, tn = _mm_tiles(m, d, MM_TN_FUSED)
    tile = pl.BlockSpec((tm, tn), lambda i, j: (i, j))
    in_specs = [pl.BlockSpec((tm, k), lambda i, j: (i, 0)),
                pl.BlockSpec((k, tn), lambda i, j: (0, j)),
                _ssq_spec(tm),
                pl.BlockSpec((tm, kp), lambda i, j: (i, 0)),
                pl.BlockSpec((kp, tn), lambda i, j: (0, j)),
                tile]
    args = [n, wg, ssq, p, wp, h]
    out_specs = [tile]
    out_shape = [jax.ShapeDtypeStruct((m, d), F32)]
    if kg:
        in_specs.append(pl.BlockSpec((kg, tn), lambda i, j: (0, j)))
        args.append(gains)
        out_specs += [tile] * kg + [_ssq_spec(tm)]
        out_shape += [jax.ShapeDtypeStruct((m, d), BF16)] * kg + [jax.ShapeDtypeStruct((m, LANES), F32)]
    outs = pl.pallas_call(
        functools.partial(_ple_kernel, n_gain=kg),
        grid=(m // tm, d // tn),
        in_specs=in_specs,
        out_specs=out_specs,
        out_shape=out_shape,
        compiler_params=_cparams(("arbitrary", "arbitrary")),
        name=name,
    )(*args)
    if kg:
        return outs[0], outs[1:-1], outs[-1]
    return outs[0]


A_W_COLS = 512


def _wt_cast_kernel(wt_ref, o_ref):
    o_ref[...] = wt_ref[...].T.astype(BF16)


def _cast_weight_cols(wt, first, width, name):
    k = wt.shape[1]
    cols = min(A_W_COLS, width)
    assert width % cols == 0
    return pl.pallas_call(
        _wt_cast_kernel,
        grid=(width // cols,),
        in_specs=[pl.BlockSpec((pl.Element(cols), pl.Element(k)), lambda i: (pl.multiple_of(first + i * cols, 8), 0))],
        out_specs=pl.BlockSpec((k, cols), lambda i: (0, i)),
        out_shape=jax.ShapeDtypeStruct((k, width), BF16),
        compiler_params=_cparams(("arbitrary",)),
        name=name,
    )(wt)


def split_a_weights(wa):
    wt = wa[0].T
    c_q, c_cmp, c_kv = NSA_WIDTH, NSA_WIDTH + 2 * NSA_KVW, NSA_WIDTH + 6 * NSA_KVW
    c_z = c_kv + 3 * NSA_G * NSA_HPG
    segs = [(0, c_q), (c_q, c_cmp - c_q), (c_cmp, c_kv - c_cmp), (c_kv, LANES)]
    segs += [(c_z + br * NSA_WIDTH, NSA_WIDTH) for br in range(3)]
    return [_cast_weight_cols(wt, first, width, "a_w_cast%d" % i) for i, (first, width) in enumerate(segs)]


def _compress_kernel(x_ref, pos_ref, w1_ref, w2_ref, o_ref, *, n_cmp_pad):
    nc = n_cmp_pad
    lo = jnp.zeros((nc, CMP_HIDDEN), F32)
    hi = jnp.zeros((nc, CMP_HIDDEN), F32)
    for l in range(CMP_STRIDE):
        xl = x_ref[pl.ds(l, nc, stride=CMP_STRIDE), :]
        a = (xl + pos_ref[0, l:l + 1, :]).astype(BF16)
        b = (xl + pos_ref[0, CMP_STRIDE + l:CMP_STRIDE + l + 1, :]).astype(BF16)
        lo = lo + _dot(a, w1_ref[0, l * LANES:(l + 1) * LANES, :])
        hi = hi + _dot(b, w1_ref[0, (CMP_STRIDE + l) * LANES:(CMP_STRIDE + l + 1) * LANES, :])
    hid = lo + pltpu.roll(hi, nc - 1, 0)
    out = _dot(_silu(hid).astype(BF16), w2_ref[0])
    n_idx = lax.broadcasted_iota(jnp.int32, (nc, LANES), 0)
    o_ref[0, 0, 0] = jnp.where(n_idx < nc - 1, out, 0.0).astype(BF16)


def compress_kv(kcvc, pos, w1, w2, batch, seq):
    nc = seq // CMP_STRIDE
    return pl.pallas_call(
        functools.partial(_compress_kernel, n_cmp_pad=nc),
        grid=(batch, 2, NSA_G),
        in_specs=[pl.BlockSpec((seq, LANES), lambda b, s, g: (b, s * NSA_G + g)),
                  pl.BlockSpec((1, CMP_LEN, LANES), lambda b, s, g: (s, 0, 0)),
                  pl.BlockSpec((1, CMP_LEN * LANES, CMP_HIDDEN), lambda b, s, g: (s, 0, 0)),
                  pl.BlockSpec((1, CMP_HIDDEN, LANES), lambda b, s, g: (s, 0, 0))],
        out_specs=pl.BlockSpec((1, 1, 1, nc, LANES), lambda b, s, g: (b, s, g, 0, 0)),
        out_shape=jax.ShapeDtypeStruct((batch, 2, NSA_G, nc, LANES), BF16),
        compiler_params=_cparams(("arbitrary", "arbitrary", "arbitrary")),
        name="compress_kv",
    )(kcvc, pos, w1, w2)


def _nsa_kernel(qp_ref, qr_ref, z0_ref, z1_ref, z2_ref,
                kc_ref, vc_ref, ks_ref, vs_ref, kw_ref, vw_ref, et_ref, mt_ref,
                o_ref, qa_scr, m_scr, acc_scr, mix_scr, *, seq):
    tq, tk, hpg = NSA_TQ, NSA_TK, NSA_HPG
    rows = hpg * tq
    nc = seq // CMP_STRIDE
    ns = seq // SEL_BLOCK
    n_top = min(SEL_TOPK, ns)
    nsp = mt_ref.shape[0]
    c_exp = LOG2E / math.sqrt(NSA_DH)
    t0 = pl.program_id(2) * tq
    ones_k = jnp.ones((tk, LANES), BF16)

    t_col = t0 + lax.broadcasted_iota(jnp.int32, (tq, 1), 0)

    n_idx = lax.broadcasted_iota(jnp.int32, (1, nc), 1)
    valid = (n_idx * CMP_STRIDE + (CMP_LEN - 1) <= t_col) & (n_idx < nc - 1)
    bias_c = jnp.where(valid, 0.0, NEG_BIG)
    row_ok = t0 + lax.broadcasted_iota(jnp.int32, (tq, LANES), 0) >= CMP_LEN - 1
    lane_ok = t0 + lax.broadcasted_iota(jnp.int32, (ns, tq), 1) >= CMP_LEN - 1
    kc = kc_ref[0, 0, 0]
    vc_aug = jnp.concatenate([vc_ref[0, 0, 0], jnp.ones((nc, LANES), BF16)], axis=1)
    hq = tq // 2
    span = NSA_WINDOW + hq
    win = []
    for hh in range(2):
        kw0 = pl.multiple_of(jnp.maximum(t0 + hh * hq - NSA_WINDOW, 0), hq)
        t_half = t_col[hh * hq:(hh + 1) * hq]
        kpos_w = kw0 + lax.broadcasted_iota(jnp.int32, (1, span), 1)
        win.append((kw_ref[pl.ds(kw0, span), :],
                    jnp.concatenate([vw_ref[pl.ds(kw0, span), :], jnp.ones((span, LANES), BF16)], axis=1),
                    jnp.where((kpos_w <= t_half) & (kpos_w > t_half - NSA_WINDOW), 0.0, NEG_BIG)))
    q_all = jnp.concatenate([qp_ref[:, h * LANES:(h + 1) * LANES] for h in range(hpg)], axis=0)
    sc = _dot_nt(q_all, kc) + jnp.concatenate([bias_c] * hpg, axis=0)
    m = jnp.max(sc, axis=-1, keepdims=True)
    pe = jnp.exp2((sc - m) * c_exp).astype(BF16)
    acc_c = _dot(pe, vc_aug)
    imp_t = _dot_nt(mt_ref[...], pe)
    imp_parts = []
    for h in range(hpg):
        sl = slice(h * LANES, (h + 1) * LANES)
        rs = slice(h * tq, (h + 1) * tq)
        oc = jnp.where(row_ok, acc_c[rs, 0:LANES] * (1.0 / acc_c[rs, LANES:2 * LANES]), 0.0)
        imp_parts.append(imp_t[0:ns, rs] * (1.0 / imp_t[nsp - 1:nsp, rs]))
        ows = []
        for hh, (kwin, vw_aug, bias_w) in enumerate(win):
            sc = _dot_nt(qr_ref[hh * hq:(hh + 1) * hq, sl], kwin) + bias_w
            m = jnp.max(sc, axis=-1, keepdims=True)
            pe_w = jnp.exp2((sc - m) * c_exp).astype(BF16)
            acc = _dot(pe_w, vw_aug)
            ows.append(acc[:, 0:LANES] * (1.0 / acc[:, LANES:2 * LANES]))
        ow = jnp.concatenate(ows, axis=0)
        mix_scr[:, sl] = oc * z0_ref[:, sl] + ow * z2_ref[:, sl]
    while len(imp_parts) > 1:
        imp_parts = [imp_parts[i] + imp_parts[i + 1] for i in range(0, len(imp_parts), 2)]
    imp = jnp.where(lane_ok, imp_parts[0], 0.0)

    j_io = lax.broadcasted_iota(jnp.int32, (ns, tq), 0)
    t_io = t0 + lax.broadcasted_iota(jnp.int32, (ns, tq), 1)
    dist = lax.shift_right_logical(t_io, int(math.log2(SEL_BLOCK))) - j_io
    forced = (j_io == 0) | ((dist >= 0) & (dist < SEL_LOCAL))
    imp = jnp.where(forced, FORCE_SCORE, imp)
    imp = jnp.where(dist >= 0, imp, -1.0)
    sub = 8
    ranks = []
    for v in range(ns // sub):
        blk = imp[v * sub:(v + 1) * sub, :]
        j_v = v * sub + lax.broadcasted_iota(jnp.int32, (sub, tq), 0)
        r = jnp.zeros((sub, tq), F32)
        for k in range(ns):
            rk = imp[k:k + 1, :]
            if k < v * sub:
                ahead = rk >= blk
            elif k >= (v + 1) * sub:
                ahead = rk > blk
            else:
                ahead = (rk > blk) | ((rk == blk) & (j_v > k))
            r = r + jnp.where(ahead, 1.0, 0.0)
        ranks.append(r)
    rank = jnp.concatenate(ranks, axis=0)
    not_sel_t = jnp.concatenate([jnp.where(rank < n_top, 0.0, 1.0), jnp.ones((nsp - ns, tq), F32)], axis=0)
    not_sel = not_sel_t.T.astype(BF16)

    for h in range(hpg):
        qa_scr[h * tq:(h + 1) * tq, 0:LANES] = qr_ref[:, h * LANES:(h + 1) * LANES]
        qa_scr[h * tq:(h + 1) * tq, LANES:2 * LANES] = not_sel

    m_scr[...] = jnp.full((rows, LANES), NEG_BIG, F32)
    acc_scr[...] = jnp.zeros((rows, 2 * LANES), F32)
    n_kt = lax.shift_right_logical(t0 + (tq + tk - 1), int(math.log2(tk)))

    def slc_tile(kt, causal, width=tk):
        k0 = pl.multiple_of(kt * tk, tk)
        k_aug = jnp.concatenate([ks_ref[pl.ds(k0, width), :], et_ref[pl.ds(k0, width), :]], axis=1)
        v_aug = jnp.concatenate([vs_ref[pl.ds(k0, width), :], ones_k[0:width]], axis=1)
        if causal:
            kpos = k0 + lax.broadcasted_iota(jnp.int32, (1, width), 1)
            bias = jnp.where(kpos <= t_col, 0.0, NEG_BIG)
        for h in range(hpg):
            rs = slice(h * tq, (h + 1) * tq)
            sc = _dot_nt(qa_scr[rs, :], k_aug)
            if causal:
                sc = sc + bias
            m_old = m_scr[rs, :]
            m_new = jnp.maximum(m_old, jnp.max(sc, axis=-1, keepdims=True))
            alpha = jnp.exp2((m_old - m_new) * c_exp)
            pe = jnp.exp2((sc - _lane_tile(m_new, width // LANES)) * c_exp)
            acc_scr[rs, :] = _lane_tile(alpha, 2) * acc_scr[rs, :] + _dot(pe.astype(BF16), v_aug)
            m_scr[rs, :] = m_new

    def pair_body(i, carry):
        slc_tile(2 * i, causal=False)
        slc_tile(2 * i + 1, causal=False)
        return carry

    lax.fori_loop(0, lax.shift_right_logical(n_kt - 1, 1), pair_body, 0)
    odd = ((n_kt - 1) & 1) == 1
    short = (t0 & (tk - 1)) == 0
    for odd_case in (True, False):
        for short_case in (True, False):
            @pl.when((odd if odd_case else jnp.logical_not(odd)) & (short if short_case else jnp.logical_not(short)))
            def _(odd_case=odd_case, short_case=short_case):
                if odd_case:
                    slc_tile(n_kt - 2, causal=False)
                slc_tile(n_kt - 1, causal=True, width=tq if short_case else tk)

    for h in range(hpg):
        rs = slice(h * tq, (h + 1) * tq)
        sl = slice(h * LANES, (h + 1) * LANES)
        o_slc = acc_scr[rs, 0:LANES] * (1.0 / acc_scr[rs, LANES:2 * LANES])
        o_ref[:, sl] = (mix_scr[:, sl] + o_slc * z1_ref[:, sl]).astype(BF16)


def nsa_attention(qp, qr, zg, cmp_kv, kv, e_t, cmp_to_sel_t, batch, seq):
    t = qp.shape[0]
    tq = NSA_TQ
    assert seq % NSA_TK == 0 and NSA_TK % tq == 0 and seq >= NSA_WINDOW + tq
    qt_per_seq = seq // tq
    nc = seq // CMP_STRIDE
    hw = NSA_HPG * NSA_DH

    qz_spec = pl.BlockSpec((tq, hw), lambda b, g, i: (b * qt_per_seq + i, g))
    cmp_specs = [pl.BlockSpec((1, 1, 1, nc, LANES), functools.partial(
        lambda b, g, i, s: (b, s, g, 0, 0), s=s)) for s in range(2)]
    kv_specs = [pl.BlockSpec((seq, LANES), functools.partial(
        lambda b, g, i, s: (b, s * NSA_G + g), s=s)) for s in range(4)]
    const_specs = [pl.BlockSpec(e_t.shape, lambda b, g, i: (0, 0)),
                   pl.BlockSpec(cmp_to_sel_t.shape, lambda b, g, i: (0, 0))]
    rows = NSA_HPG * tq
    return pl.pallas_call(
        functools.partial(_nsa_kernel, seq=seq),
        grid=(batch, NSA_G, qt_per_seq),
        in_specs=[qz_spec] * 5 + cmp_specs + kv_specs + const_specs,
        out_specs=qz_spec,
        out_shape=jax.ShapeDtypeStruct((t, NSA_WIDTH), BF16),
        scratch_shapes=[pltpu.VMEM((rows, 2 * LANES), BF16),
                        pltpu.VMEM((rows, LANES), F32),
                        pltpu.VMEM((rows, 2 * LANES), F32),
                        pltpu.VMEM((tq, hw), F32)],
        compiler_params=_cparams(("arbitrary", "arbitrary", "arbitrary")),
        name="nsa_attention",
    )(qp, qr, *zg, cmp_kv, cmp_kv, kv, kv, kv, kv, e_t, cmp_to_sel_t)


def _swa_kernel(sink_ref, q_ref, z_ref, *rest):
    kv_refs, o_ref = rest[:-1], rest[-1]
    tq = ATT_TQ
    n_pair = SWA_KVH // 2
    per_kv = SWA_HPG * SWA_DH // LANES
    scale = 1.0 / math.sqrt(SWA_DH)
    c_exp = LOG2E * scale
    t0 = pl.program_id(1) * tq
    ones_k = jnp.ones((2 * tq, LANES), BF16)

    def window(kind, var, pair):
        base = (kind * 4 + var) * 2
        sl = slice(pair * LANES, (pair + 1) * LANES)
        return jnp.concatenate([kv_refs[base][:, sl], kv_refs[base + 1][:, sl]], axis=0)

    t_col = t0 + lax.broadcasted_iota(jnp.int32, (tq, 1), 0)
    kpos = t0 - tq + lax.broadcasted_iota(jnp.int32, (1, 2 * tq), 1)
    ok = (kpos <= t_col) & (kpos > t_col - SWA_WINDOW) & (kpos >= 0)
    bias = jnp.where(ok, 0.0, NEG_BIG)
    col0 = lax.broadcasted_iota(jnp.int32, (tq, LANES), 1) == 0
    keep = jnp.where(lax.broadcasted_iota(jnp.int32, (2 * tq, LANES), 0) == 0, 0.0, 1.0).astype(BF16)

    for pair in range(n_pair):
        for j in range(2):
            k_var = [window(0, 2 * j + par, pair) * keep for par in range(2)]
            v_var = [jnp.concatenate([window(1, 2 * j + par, pair) * keep, ones_k], axis=1) for par in range(2)]
            for c in range(per_kv):
                col = ((pair * 2 + j) * per_kv + c) * LANES
                qc = q_ref[:, col:col + LANES]
                o_chunk = None
                for par in range(2):
                    head = (pair * 2 + j) * SWA_HPG + 2 * c + par
                    sink_raw = sink_ref[head] * math.sqrt(SWA_DH)
                    bias_h = jnp.concatenate([jnp.where(col0, sink_raw, bias[:, 0:LANES]), bias[:, LANES:]], axis=1)
                    sc = _dot_nt(qc, k_var[par]) + bias_h
                    m = jnp.max(sc, axis=-1, keepdims=True)
                    e = jnp.exp2((sc - m) * c_exp)
                    acc = _dot(e.astype(BF16), v_var[par])
                    o_par = acc[:, 0:LANES] * (1.0 / acc[:, LANES:2 * LANES])
                    o_chunk = o_par if o_chunk is None else o_chunk + o_par
                o_ref[:, col:col + LANES] = (o_chunk * z_ref[:, col:col + LANES]).astype(BF16)


def swa_attention(q, zs, kvb, sinks, batch, seq):
    t = q.shape[0]
    tq = ATT_TQ
    qt_per_seq = seq // tq
    kvw = SWA_KVH * SWA_DH

    def row(b, i):
        return b * qt_per_seq + i

    def prev_row(b, i):
        return b * qt_per_seq + jnp.maximum(i - 1, 0)

    qz_spec = pl.BlockSpec((tq, SWA_WIDTH), lambda b, i: (row(b, i), 0))
    kv_specs = []
    for cb in range(8):
        kv_specs.append(pl.BlockSpec((tq, kvw), functools.partial(lambda b, i, cb: (prev_row(b, i), cb), cb=cb)))
        kv_specs.append(pl.BlockSpec((tq, kvw), functools.partial(lambda b, i, cb: (row(b, i), cb), cb=cb)))
    return pl.pallas_call(
        _swa_kernel,
        grid=(batch, qt_per_seq),
        in_specs=[pl.BlockSpec(memory_space=pltpu.SMEM), qz_spec, qz_spec] + kv_specs,
        out_specs=qz_spec,
        out_shape=jax.ShapeDtypeStruct((t, SWA_WIDTH), BF16),
        compiler_params=_cparams(("arbitrary", "arbitrary")),
        name="swa_attention",
    )(sinks, q, zs, *([kvb] * 16))


def _selection_constants(seq):
    nc = seq // CMP_STRIDE
    ns = seq // SEL_BLOCK
    nsp = -(-ns // LANES) * LANES
    e_t = np.zeros((seq, nsp), np.float32)
    e_t[np.arange(seq), np.arange(seq) // SEL_BLOCK] = NEG_BIG
    c0 = np.arange(nc - 1) * CMP_STRIDE
    s0 = np.arange(ns) * SEL_BLOCK
    overlap = (c0[:, None] < s0[None, :] + SEL_BLOCK) & (c0[:, None] + CMP_LEN > s0[None, :])
    assert ns < nsp
    cmp_to_sel_t = np.zeros((nsp, nc), np.float32)
    cmp_to_sel_t[:ns, :nc - 1] = overlap.T
    cmp_to_sel_t[nsp - 1, :] = 1.0
    return jnp.asarray(e_t, BF16), jnp.asarray(cmp_to_sel_t, BF16)


def kernel(x, p, a_norm, a_w_in, a_w_out, a_cmp_pos_k, a_cmp_w1_k, a_cmp_w2_k, a_cmp_pos_v, a_cmp_w1_v,
           a_cmp_w2_v, kv_norm, w_kv, b_norm, b_w_in, b_w_out, b_sinks, ple_norm, ple_gate_w, ple_proj,
           final_norm):
    batch, seq, d = x.shape
    assert d == D_MODEL and p.shape[0] == 2 and a_w_in.shape[0] == 1 and b_w_in.shape[0] == 1
    t = batch * seq
    xf = x.reshape(t, d)

    n_gate = 3 * NSA_G * NSA_HPG
    w_q, w_cmp, w_kvs, w_gate, *w_z = split_a_weights(a_w_in)
    wg = w_gate[:, :n_gate].reshape(d, 3, NSA_G, NSA_HPG).transpose(0, 2, 1, 3)
    wg = wg.reshape(d, NSA_G, 3 * NSA_HPG)
    wg = jnp.pad(wg, ((0, 0), (0, 0), (0, LANES - 3 * NSA_HPG))).reshape(d, NSA_G * LANES)
    pos_kv = jnp.stack([a_cmp_pos_k[0], a_cmp_pos_v[0]])
    w1_kv = jnp.stack([a_cmp_w1_k[0], a_cmp_w1_v[0]]).astype(BF16)
    w2_kv = jnp.stack([a_cmp_w2_k[0], a_cmp_w2_v[0]]).astype(BF16)
    wb = b_w_in[0].astype(BF16)
    tabs_a = _rope_tables(seq, NSA_DH)
    tabs_b = _rope_tables(seq, SWA_DH)
    e_t, cmp_to_sel_t = _selection_constants(seq)

    (n_a,) = rms_norm_rows(xf, a_norm, BF16)
    q_rot, q_plain = matmul_rope(n_a, w_q, tabs_a, seq, NSA_DH, MM_TN, True, "a_q_proj")
    (kv_a,) = matmul_rope(n_a, w_kvs, tabs_a, seq, NSA_DH, NSA_KVW, False, "a_kv_proj")
    kcvc = matmul(n_a, w_cmp, "a_cmp_proj")
    gates = matmul(n_a, wg, "a_gate_proj", act="sigmoid")
    zg = [matmul_zgate(n_a, w_z[br], gates, br, "a_z_proj%d" % br) for br in range(3)]
    cmp_kv = compress_kv(kcvc, pos_kv, w1_kv, w2_kv, batch, seq)
    mixed = nsa_attention(q_plain, q_rot, zg, cmp_kv, kv_a, e_t, cmp_to_sel_t, batch, seq)
    h, (n_p,), ssq = matmul_residual(mixed, a_w_out[0].astype(BF16), xf, ple_norm[0:1], "a_out_proj")
    pf = p.reshape(2, t, PLE_DIM)
    h, (n_kv, n_b), ssq = ple_update(n_p, ssq, ple_gate_w[0].astype(BF16), pf[0], ple_proj[0].astype(BF16), h,
                                     jnp.stack([kv_norm, b_norm[0]]), "ple0")

    kvb = matmul_kvb(n_kv, w_kv.astype(BF16), ssq, tabs_b, seq, "kv_proj")
    (q_b,) = matmul_rope(n_b, wb, tabs_b, seq, SWA_DH, MM_TN, False, "b_q_proj", ssq=ssq, cols=(0, SWA_WIDTH))
    zs_b = matmul(n_b, wb, "b_z_proj", act="silu", ssq=ssq, cols=(SWA_WIDTH, SWA_WIDTH))
    ob = swa_attention(q_b, zs_b, kvb, b_sinks[0], batch, seq)
    h, (n_p,), ssq = matmul_residual(ob, b_w_out[0].astype(BF16), h, ple_norm[1:2], "b_out_proj")
    h = ple_update(n_p, ssq, ple_gate_w[1].astype(BF16), pf[1], ple_proj[1].astype(BF16), h, None, "ple1")

    (out,) = rms_norm_rows(h, final_norm[None, :], F32)
    return out.reshape(batch, seq, d)
```

```python
import functools
import math

import numpy as np
import jax
import jax.numpy as jnp
from jax import lax
from jax.experimental import pallas as pl
from jax.experimental.pallas import tpu as pltpu

F32 = jnp.float32
BF16 = jnp.bfloat16

D_MODEL = 4096
RMS_EPS = 1e-6
ROPE_THETA = 500000.0
NEG_BIG = -1e30
FORCE_SCORE = 1e9
PLE_DIM = 256

NSA_DH = 128
NSA_G = 4
NSA_HPG = 8
NSA_WIDTH = 4096
NSA_KVW = 512
CMP_LEN = 32
CMP_STRIDE = 16
CMP_HIDDEN = 256
SEL_BLOCK = 64
SEL_TOPK = 16
SEL_LOCAL = 2
NSA_WINDOW = 512

SWA_DH = 64
SWA_KVH = 8
SWA_HPG = 8
SWA_WIDTH = 4096
SWA_WINDOW = 128

LANES = 128
VMEM_LIMIT_BYTES = 58 * 1024 * 1024

MM_TM = 1024
MM_TN = 1024
MM_TN_FUSED = 512
MM_ROW_CHUNKS = 8
NORM_ROWS = 256
ATT_TQ = 128
NSA_TQ = 256
NSA_TK = 512

LOG2E = math.log2(math.e)


def _cparams(sem):
    return pltpu.CompilerParams(dimension_semantics=sem, vmem_limit_bytes=VMEM_LIMIT_BYTES)


def _sigmoid(v):
    return 1.0 / (1.0 + jnp.exp(-v))


def _silu(v):
    return v * _sigmoid(v)


def _dot(a, b):
    return jnp.dot(a, b, preferred_element_type=F32)


def _dot_nt(a, b):
    return lax.dot_general(a, b, (((1,), (1,)), ((), ())), preferred_element_type=F32)


def _lane_tile(x, n):
    return jnp.concatenate([x] * n, axis=1)


def _norm_kernel(h_ref, g_ref, *o_refs):
    x = h_ref[...]
    r = lax.rsqrt(jnp.mean(x * x, axis=-1, keepdims=True) + RMS_EPS)
    y = x * r
    for i, o_ref in enumerate(o_refs):
        o_ref[...] = (y * g_ref[i:i + 1, :]).astype(o_ref.dtype)


def rms_norm_rows(h, gains, out_dtype):
    t, d = h.shape
    k = gains.shape[0]
    outs = pl.pallas_call(
        _norm_kernel,
        grid=(t // NORM_ROWS,),
        in_specs=[pl.BlockSpec((NORM_ROWS, d), lambda i: (i, 0)),
                  pl.BlockSpec((k, d), lambda i: (0, 0))],
        out_specs=[pl.BlockSpec((NORM_ROWS, d), lambda i: (i, 0))] * k,
        out_shape=[jax.ShapeDtypeStruct((t, d), out_dtype)] * k,
        compiler_params=_cparams(("arbitrary",)),
        name="rms_norm",
    )(h, gains)
    return outs


def _rope_tables(seq, head_dim):
    rot = head_dim // 4
    half = rot // 2
    inv_freq = ROPE_THETA ** (-jnp.arange(half, dtype=F32) / half)
    ang = jnp.arange(seq, dtype=jnp.int32).astype(F32)[:, None] * inv_freq[None, :]
    cos, sin = jnp.cos(ang), jnp.sin(ang)
    rest = head_dim - rot
    zeros_h = jnp.zeros((seq, half), F32)
    c = jnp.concatenate([cos, cos, jnp.ones((seq, rest), F32)], axis=1)
    s1 = jnp.concatenate([-sin, zeros_h, jnp.zeros((seq, rest), F32)], axis=1)
    s2 = jnp.concatenate([zeros_h, sin, jnp.zeros((seq, rest), F32)], axis=1)
    reps = LANES // head_dim
    return tuple(jnp.tile(a, (1, reps)) for a in (c, s1, s2))


def _rope_lanes(t, c, s1, s2, half):
    return t * c + pltpu.roll(t, LANES - half, 1) * s1 + pltpu.roll(t, half, 1) * s2


def _row_chunks(ref, n=MM_ROW_CHUNKS):
    rows = ref.shape[0] // n
    return [slice(c * rows, (c + 1) * rows) for c in range(n)]


def _scaled_dot(x_ref, w_ref, s_ref, rs):
    acc = _dot(x_ref[rs, :], w_ref[...])
    if s_ref is not None:
        r = lax.rsqrt(s_ref[rs, :] * (1.0 / x_ref.shape[1]) + RMS_EPS)
        acc = acc * _lane_tile(r, acc.shape[1] // LANES)
    return acc


def _write_scaled_copies(h, rs, g_ref, hb_refs):
    for i, hb_ref in enumerate(hb_refs):
        hb_ref[rs, :] = (h * g_ref[i:i + 1, :]).astype(BF16)
    return jnp.broadcast_to(jnp.sum(h * h, axis=-1, keepdims=True), (h.shape[0], LANES))


def _accumulate_ssq(parts, ssq_ref):
    part = jnp.concatenate(parts, axis=0)
    j = pl.program_id(1)

    @pl.when(j == 0)
    def _():
        ssq_ref[...] = part

    @pl.when(j > 0)
    def _():
        ssq_ref[...] = ssq_ref[...] + part


def _mm_kernel(*refs, act, scaled):
    x_ref, w_ref = refs[0], refs[1]
    s_ref = refs[2] if scaled else None
    o_ref = refs[-1]
    for rs in _row_chunks(o_ref):
        acc = _scaled_dot(x_ref, w_ref, s_ref, rs)
        if act == "silu":
            acc = _silu(acc)
        elif act == "sigmoid":
            acc = _sigmoid(acc)
        o_ref[rs, :] = acc.astype(o_ref.dtype)


def _mm_zgate_kernel(x_ref, w_ref, g_ref, o_ref, *, first_col):
    for rs in _row_chunks(o_ref):
        acc = _silu(_dot(x_ref[rs, :], w_ref[...]))
        for h in range(acc.shape[1] // LANES):
            sl = slice(h * LANES, (h + 1) * LANES)
            o_ref[rs, sl] = acc[:, sl] * g_ref[rs, first_col + h:first_col + h + 1]


def _mm_res_kernel(x_ref, w_ref, r_ref, g_ref, h_ref, *rest):
    parts = []
    for rs in _row_chunks(h_ref, 1):
        h = r_ref[rs, :] + _dot(x_ref[rs, :], w_ref[...])
        h_ref[rs, :] = h
        parts.append(_write_scaled_copies(h, rs, g_ref, rest[:-1]))
    _accumulate_ssq(parts, rest[-1])


def _ple_kernel(n_ref, wg_ref, s_ref, p_ref, wp_ref, h_ref, *rest, n_gain):
    o_ref = rest[1] if n_gain else rest[0]
    parts = []
    for rs in _row_chunks(o_ref):
        gate = _sigmoid(_scaled_dot(n_ref, wg_ref, s_ref, rs))
        emb = _dot(p_ref[rs, :].astype(BF16), wp_ref[...])
        h = h_ref[rs, :] + gate * emb
        o_ref[rs, :] = h
        if n_gain:
            parts.append(_write_scaled_copies(h, rs, rest[0], rest[2:-1]))
    if n_gain:
        _accumulate_ssq(parts, rest[-1])


def _mm_rope_kernel(*refs, half, n_rope, plain_too, scaled):
    x_ref, w_ref = refs[0], refs[1]
    s_ref = refs[2] if scaled else None
    c_ref, s1_ref, s2_ref = refs[3:6] if scaled else refs[2:5]
    o_refs = refs[6:] if scaled else refs[5:]
    for rs in _row_chunks(o_refs[0]):
        acc = _scaled_dot(x_ref, w_ref, s_ref, rs)
        c, s1, s2 = c_ref[rs, :], s1_ref[rs, :], s2_ref[rs, :]
        for ch in range(acc.shape[1] // LANES):
            sl = slice(ch * LANES, (ch + 1) * LANES)
            t = acc[:, sl]
            o_refs[0][rs, sl] = (_rope_lanes(t, c, s1, s2, half) if ch < n_rope else t).astype(BF16)
            if plain_too:
                o_refs[1][rs, sl] = t.astype(BF16)


def _mm_kvb_kernel(x_ref, w_ref, s_ref, c_ref, s1_ref, s2_ref, o_ref, *, half):
    kvw = SWA_KVH * SWA_DH
    n_ch = kvw // LANES
    for rs in _row_chunks(o_ref):
        acc = _scaled_dot(x_ref, w_ref, s_ref, rs)
        c, s1, s2 = c_ref[rs, :], s1_ref[rs, :], s2_ref[rs, :]
        low = lax.broadcasted_iota(jnp.int32, (acc.shape[0], LANES), 1) < SWA_DH
        for kind in range(2):
            for ch in range(n_ch):
                t = acc[:, kind * kvw + ch * LANES: kind * kvw + (ch + 1) * LANES]
                if kind == 0:
                    t = _rope_lanes(t, c, s1, s2, half)
                ts = pltpu.roll(t, SWA_DH, 1)
                variants = (jnp.where(low, t, 0.0), jnp.where(low, 0.0, ts),
                            jnp.where(low, ts, 0.0), jnp.where(low, 0.0, t))
                for var, val in enumerate(variants):
                    col = (kind * 4 + var) * kvw + ch * LANES
                    o_ref[rs, col:col + LANES] = val.astype(BF16)


def _mm_tiles(m, n, tn_max=MM_TN):
    tm = min(MM_TM, m)
    tn = min(tn_max, n)
    assert m % tm == 0 and n % tn == 0, (m, n)
    return tm, tn


def _ssq_spec(tm):
    return pl.BlockSpec((tm, LANES), lambda i, j: (i, 0))


def _w_cols(w, cols, m):
    first, n = (0, w.shape[1]) if cols is None else cols
    tm, tn = _mm_tiles(m, n)
    assert first % tn == 0
    return first // tn, n, tm, tn


def matmul(x, w, name, act=None, out_dtype=F32, ssq=None, cols=None):
    m, k = x.shape
    jb, n, tm, tn = _w_cols(w, cols, m)
    scaled = ssq is not None
    return pl.pallas_call(
        functools.partial(_mm_kernel, act=act, scaled=scaled),
        grid=(m // tm, n // tn),
        in_specs=[pl.BlockSpec((tm, k), lambda i, j: (i, 0)),
                  pl.BlockSpec((k, tn), lambda i, j: (0, jb + j))] + ([_ssq_spec(tm)] if scaled else []),
        out_specs=pl.BlockSpec((tm, tn), lambda i, j: (i, j)),
        out_shape=jax.ShapeDtypeStruct((m, n), out_dtype),
        compiler_params=_cparams(("arbitrary", "arbitrary")),
        name=name,
    )(x, w, *([ssq] if scaled else []))


def matmul_zgate(x, w, gates, branch, name):
    m, k = x.shape
    n = w.shape[1]
    tm, tn = _mm_tiles(m, n)
    assert tn == NSA_HPG * NSA_DH
    return pl.pallas_call(
        functools.partial(_mm_zgate_kernel, first_col=branch * NSA_HPG),
        grid=(m // tm, n // tn),
        in_specs=[pl.BlockSpec((tm, k), lambda i, j: (i, 0)),
                  pl.BlockSpec((k, tn), lambda i, j: (0, j)),
                  pl.BlockSpec((tm, LANES), lambda i, j: (i, j))],
        out_specs=pl.BlockSpec((tm, tn), lambda i, j: (i, j)),
        out_shape=jax.ShapeDtypeStruct((m, n), F32),
        compiler_params=_cparams(("arbitrary", "arbitrary")),
        name=name,
    )(x, w, gates)


def matmul_rope(x, w, tabs, seq, head_dim, rope_cols, plain_too, name, ssq=None, cols=None):
    m, k = x.shape
    jb, n, tm, tn = _w_cols(w, cols, m)
    assert seq % tm == 0
    per_seq = seq // tm
    scaled = ssq is not None
    tab_spec = pl.BlockSpec((tm, LANES), lambda i, j: (i % per_seq, 0))
    n_out = 2 if plain_too else 1
    outs = pl.pallas_call(
        functools.partial(_mm_rope_kernel, half=head_dim // 8, n_rope=rope_cols // LANES,
                          plain_too=plain_too, scaled=scaled),
        grid=(m // tm, n // tn),
        in_specs=[pl.BlockSpec((tm, k), lambda i, j: (i, 0)),
                  pl.BlockSpec((k, tn), lambda i, j: (0, jb + j))]
        + ([_ssq_spec(tm)] if scaled else []) + [tab_spec] * 3,
        out_specs=[pl.BlockSpec((tm, tn), lambda i, j: (i, j))] * n_out,
        out_shape=[jax.ShapeDtypeStruct((m, n), BF16)] * n_out,
        compiler_params=_cparams(("arbitrary", "arbitrary")),
        name=name,
    )(x, w, *([ssq] if scaled else []), *tabs)
    return outs


def matmul_kvb(x, w, ssq, tabs, seq, name):
    m, k = x.shape
    n = w.shape[1]
    tm = min(MM_TM, m)
    assert seq % tm == 0 and n == 2 * SWA_KVH * SWA_DH
    per_seq = seq // tm
    tab_spec = pl.BlockSpec((tm, LANES), lambda i: (i % per_seq, 0))
    return pl.pallas_call(
        functools.partial(_mm_kvb_kernel, half=SWA_DH // 8),
        grid=(m // tm,),
        in_specs=[pl.BlockSpec((tm, k), lambda i: (i, 0)),
                  pl.BlockSpec((k, n), lambda i: (0, 0)),
                  pl.BlockSpec((tm, LANES), lambda i: (i, 0))] + [tab_spec] * 3,
        out_specs=pl.BlockSpec((tm, 4 * n), lambda i: (i, 0)),
        out_shape=jax.ShapeDtypeStruct((m, 4 * n), BF16),
        compiler_params=_cparams(("arbitrary",)),
        name=name,
    )(x, w, ssq, *tabs)


def matmul_residual(x, w, res, gains, name):
    m, k = x.shape
    n = w.shape[1]
    kg = gains.shape[0]
    tm, tn = _mm_tiles(m, n, MM_TN_FUSED)
    tile = pl.BlockSpec((tm, tn), lambda i, j: (i, j))
    outs = pl.pallas_call(
        _mm_res_kernel,
        grid=(m // tm, n // tn),
        in_specs=[pl.BlockSpec((tm, k), lambda i, j: (i, 0)),
                  pl.BlockSpec((k, tn), lambda i, j: (0, j)),
                  tile,
                  pl.BlockSpec((kg, tn), lambda i, j: (0, j))],
        out_specs=[tile] + [tile] * kg + [_ssq_spec(tm)],
        out_shape=[jax.ShapeDtypeStruct((m, n), F32)] + [jax.ShapeDtypeStruct((m, n), BF16)] * kg
        + [jax.ShapeDtypeStruct((m, LANES), F32)],
        compiler_params=_cparams(("arbitrary", "arbitrary")),
        name=name,
    )(x, w, res, gains)
    return outs[0], outs[1:-1], outs[-1]


def ple_update(n, ssq, wg, p, wp, layer, h, gains, name):
    m, k = n.shape
    d = wg.shape[2]
    kp = p.shape[2]
    kg = 0 if gains is None else gains.shape[0]
    tm, tn = _mm_tiles(m, d, MM_TN_FUSED)
    tile = pl.BlockSpec((tm, tn), lambda i, j: (i, j))
    in_specs = [pl.BlockSpec((tm, k), lambda i, j: (i, 0)),
                pl.BlockSpec((None, k, tn), lambda i, j: (layer, 0, j)),
                _ssq_spec(tm),
                pl.BlockSpec((None, tm, kp), lambda i, j: (layer, i, 0)),
                pl.BlockSpec((None, kp, tn), lambda i, j: (layer, 0, j)),
                tile]
    args = [n, wg, ssq, p, wp, h]
    out_specs = [tile]
    out_shape = [jax.ShapeDtypeStruct((m, d), F32)]
    if kg:
        in_specs.append(pl.BlockSpec((kg, tn), lambda i, j: (0, j)))
        args.append(gains)
        out_specs += [tile] * kg + [_ssq_spec(tm)]
        out_shape += [jax.ShapeDtypeStruct((m, d), BF16)] * kg + [jax.ShapeDtypeStruct((m, LANES), F32)]
    outs = pl.pallas_call(
        functools.partial(_ple_kernel, n_gain=kg),
        grid=(m // tm, d // tn),
        in_specs=in_specs,
        out_specs=out_specs,
        out_shape=out_shape,
        compiler_params=_cparams(("arbitrary", "arbitrary")),
        name=name,
    )(*args)
    if kg:
        return outs[0], outs[1:-1], outs[-1]
    return outs[0]


A_W_COLS = 512


def _wt_cast_kernel(wt_ref, o_ref):
    o_ref[...] = wt_ref[...].T.astype(BF16)


def _cast_weight_cols(wt, first, width, name):
    k = wt.shape[1]
    cols = min(A_W_COLS, width)
    assert width % cols == 0
    return pl.pallas_call(
        _wt_cast_kernel,
        grid=(width // cols,),
        in_specs=[pl.BlockSpec((pl.Element(cols), pl.Element(k)), lambda i: (pl.multiple_of(first + i * cols, 8), 0))],
        out_specs=pl.BlockSpec((k, cols), lambda i: (0, i)),
        out_shape=jax.ShapeDtypeStruct((k, width), BF16),
        compiler_params=_cparams(("arbitrary",)),
        name=name,
    )(wt)


def split_a_weights(wa):
    wt = wa[0].T
    c_q, c_cmp, c_kv = NSA_WIDTH, NSA_WIDTH + 2 * NSA_KVW, NSA_WIDTH + 6 * NSA_KVW
    c_z = c_kv + 3 * NSA_G * NSA_HPG
    segs = [(0, c_q), (c_q, c_cmp - c_q), (c_cmp, c_kv - c_cmp), (c_kv, LANES)]
    segs += [(c_z + br * NSA_WIDTH, NSA_WIDTH) for br in range(3)]
    return [_cast_weight_cols(wt, first, width, "a_w_cast%d" % i) for i, (first, width) in enumerate(segs)]


def _compress_kernel(x_ref, pos_ref, w1_ref, w2_ref, o_ref, *, n_cmp_pad):
    nc = n_cmp_pad
    lo = jnp.zeros((nc, CMP_HIDDEN), F32)
    hi = jnp.zeros((nc, CMP_HIDDEN), F32)
    for l in range(CMP_STRIDE):
        xl = x_ref[pl.ds(l, nc, stride=CMP_STRIDE), :]
        a = (xl + pos_ref[0, l:l + 1, :]).astype(BF16)
        b = (xl + pos_ref[0, CMP_STRIDE + l:CMP_STRIDE + l + 1, :]).astype(BF16)
        lo = lo + _dot(a, w1_ref[0, l * LANES:(l + 1) * LANES, :])
        hi = hi + _dot(b, w1_ref[0, (CMP_STRIDE + l) * LANES:(CMP_STRIDE + l + 1) * LANES, :])
    hid = lo + pltpu.roll(hi, nc - 1, 0)
    out = _dot(_silu(hid).astype(BF16), w2_ref[0])
    n_idx = lax.broadcasted_iota(jnp.int32, (nc, LANES), 0)
    o_ref[0, 0, 0] = jnp.where(n_idx < nc - 1, out, 0.0).astype(BF16)


def compress_kv(kcvc, pos, w1, w2, batch, seq):
    nc = seq // CMP_STRIDE
    return pl.pallas_call(
        functools.partial(_compress_kernel, n_cmp_pad=nc),
        grid=(batch, 2, NSA_G),
        in_specs=[pl.BlockSpec((seq, LANES), lambda b, s, g: (b, s * NSA_G + g)),
                  pl.BlockSpec((1, CMP_LEN, LANES), lambda b, s, g: (s, 0, 0)),
                  pl.BlockSpec((1, CMP_LEN * LANES, CMP_HIDDEN), lambda b, s, g: (s, 0, 0)),
                  pl.BlockSpec((1, CMP_HIDDEN, LANES), lambda b, s, g: (s, 0, 0))],
        out_specs=pl.BlockSpec((1, 1, 1, nc, LANES), lambda b, s, g: (b, s, g, 0, 0)),
        out_shape=jax.ShapeDtypeStruct((batch, 2, NSA_G, nc, LANES), BF16),
        compiler_params=_cparams(("arbitrary", "arbitrary", "arbitrary")),
        name="compress_kv",
    )(kcvc, pos, w1, w2)


def _nsa_kernel(qp_ref, qr_ref, z0_ref, z1_ref, z2_ref,
                kc_ref, vc_ref, ks_ref, vs_ref, kw_ref, vw_ref, et_ref, mt_ref,
                o_ref, qa_scr, m_scr, acc_scr, mix_scr, *, seq):
    tq, tk, hpg = NSA_TQ, NSA_TK, NSA_HPG
    rows = hpg * tq
    nc = seq // CMP_STRIDE
    ns = seq // SEL_BLOCK
    n_top = min(SEL_TOPK, ns)
    nsp = mt_ref.shape[0]
    c_exp = LOG2E / math.sqrt(NSA_DH)
    t0 = pl.program_id(2) * tq
    ones_k = jnp.ones((tk, LANES), BF16)

    t_col = t0 + lax.broadcasted_iota(jnp.int32, (tq, 1), 0)

    n_idx = lax.broadcasted_iota(jnp.int32, (1, nc), 1)
    valid = (n_idx * CMP_STRIDE + (CMP_LEN - 1) <= t_col) & (n_idx < nc - 1)
    bias_c = jnp.where(valid, 0.0, NEG_BIG)
    row_ok = t0 + lax.broadcasted_iota(jnp.int32, (tq, LANES), 0) >= CMP_LEN - 1
    lane_ok = t0 + lax.broadcasted_iota(jnp.int32, (ns, tq), 1) >= CMP_LEN - 1
    kc = kc_ref[0, 0, 0]
    vc_aug = jnp.concatenate([vc_ref[0, 0, 0], jnp.ones((nc, LANES), BF16)], axis=1)
    hq = tq // 2
    span = NSA_WINDOW + hq
    win = []
    for hh in range(2):
        kw0 = pl.multiple_of(jnp.maximum(t0 + hh * hq - NSA_WINDOW, 0), hq)
        t_half = t_col[hh * hq:(hh + 1) * hq]
        kpos_w = kw0 + lax.broadcasted_iota(jnp.int32, (1, span), 1)
        win.append((kw_ref[pl.ds(kw0, span), :],
                    jnp.concatenate([vw_ref[pl.ds(kw0, span), :], jnp.ones((span, LANES), BF16)], axis=1),
                    jnp.where((kpos_w <= t_half) & (kpos_w > t_half - NSA_WINDOW), 0.0, NEG_BIG)))
    q_all = jnp.concatenate([qp_ref[:, h * LANES:(h + 1) * LANES] for h in range(hpg)], axis=0)
    sc = _dot_nt(q_all, kc) + jnp.concatenate([bias_c] * hpg, axis=0)
    m = jnp.max(sc, axis=-1, keepdims=True)
    pe = jnp.exp2((sc - m) * c_exp).astype(BF16)
    acc_c = _dot(pe, vc_aug)
    imp_t = _dot_nt(mt_ref[...], pe)
    imp_parts = []
    for h in range(hpg):
        sl = slice(h * LANES, (h + 1) * LANES)
        rs = slice(h * tq, (h + 1) * tq)
        oc = jnp.where(row_ok, acc_c[rs, 0:LANES] * (1.0 / acc_c[rs, LANES:2 * LANES]), 0.0)
        imp_parts.append(imp_t[0:ns, rs] * (1.0 / imp_t[nsp - 1:nsp, rs]))
        ows = []
        for hh, (kwin, vw_aug, bias_w) in enumerate(win):
            sc = _dot_nt(qr_ref[hh * hq:(hh + 1) * hq, sl], kwin) + bias_w
            m = jnp.max(sc, axis=-1, keepdims=True)
            pe_w = jnp.exp2((sc - m) * c_exp).astype(BF16)
            acc = _dot(pe_w, vw_aug)
            ows.append(acc[:, 0:LANES] * (1.0 / acc[:, LANES:2 * LANES]))
        ow = jnp.concatenate(ows, axis=0)
        mix_scr[:, sl] = oc * z0_ref[:, sl] + ow * z2_ref[:, sl]
    while len(imp_parts) > 1:
        imp_parts = [imp_parts[i] + imp_parts[i + 1] for i in range(0, len(imp_parts), 2)]
    imp = jnp.where(lane_ok, imp_parts[0], 0.0)

    j_io = lax.broadcasted_iota(jnp.int32, (ns, tq), 0)
    t_io = t0 + lax.broadcasted_iota(jnp.int32, (ns, tq), 1)
    dist = lax.shift_right_logical(t_io, int(math.log2(SEL_BLOCK))) - j_io
    forced = (j_io == 0) | ((dist >= 0) & (dist < SEL_LOCAL))
    imp = jnp.where(forced, FORCE_SCORE, imp)
    imp = jnp.where(dist >= 0, imp, -1.0)
    sub = 8
    ranks = []
    for v in range(ns // sub):
        blk = imp[v * sub:(v + 1) * sub, :]
        j_v = v * sub + lax.broadcasted_iota(jnp.int32, (sub, tq), 0)
        r = jnp.zeros((sub, tq), F32)
        for k in range(ns):
            rk = imp[k:k + 1, :]
            if k < v * sub:
                ahead = rk >= blk
            elif k >= (v + 1) * sub:
                ahead = rk > blk
            else:
                ahead = (rk > blk) | ((rk == blk) & (j_v > k))
            r = r + jnp.where(ahead, 1.0, 0.0)
        ranks.append(r)
    rank = jnp.concatenate(ranks, axis=0)
    not_sel_t = jnp.concatenate([jnp.where(rank < n_top, 0.0, 1.0), jnp.ones((nsp - ns, tq), F32)], axis=0)
    not_sel = not_sel_t.T.astype(BF16)

    for h in range(hpg):
        qa_scr[h * tq:(h + 1) * tq, 0:LANES] = qr_ref[:, h * LANES:(h + 1) * LANES]
        qa_scr[h * tq:(h + 1) * tq, LANES:2 * LANES] = not_sel

    m_scr[...] = jnp.full((rows, LANES), NEG_BIG, F32)
    acc_scr[...] = jnp.zeros((rows, 2 * LANES), F32)
    n_kt = lax.shift_right_logical(t0 + (tq + tk - 1), int(math.log2(tk)))

    def slc_tile(kt, causal, width=tk):
        k0 = pl.multiple_of(kt * tk, tk)
        k_aug = jnp.concatenate([ks_ref[pl.ds(k0, width), :], et_ref[pl.ds(k0, width), :]], axis=1)
        v_aug = jnp.concatenate([vs_ref[pl.ds(k0, width), :], ones_k[0:width]], axis=1)
        if causal:
            kpos = k0 + lax.broadcasted_iota(jnp.int32, (1, width), 1)
            bias = jnp.where(kpos <= t_col, 0.0, NEG_BIG)
        for h in range(hpg):
            rs = slice(h * tq, (h + 1) * tq)
            sc = _dot_nt(qa_scr[rs, :], k_aug)
            if causal:
                sc = sc + bias
            m_old = m_scr[rs, :]
            m_new = jnp.maximum(m_old, jnp.max(sc, axis=-1, keepdims=True))
            alpha = jnp.exp2((m_old - m_new) * c_exp)
            pe = jnp.exp2((sc - _lane_tile(m_new, width // LANES)) * c_exp)
            acc_scr[rs, :] = _lane_tile(alpha, 2) * acc_scr[rs, :] + _dot(pe.astype(BF16), v_aug)
            m_scr[rs, :] = m_new

    def pair_body(i, carry):
        slc_tile(2 * i, causal=False)
        slc_tile(2 * i + 1, causal=False)
        return carry

    lax.fori_loop(0, lax.shift_right_logical(n_kt - 1, 1), pair_body, 0)
    odd = ((n_kt - 1) & 1) == 1
    short = (t0 & (tk - 1)) == 0
    for odd_case in (True, False):
        for short_case in (True, False):
            @pl.when((odd if odd_case else jnp.logical_not(odd)) & (short if short_case else jnp.logical_not(short)))
            def _(odd_case=odd_case, short_case=short_case):
                if odd_case:
                    slc_tile(n_kt - 2, causal=False)
                slc_tile(n_kt - 1, causal=True, width=tq if short_case else tk)

    for h in range(hpg):
        rs = slice(h * tq, (h + 1) * tq)
        sl = slice(h * LANES, (h + 1) * LANES)
        o_slc = acc_scr[rs, 0:LANES] * (1.0 / acc_scr[rs, LANES:2 * LANES])
        o_ref[:, sl] = (mix_scr[:, sl] + o_slc * z1_ref[:, sl]).astype(BF16)


def nsa_attention(qp, qr, zg, cmp_kv, kv, e_t, cmp_to_sel_t, batch, seq):
    t = qp.shape[0]
    tq = NSA_TQ
    assert seq % NSA_TK == 0 and NSA_TK % tq == 0 and seq >= NSA_WINDOW + tq
    qt_per_seq = seq // tq
    nc = seq // CMP_STRIDE
    hw = NSA_HPG * NSA_DH

    qz_spec = pl.BlockSpec((tq, hw), lambda b, g, i: (b * qt_per_seq + i, g))
    cmp_specs = [pl.BlockSpec((1, 1, 1, nc, LANES), functools.partial(
        lambda b, g, i, s: (b, s, g, 0, 0), s=s)) for s in range(2)]
    kv_specs = [pl.BlockSpec((seq, LANES), functools.partial(
        lambda b, g, i, s: (b, s * NSA_G + g), s=s)) for s in range(4)]
    const_specs = [pl.BlockSpec(e_t.shape, lambda b, g, i: (0, 0)),
                   pl.BlockSpec(cmp_to_sel_t.shape, lambda b, g, i: (0, 0))]
    rows = NSA_HPG * tq
    return pl.pallas_call(
        functools.partial(_nsa_kernel, seq=seq),
        grid=(batch, NSA_G, qt_per_seq),
        in_specs=[qz_spec] * 5 + cmp_specs + kv_specs + const_specs,
        out_specs=qz_spec,
        out_shape=jax.ShapeDtypeStruct((t, NSA_WIDTH), BF16),
        scratch_shapes=[pltpu.VMEM((rows, 2 * LANES), BF16),
                        pltpu.VMEM((rows, LANES), F32),
                        pltpu.VMEM((rows, 2 * LANES), F32),
                        pltpu.VMEM((tq, hw), F32)],
        compiler_params=_cparams(("arbitrary", "arbitrary", "arbitrary")),
        name="nsa_attention",
    )(qp, qr, *zg, cmp_kv, cmp_kv, kv, kv, kv, kv, e_t, cmp_to_sel_t)


def _swa_kernel(sink_ref, q_ref, z_ref, *rest):
    kv_refs, o_ref = rest[:-1], rest[-1]
    tq = ATT_TQ
    n_pair = SWA_KVH // 2
    per_kv = SWA_HPG * SWA_DH // LANES
    scale = 1.0 / math.sqrt(SWA_DH)
    c_exp = LOG2E * scale
    t0 = pl.program_id(1) * tq
    ones_k = jnp.ones((2 * tq, LANES), BF16)

    def window(kind, var, pair):
        base = (kind * 4 + var) * 2
        sl = slice(pair * LANES, (pair + 1) * LANES)
        return jnp.concatenate([kv_refs[base][:, sl], kv_refs[base + 1][:, sl]], axis=0)

    t_col = t0 + lax.broadcasted_iota(jnp.int32, (tq, 1), 0)
    kpos = t0 - tq + lax.broadcasted_iota(jnp.int32, (1, 2 * tq), 1)
    ok = (kpos <= t_col) & (kpos > t_col - SWA_WINDOW) & (kpos >= 0)
    bias = jnp.where(ok, 0.0, NEG_BIG)
    col0 = lax.broadcasted_iota(jnp.int32, (tq, LANES), 1) == 0
    keep = jnp.where(lax.broadcasted_iota(jnp.int32, (2 * tq, LANES), 0) == 0, 0.0, 1.0).astype(BF16)

    for pair in range(n_pair):
        for j in range(2):
            k_var = [window(0, 2 * j + par, pair) * keep for par in range(2)]
            v_var = [jnp.concatenate([window(1, 2 * j + par, pair) * keep, ones_k], axis=1) for par in range(2)]
            for c in range(per_kv):
                col = ((pair * 2 + j) * per_kv + c) * LANES
                qc = q_ref[:, col:col + LANES]
                o_chunk = None
                for par in range(2):
                    head = (pair * 2 + j) * SWA_HPG + 2 * c + par
                    sink_raw = sink_ref[head] * math.sqrt(SWA_DH)
                    bias_h = jnp.concatenate([jnp.where(col0, sink_raw, bias[:, 0:LANES]), bias[:, LANES:]], axis=1)
                    sc = _dot_nt(qc, k_var[par]) + bias_h
                    m = jnp.max(sc, axis=-1, keepdims=True)
                    e = jnp.exp2((sc - m) * c_exp)
                    acc = _dot(e.astype(BF16), v_var[par])
                    o_par = acc[:, 0:LANES] * (1.0 / acc[:, LANES:2 * LANES])
                    o_chunk = o_par if o_chunk is None else o_chunk + o_par
                o_ref[:, col:col + LANES] = (o_chunk * z_ref[:, col:col + LANES]).astype(BF16)


def swa_attention(q, zs, kvb, sinks, batch, seq):
    t = q.shape[0]
    tq = ATT_TQ
    qt_per_seq = seq // tq
    kvw = SWA_KVH * SWA_DH

    def row(b, i):
        return b * qt_per_seq + i

    def prev_row(b, i):
        return b * qt_per_seq + jnp.maximum(i - 1, 0)

    qz_spec = pl.BlockSpec((tq, SWA_WIDTH), lambda b, i: (row(b, i), 0))
    kv_specs = []
    for cb in range(8):
        kv_specs.append(pl.BlockSpec((tq, kvw), functools.partial(lambda b, i, cb: (prev_row(b, i), cb), cb=cb)))
        kv_specs.append(pl.BlockSpec((tq, kvw), functools.partial(lambda b, i, cb: (row(b, i), cb), cb=cb)))
    return pl.pallas_call(
        _swa_kernel,
        grid=(batch, qt_per_seq),
        in_specs=[pl.BlockSpec(memory_space=pltpu.SMEM), qz_spec, qz_spec] + kv_specs,
        out_specs=qz_spec,
        out_shape=jax.ShapeDtypeStruct((t, SWA_WIDTH), BF16),
        compiler_params=_cparams(("arbitrary", "arbitrary")),
        name="swa_attention",
    )(sinks, q, zs, *([kvb] * 16))


def _selection_constants(seq):
    nc = seq // CMP_STRIDE
    ns = seq // SEL_BLOCK
    nsp = -(-ns // LANES) * LANES
    e_t = np.zeros((seq, nsp), np.float32)
    e_t[np.arange(seq), np.arange(seq) // SEL_BLOCK] = NEG_BIG
    c0 = np.arange(nc - 1) * CMP_STRIDE
    s0 = np.arange(ns) * SEL_BLOCK
    overlap = (c0[:, None] < s0[None, :] + SEL_BLOCK) & (c0[:, None] + CMP_LEN > s0[None, :])
    assert ns < nsp
    cmp_to_sel_t = np.zeros((nsp, nc), np.float32)
    cmp_to_sel_t[:ns, :nc - 1] = overlap.T
    cmp_to_sel_t[nsp - 1, :] = 1.0
    return jnp.asarray(e_t, BF16), jnp.asarray(cmp_to_sel_t, BF16)


def kernel(x, p, a_norm, a_w_in, a_w_out, a_cmp_pos_k, a_cmp_w1_k, a_cmp_w2_k, a_cmp_pos_v, a_cmp_w1_v,
           a_cmp_w2_v, kv_norm, w_kv, b_norm, b_w_in, b_w_out, b_sinks, ple_norm, ple_gate_w, ple_proj,
           final_norm):
    batch, seq, d = x.shape
    assert d == D_MODEL and p.shape[0] == 2 and a_w_in.shape[0] == 1 and b_w_in.shape[0] == 1
    t = batch * seq
    xf = x.reshape(t, d)

    n_gate = 3 * NSA_G * NSA_HPG
    w_q, w_cmp, w_kvs, w_gate, *w_z = split_a_weights(a_w_in)
    wg = w_gate[:, :n_gate].reshape(d, 3, NSA_G, NSA_HPG).transpose(0, 2, 1, 3)
    wg = wg.reshape(d, NSA_G, 3 * NSA_HPG)
    wg = jnp.pad(wg, ((0, 0), (0, 0), (0, LANES - 3 * NSA_HPG))).reshape(d, NSA_G * LANES)
    pos_kv = jnp.stack([a_cmp_pos_k[0], a_cmp_pos_v[0]])
    w1_kv = jnp.stack([a_cmp_w1_k[0], a_cmp_w1_v[0]]).astype(BF16)
    w2_kv = jnp.stack([a_cmp_w2_k[0], a_cmp_w2_v[0]]).astype(BF16)
    wb = b_w_in[0].astype(BF16)
    tabs_a = _rope_tables(seq, NSA_DH)
    tabs_b = _rope_tables(seq, SWA_DH)
    e_t, cmp_to_sel_t = _selection_constants(seq)

    (n_a,) = rms_norm_rows(xf, a_norm, BF16)
    q_rot, q_plain = matmul_rope(n_a, w_q, tabs_a, seq, NSA_DH, MM_TN, True, "a_q_proj")
    (kv_a,) = matmul_rope(n_a, w_kvs, tabs_a, seq, NSA_DH, NSA_KVW, False, "a_kv_proj")
    kcvc = matmul(n_a, w_cmp, "a_cmp_proj")
    gates = matmul(n_a, wg, "a_gate_proj", act="sigmoid")
    zg = [matmul_zgate(n_a, w_z[br], gates, br, "a_z_proj%d" % br) for br in range(3)]
    cmp_kv = compress_kv(kcvc, pos_kv, w1_kv, w2_kv, batch, seq)
    mixed = nsa_attention(q_plain, q_rot, zg, cmp_kv, kv_a, e_t, cmp_to_sel_t, batch, seq)
    h, (n_p,), ssq = matmul_residual(mixed, a_w_out[0].astype(BF16), xf, ple_norm[0:1], "a_out_proj")
    pf = p.reshape(2, t, PLE_DIM)
    ple_wg = ple_gate_w.astype(BF16)
    ple_wp = ple_proj.astype(BF16)
    h, (n_kv, n_b), ssq = ple_update(n_p, ssq, ple_wg, pf, ple_wp, 0, h, jnp.stack([kv_norm, b_norm[0]]), "ple0")

    kvb = matmul_kvb(n_kv, w_kv.astype(BF16), ssq, tabs_b, seq, "kv_proj")
    (q_b,) = matmul_rope(n_b, wb, tabs_b, seq, SWA_DH, MM_TN, False, "b_q_proj", ssq=ssq, cols=(0, SWA_WIDTH))
    zs_b = matmul(n_b, wb, "b_z_proj", act="silu", ssq=ssq, cols=(SWA_WIDTH, SWA_WIDTH))
    ob = swa_attention(q_b, zs_b, kvb, b_sinks[0], batch, seq)
    h, (n_p,), ssq = matmul_residual(ob, b_w_out[0].astype(BF16), h, ple_norm[1:2], "b_out_proj")
    h = ple_update(n_p, ssq, ple_wg, pf, ple_wp, 1, h, None, "ple1")

    (out,) = rms_norm_rows(h, final_norm[None, :], F32)
    return out.reshape(batch, seq, d)
```

```python
import functools
import math

import numpy as np
import jax
import jax.numpy as jnp
from jax import lax
from jax.experimental import pallas as pl
from jax.experimental.pallas import tpu as pltpu

F32 = jnp.float32
BF16 = jnp.bfloat16

D_MODEL = 4096
RMS_EPS = 1e-6
ROPE_THETA = 500000.0
NEG_BIG = -1e30
FORCE_SCORE = 1e9
PLE_DIM = 256

NSA_DH = 128
NSA_G = 4
NSA_HPG = 8
NSA_WIDTH = 4096
NSA_KVW = 512
CMP_LEN = 32
CMP_STRIDE = 16
CMP_HIDDEN = 256
SEL_BLOCK = 64
SEL_TOPK = 16
SEL_LOCAL = 2
NSA_WINDOW = 512

SWA_DH = 64
SWA_KVH = 8
SWA_HPG = 8
SWA_WIDTH = 4096
SWA_WINDOW = 128

LANES = 128
VMEM_LIMIT_BYTES = 58 * 1024 * 1024

MM_TM = 1024
MM_TN = 1024
MM_TN_FUSED = 512
MM_ROW_CHUNKS = 8
NORM_ROWS = 256
ATT_TQ = 128
NSA_TQ = 256
NSA_TK = 512

LOG2E = math.log2(math.e)


def _cparams(sem):
    return pltpu.CompilerParams(dimension_semantics=sem, vmem_limit_bytes=VMEM_LIMIT_BYTES)


def _sigmoid(v):
    return 1.0 / (1.0 + jnp.exp(-v))


def _silu(v):
    return v * _sigmoid(v)


def _dot(a, b):
    return jnp.dot(a, b, preferred_element_type=F32)


def _dot_nt(a, b):
    return lax.dot_general(a, b, (((1,), (1,)), ((), ())), preferred_element_type=F32)


def _lane_tile(x, n):
    return jnp.concatenate([x] * n, axis=1)


def _norm_kernel(h_ref, g_ref, *o_refs):
    x = h_ref[...]
    r = lax.rsqrt(jnp.mean(x * x, axis=-1, keepdims=True) + RMS_EPS)
    y = x * r
    for i, o_ref in enumerate(o_refs):
        o_ref[...] = (y * g_ref[i:i + 1, :]).astype(o_ref.dtype)


def rms_norm_rows(h, gains, out_dtype):
    t, d = h.shape
    k = gains.shape[0]
    outs = pl.pallas_call(
        _norm_kernel,
        grid=(t // NORM_ROWS,),
        in_specs=[pl.BlockSpec((NORM_ROWS, d), lambda i: (i, 0)),
                  pl.BlockSpec((k, d), lambda i: (0, 0))],
        out_specs=[pl.BlockSpec((NORM_ROWS, d), lambda i: (i, 0))] * k,
        out_shape=[jax.ShapeDtypeStruct((t, d), out_dtype)] * k,
        compiler_params=_cparams(("arbitrary",)),
        name="rms_norm",
    )(h, gains)
    return outs


def _rope_tables(seq, head_dim):
    rot = head_dim // 4
    half = rot // 2
    inv_freq = ROPE_THETA ** (-jnp.arange(half, dtype=F32) / half)
    ang = jnp.arange(seq, dtype=jnp.int32).astype(F32)[:, None] * inv_freq[None, :]
    cos, sin = jnp.cos(ang), jnp.sin(ang)
    rest = head_dim - rot
    zeros_h = jnp.zeros((seq, half), F32)
    c = jnp.concatenate([cos, cos, jnp.ones((seq, rest), F32)], axis=1)
    s1 = jnp.concatenate([-sin, zeros_h, jnp.zeros((seq, rest), F32)], axis=1)
    s2 = jnp.concatenate([zeros_h, sin, jnp.zeros((seq, rest), F32)], axis=1)
    reps = LANES // head_dim
    return tuple(jnp.tile(a, (1, reps)) for a in (c, s1, s2))


def _rope_lanes(t, c, s1, s2, half):
    return t * c + pltpu.roll(t, LANES - half, 1) * s1 + pltpu.roll(t, half, 1) * s2


def _row_chunks(ref, n=MM_ROW_CHUNKS):
    rows = ref.shape[0] // n
    return [slice(c * rows, (c + 1) * rows) for c in range(n)]


def _scaled_dot(x_ref, w_ref, s_ref, rs):
    acc = _dot(x_ref[rs, :], w_ref[...])
    if s_ref is not None:
        r = lax.rsqrt(s_ref[rs, :] * (1.0 / x_ref.shape[1]) + RMS_EPS)
        acc = acc * _lane_tile(r, acc.shape[1] // LANES)
    return acc


def _write_scaled_copies(h, rs, g_ref, hb_refs):
    for i, hb_ref in enumerate(hb_refs):
        hb_ref[rs, :] = (h * g_ref[i:i + 1, :]).astype(BF16)
    return jnp.broadcast_to(jnp.sum(h * h, axis=-1, keepdims=True), (h.shape[0], LANES))


def _accumulate_ssq(parts, ssq_ref):
    part = jnp.concatenate(parts, axis=0)
    j = pl.program_id(1)

    @pl.when(j == 0)
    def _():
        ssq_ref[...] = part

    @pl.when(j > 0)
    def _():
        ssq_ref[...] = ssq_ref[...] + part


def _mm_kernel(*refs, act, scaled):
    x_ref, w_ref = refs[0], refs[1]
    s_ref = refs[2] if scaled else None
    o_ref = refs[-1]
    for rs in _row_chunks(o_ref):
        acc = _scaled_dot(x_ref, w_ref, s_ref, rs)
        if act == "silu":
            acc = _silu(acc)
        elif act == "sigmoid":
            acc = _sigmoid(acc)
        o_ref[rs, :] = acc.astype(o_ref.dtype)


def _mm_zgate_kernel(x_ref, w_ref, g_ref, o_ref, *, first_col):
    for rs in _row_chunks(o_ref):
        acc = _silu(_dot(x_ref[rs, :], w_ref[...]))
        for h in range(acc.shape[1] // LANES):
            sl = slice(h * LANES, (h + 1) * LANES)
            o_ref[rs, sl] = acc[:, sl] * g_ref[rs, first_col + h:first_col + h + 1]


def _mm_res_kernel(x_ref, w_ref, r_ref, g_ref, h_ref, *rest):
    parts = []
    for rs in _row_chunks(h_ref):
        h = r_ref[rs, :] + _dot(x_ref[rs, :], w_ref[...])
        h_ref[rs, :] = h
        parts.append(_write_scaled_copies(h, rs, g_ref, rest[:-1]))
    _accumulate_ssq(parts, rest[-1])


def _ple_kernel(n_ref, wg_ref, s_ref, p_ref, wp_ref, h_ref, *rest, n_gain):
    o_ref = rest[1] if n_gain else rest[0]
    parts = []
    for rs in _row_chunks(o_ref):
        gate = _sigmoid(_scaled_dot(n_ref, wg_ref, s_ref, rs))
        emb = _dot(p_ref[rs, :].astype(BF16), wp_ref[...])
        h = h_ref[rs, :] + gate * emb
        o_ref[rs, :] = h
        if n_gain:
            parts.append(_write_scaled_copies(h, rs, rest[0], rest[2:-1]))
    if n_gain:
        _accumulate_ssq(parts, rest[-1])


def _mm_rope_kernel(*refs, half, n_rope, plain_too, scaled):
    x_ref, w_ref = refs[0], refs[1]
    s_ref = refs[2] if scaled else None
    c_ref, s1_ref, s2_ref = refs[3:6] if scaled else refs[2:5]
    o_refs = refs[6:] if scaled else refs[5:]
    for rs in _row_chunks(o_refs[0]):
        acc = _scaled_dot(x_ref, w_ref, s_ref, rs)
        c, s1, s2 = c_ref[rs, :], s1_ref[rs, :], s2_ref[rs, :]
        for ch in range(acc.shape[1] // LANES):
            sl = slice(ch * LANES, (ch + 1) * LANES)
            t = acc[:, sl]
            o_refs[0][rs, sl] = (_rope_lanes(t, c, s1, s2, half) if ch < n_rope else t).astype(BF16)
            if plain_too:
                o_refs[1][rs, sl] = t.astype(BF16)


def _mm_kvb_kernel(x_ref, w_ref, s_ref, c_ref, s1_ref, s2_ref, o_ref, *, half):
    kvw = SWA_KVH * SWA_DH
    n_ch = kvw // LANES
    for rs in _row_chunks(o_ref):
        acc = _scaled_dot(x_ref, w_ref, s_ref, rs)
        c, s1, s2 = c_ref[rs, :], s1_ref[rs, :], s2_ref[rs, :]
        low = lax.broadcasted_iota(jnp.int32, (acc.shape[0], LANES), 1) < SWA_DH
        for kind in range(2):
            for ch in range(n_ch):
                t = acc[:, kind * kvw + ch * LANES: kind * kvw + (ch + 1) * LANES]
                if kind == 0:
                    t = _rope_lanes(t, c, s1, s2, half)
                ts = pltpu.roll(t, SWA_DH, 1)
                variants = (jnp.where(low, t, 0.0), jnp.where(low, 0.0, ts),
                            jnp.where(low, ts, 0.0), jnp.where(low, 0.0, t))
                for var, val in enumerate(variants):
                    col = (kind * 4 + var) * kvw + ch * LANES
                    o_ref[rs, col:col + LANES] = val.astype(BF16)


def _mm_tiles(m, n, tn_max=MM_TN):
    tm = min(MM_TM, m)
    tn = min(tn_max, n)
    assert m % tm == 0 and n % tn == 0, (m, n)
    return tm, tn


def _ssq_spec(tm):
    return pl.BlockSpec((tm, LANES), lambda i, j: (i, 0))


def _w_cols(w, cols, m):
    first, n = (0, w.shape[1]) if cols is None else cols
    tm, tn = _mm_tiles(m, n)
    assert first % tn == 0
    return first // tn, n, tm, tn


def matmul(x, w, name, act=None, out_dtype=F32, ssq=None, cols=None):
    m, k = x.shape
    jb, n, tm, tn = _w_cols(w, cols, m)
    scaled = ssq is not None
    return pl.pallas_call(
        functools.partial(_mm_kernel, act=act, scaled=scaled),
        grid=(m // tm, n // tn),
        in_specs=[pl.BlockSpec((tm, k), lambda i, j: (i, 0)),
                  pl.BlockSpec((k, tn), lambda i, j: (0, jb + j))] + ([_ssq_spec(tm)] if scaled else []),
        out_specs=pl.BlockSpec((tm, tn), lambda i, j: (i, j)),
        out_shape=jax.ShapeDtypeStruct((m, n), out_dtype),
        compiler_params=_cparams(("arbitrary", "arbitrary")),
        name=name,
    )(x, w, *([ssq] if scaled else []))


def matmul_zgate(x, w, gates, branch, name):
    m, k = x.shape
    n = w.shape[1]
    tm, tn = _mm_tiles(m, n)
    assert tn == NSA_HPG * NSA_DH
    return pl.pallas_call(
        functools.partial(_mm_zgate_kernel, first_col=branch * NSA_HPG),
        grid=(m // tm, n // tn),
        in_specs=[pl.BlockSpec((tm, k), lambda i, j: (i, 0)),
                  pl.BlockSpec((k, tn), lambda i, j: (0, j)),
                  pl.BlockSpec((tm, LANES), lambda i, j: (i, j))],
        out_specs=pl.BlockSpec((tm, tn), lambda i, j: (i, j)),
        out_shape=jax.ShapeDtypeStruct((m, n), F32),
        compiler_params=_cparams(("arbitrary", "arbitrary")),
        name=name,
    )(x, w, gates)


def matmul_rope(x, w, tabs, seq, head_dim, rope_cols, plain_too, name, ssq=None, cols=None):
    m, k = x.shape
    jb, n, tm, tn = _w_cols(w, cols, m)
    assert seq % tm == 0
    per_seq = seq // tm
    scaled = ssq is not None
    tab_spec = pl.BlockSpec((tm, LANES), lambda i, j: (i % per_seq, 0))
    n_out = 2 if plain_too else 1
    outs = pl.pallas_call(
        functools.partial(_mm_rope_kernel, half=head_dim // 8, n_rope=rope_cols // LANES,
                          plain_too=plain_too, scaled=scaled),
        grid=(m // tm, n // tn),
        in_specs=[pl.BlockSpec((tm, k), lambda i, j: (i, 0)),
                  pl.BlockSpec((k, tn), lambda i, j: (0, jb + j))]
        + ([_ssq_spec(tm)] if scaled else []) + [tab_spec] * 3,
        out_specs=[pl.BlockSpec((tm, tn), lambda i, j: (i, j))] * n_out,
        out_shape=[jax.ShapeDtypeStruct((m, n), BF16)] * n_out,
        compiler_params=_cparams(("arbitrary", "arbitrary")),
        name=name,
    )(x, w, *([ssq] if scaled else []), *tabs)
    return outs


def matmul_kvb(x, w, ssq, tabs, seq, name):
    m, k = x.shape
    n = w.shape[1]
    tm = min(MM_TM, m)
    assert seq % tm == 0 and n == 2 * SWA_KVH * SWA_DH
    per_seq = seq // tm
    tab_spec = pl.BlockSpec((tm, LANES), lambda i: (i % per_seq, 0))
    return pl.pallas_call(
        functools.partial(_mm_kvb_kernel, half=SWA_DH // 8),
        grid=(m // tm,),
        in_specs=[pl.BlockSpec((tm, k), lambda i: (i, 0)),
                  pl.BlockSpec((k, n), lambda i: (0, 0)),
                  pl.BlockSpec((tm, LANES), lambda i: (i, 0))] + [tab_spec] * 3,
        out_specs=pl.BlockSpec((tm, 4 * n), lambda i: (i, 0)),
        out_shape=jax.ShapeDtypeStruct((m, 4 * n), BF16),
        compiler_params=_cparams(("arbitrary",)),
        name=name,
    )(x, w, ssq, *tabs)


def matmul_residual(x, w, res, gains, name):
    m, k = x.shape
    n = w.shape[1]
    kg = gains.shape[0]
    tm, tn = _mm_tiles(m, n, MM_TN)
    tile = pl.BlockSpec((tm, tn), lambda i, j: (i, j))
    outs = pl.pallas_call(
        _mm_res_kernel,
        grid=(m // tm, n // tn),
        in_specs=[pl.BlockSpec((tm, k), lambda i, j: (i, 0)),
                  pl.BlockSpec((k, tn), lambda i, j: (0, j)),
                  tile,
                  pl.BlockSpec((kg, tn), lambda i, j: (0, j))],
        out_specs=[tile] + [tile] * kg + [_ssq_spec(tm)],
        out_shape=[jax.ShapeDtypeStruct((m, n), F32)] + [jax.ShapeDtypeStruct((m, n), BF16)] * kg
        + [jax.ShapeDtypeStruct((m, LANES), F32)],
        compiler_params=_cparams(("arbitrary", "arbitrary")),
        name=name,
    )(x, w, res, gains)
    return outs[0], outs[1:-1], outs[-1]


def ple_update(n, ssq, wg, p, wp, layer, h, gains, name):
    m, k = n.shape
    d = wg.shape[2]
    kp = p.shape[2]
    kg = 0 if gains is None else gains.shape[0]
    tm, tn = _mm_tiles(m, d, MM_TN_FUSED if kg else MM_TN)
    tile = pl.BlockSpec((tm, tn), lambda i, j: (i, j))
    in_specs = [pl.BlockSpec((tm, k), lambda i, j: (i, 0)),
                pl.BlockSpec((None, k, tn), lambda i, j: (layer, 0, j)),
                _ssq_spec(tm),
                pl.BlockSpec((None, tm, kp), lambda i, j: (layer, i, 0)),
                pl.BlockSpec((None, kp, tn), lambda i, j: (layer, 0, j)),
                tile]
    args = [n, wg, ssq, p, wp, h]
    out_specs = [tile]
    out_shape = [jax.ShapeDtypeStruct((m, d), F32)]
    if kg:
        in_specs.append(pl.BlockSpec((kg, tn), lambda i, j: (0, j)))
        args.append(gains)
        out_specs += [tile] * kg + [_ssq_spec(tm)]
        out_shape += [jax.ShapeDtypeStruct((m, d), BF16)] * kg + [jax.ShapeDtypeStruct((m, LANES), F32)]
    outs = pl.pallas_call(
        functools.partial(_ple_kernel, n_gain=kg),
        grid=(m // tm, d // tn),
        in_specs=in_specs,
        out_specs=out_specs,
        out_shape=out_shape,
        compiler_params=_cparams(("arbitrary", "arbitrary")),
        name=name,
    )(*args)
    if kg:
        return outs[0], outs[1:-1], outs[-1]
    return outs[0]


A_W_COLS = 512


def _wt_cast_kernel(wt_ref, o_ref):
    o_ref[...] = wt_ref[...].T.astype(BF16)


def _cast_weight_cols(wt, first, width, name):
    k = wt.shape[1]
    cols = min(A_W_COLS, width)
    assert width % cols == 0
    return pl.pallas_call(
        _wt_cast_kernel,
        grid=(width // cols,),
        in_specs=[pl.BlockSpec((pl.Element(cols), pl.Element(k)), lambda i: (pl.multiple_of(first + i * cols, 8), 0))],
        out_specs=pl.BlockSpec((k, cols), lambda i: (0, i)),
        out_shape=jax.ShapeDtypeStruct((k, width), BF16),
        compiler_params=_cparams(("arbitrary",)),
        name=name,
    )(wt)


def split_a_weights(wa):
    wt = wa[0].T
    c_q, c_cmp, c_kv = NSA_WIDTH, NSA_WIDTH + 2 * NSA_KVW, NSA_WIDTH + 6 * NSA_KVW
    c_z = c_kv + 3 * NSA_G * NSA_HPG
    segs = [(0, c_q), (c_q, c_cmp - c_q), (c_cmp, c_kv - c_cmp), (c_kv, LANES)]
    segs += [(c_z + br * NSA_WIDTH, NSA_WIDTH) for br in range(3)]
    return [_cast_weight_cols(wt, first, width, "a_w_cast%d" % i) for i, (first, width) in enumerate(segs)]


def _compress_kernel(x_ref, pos_ref, w1_ref, w2_ref, o_ref, *, n_cmp_pad):
    nc = n_cmp_pad
    lo = jnp.zeros((nc, CMP_HIDDEN), F32)
    hi = jnp.zeros((nc, CMP_HIDDEN), F32)
    for l in range(CMP_STRIDE):
        xl = x_ref[pl.ds(l, nc, stride=CMP_STRIDE), :]
        a = (xl + pos_ref[0, l:l + 1, :]).astype(BF16)
        b = (xl + pos_ref[0, CMP_STRIDE + l:CMP_STRIDE + l + 1, :]).astype(BF16)
        lo = lo + _dot(a, w1_ref[0, l * LANES:(l + 1) * LANES, :])
        hi = hi + _dot(b, w1_ref[0, (CMP_STRIDE + l) * LANES:(CMP_STRIDE + l + 1) * LANES, :])
    hid = lo + pltpu.roll(hi, nc - 1, 0)
    out = _dot(_silu(hid).astype(BF16), w2_ref[0])
    n_idx = lax.broadcasted_iota(jnp.int32, (nc, LANES), 0)
    o_ref[0, 0, 0] = jnp.where(n_idx < nc - 1, out, 0.0).astype(BF16)


def compress_kv(kcvc, pos, w1, w2, batch, seq):
    nc = seq // CMP_STRIDE
    return pl.pallas_call(
        functools.partial(_compress_kernel, n_cmp_pad=nc),
        grid=(batch, 2, NSA_G),
        in_specs=[pl.BlockSpec((seq, LANES), lambda b, s, g: (b, s * NSA_G + g)),
                  pl.BlockSpec((1, CMP_LEN, LANES), lambda b, s, g: (s, 0, 0)),
                  pl.BlockSpec((1, CMP_LEN * LANES, CMP_HIDDEN), lambda b, s, g: (s, 0, 0)),
                  pl.BlockSpec((1, CMP_HIDDEN, LANES), lambda b, s, g: (s, 0, 0))],
        out_specs=pl.BlockSpec((1, 1, 1, nc, LANES), lambda b, s, g: (b, s, g, 0, 0)),
        out_shape=jax.ShapeDtypeStruct((batch, 2, NSA_G, nc, LANES), BF16),
        compiler_params=_cparams(("arbitrary", "arbitrary", "arbitrary")),
        name="compress_kv",
    )(kcvc, pos, w1, w2)


def _nsa_kernel(qp_ref, qr_ref, z0_ref, z1_ref, z2_ref,
                kc_ref, vc_ref, ks_ref, vs_ref, kw_ref, vw_ref, et_ref, mt_ref,
                o_ref, qa_scr, m_scr, acc_scr, mix_scr, *, seq):
    tq, tk, hpg = NSA_TQ, NSA_TK, NSA_HPG
    rows = hpg * tq
    nc = seq // CMP_STRIDE
    ns = seq // SEL_BLOCK
    n_top = min(SEL_TOPK, ns)
    nsp = mt_ref.shape[0]
    c_exp = LOG2E / math.sqrt(NSA_DH)
    t0 = pl.program_id(2) * tq
    ones_k = jnp.ones((tk, LANES), BF16)

    t_col = t0 + lax.broadcasted_iota(jnp.int32, (tq, 1), 0)

    n_idx = lax.broadcasted_iota(jnp.int32, (1, nc), 1)
    valid = (n_idx * CMP_STRIDE + (CMP_LEN - 1) <= t_col) & (n_idx < nc - 1)
    bias_c = jnp.where(valid, 0.0, NEG_BIG)
    row_ok = t0 + lax.broadcasted_iota(jnp.int32, (tq, LANES), 0) >= CMP_LEN - 1
    lane_ok = t0 + lax.broadcasted_iota(jnp.int32, (ns, tq), 1) >= CMP_LEN - 1
    kc = kc_ref[0, 0, 0]
    vc_aug = jnp.concatenate([vc_ref[0, 0, 0], jnp.ones((nc, LANES), BF16)], axis=1)
    hq = tq // 2
    span = NSA_WINDOW + hq
    win = []
    for hh in range(2):
        kw0 = pl.multiple_of(jnp.maximum(t0 + hh * hq - NSA_WINDOW, 0), hq)
        t_half = t_col[hh * hq:(hh + 1) * hq]
        kpos_w = kw0 + lax.broadcasted_iota(jnp.int32, (1, span), 1)
        win.append((kw_ref[pl.ds(kw0, span), :],
                    jnp.concatenate([vw_ref[pl.ds(kw0, span), :], jnp.ones((span, LANES), BF16)], axis=1),
                    jnp.where((kpos_w <= t_half) & (kpos_w > t_half - NSA_WINDOW), 0.0, NEG_BIG)))
    q_all = jnp.concatenate([qp_ref[:, h * LANES:(h + 1) * LANES] for h in range(hpg)], axis=0)
    sc = _dot_nt(q_all, kc) + jnp.concatenate([bias_c] * hpg, axis=0)
    m = jnp.max(sc, axis=-1, keepdims=True)
    pe = jnp.exp2((sc - m) * c_exp).astype(BF16)
    acc_c = _dot(pe, vc_aug)
    imp_t = _dot_nt(mt_ref[...], pe)
    imp_parts = []
    for h in range(hpg):
        sl = slice(h * LANES, (h + 1) * LANES)
        rs = slice(h * tq, (h + 1) * tq)
        oc = jnp.where(row_ok, acc_c[rs, 0:LANES] * (1.0 / acc_c[rs, LANES:2 * LANES]), 0.0)
        imp_parts.append(imp_t[0:ns, rs] * (1.0 / imp_t[nsp - 1:nsp, rs]))
        ows = []
        for hh, (kwin, vw_aug, bias_w) in enumerate(win):
            sc = _dot_nt(qr_ref[hh * hq:(hh + 1) * hq, sl], kwin) + bias_w
            m = jnp.max(sc, axis=-1, keepdims=True)
            pe_w = jnp.exp2((sc - m) * c_exp).astype(BF16)
            acc = _dot(pe_w, vw_aug)
            ows.append(acc[:, 0:LANES] * (1.0 / acc[:, LANES:2 * LANES]))
        ow = jnp.concatenate(ows, axis=0)
        mix_scr[:, sl] = oc * z0_ref[:, sl] + ow * z2_ref[:, sl]
    while len(imp_parts) > 1:
        imp_parts = [imp_parts[i] + imp_parts[i + 1] for i in range(0, len(imp_parts), 2)]
    imp = jnp.where(lane_ok, imp_parts[0], 0.0)

    j_io = lax.broadcasted_iota(jnp.int32, (ns, tq), 0)
    t_io = t0 + lax.broadcasted_iota(jnp.int32, (ns, tq), 1)
    dist = lax.shift_right_logical(t_io, int(math.log2(SEL_BLOCK))) - j_io
    forced = (j_io == 0) | ((dist >= 0) & (dist < SEL_LOCAL))
    imp = jnp.where(forced, FORCE_SCORE, imp)
    imp = jnp.where(dist >= 0, imp, -1.0)
    sub = 8
    ranks = []
    for v in range(ns // sub):
        blk = imp[v * sub:(v + 1) * sub, :]
        j_v = v * sub + lax.broadcasted_iota(jnp.int32, (sub, tq), 0)
        r = jnp.zeros((sub, tq), F32)
        for k in range(ns):
            rk = imp[k:k + 1, :]
            if k < v * sub:
                ahead = rk >= blk
            elif k >= (v + 1) * sub:
                ahead = rk > blk
            else:
                ahead = (rk > blk) | ((rk == blk) & (j_v > k))
            r = r + jnp.where(ahead, 1.0, 0.0)
        ranks.append(r)
    rank = jnp.concatenate(ranks, axis=0)
    not_sel_t = jnp.concatenate([jnp.where(rank < n_top, 0.0, 1.0), jnp.ones((nsp - ns, tq), F32)], axis=0)
    not_sel = not_sel_t.T.astype(BF16)

    for h in range(hpg):
        qa_scr[h * tq:(h + 1) * tq, 0:LANES] = qr_ref[:, h * LANES:(h + 1) * LANES]
        qa_scr[h * tq:(h + 1) * tq, LANES:2 * LANES] = not_sel

    m_scr[...] = jnp.full((rows, LANES), NEG_BIG, F32)
    acc_scr[...] = jnp.zeros((rows, 2 * LANES), F32)
    n_kt = lax.shift_right_logical(t0 + (tq + tk - 1), int(math.log2(tk)))

    def slc_tile(kt, causal, width=tk):
        k0 = pl.multiple_of(kt * tk, tk)
        k_aug = jnp.concatenate([ks_ref[pl.ds(k0, width), :], et_ref[pl.ds(k0, width), :]], axis=1)
        v_aug = jnp.concatenate([vs_ref[pl.ds(k0, width), :], ones_k[0:width]], axis=1)
        if causal:
            kpos = k0 + lax.broadcasted_iota(jnp.int32, (1, width), 1)
            bias = jnp.where(kpos <= t_col, 0.0, NEG_BIG)
        for h in range(hpg):
            rs = slice(h * tq, (h + 1) * tq)
            sc = _dot_nt(qa_scr[rs, :], k_aug)
            if causal:
                sc = sc + bias
            m_old = m_scr[rs, :]
            m_new = jnp.maximum(m_old, jnp.max(sc, axis=-1, keepdims=True))
            alpha = jnp.exp2((m_old - m_new) * c_exp)
            pe = jnp.exp2((sc - _lane_tile(m_new, width // LANES)) * c_exp)
            acc_scr[rs, :] = _lane_tile(alpha, 2) * acc_scr[rs, :] + _dot(pe.astype(BF16), v_aug)
            m_scr[rs, :] = m_new

    def pair_body(i, carry):
        slc_tile(2 * i, causal=False)
        slc_tile(2 * i + 1, causal=False)
        return carry

    lax.fori_loop(0, lax.shift_right_logical(n_kt - 1, 1), pair_body, 0)
    odd = ((n_kt - 1) & 1) == 1
    short = (t0 & (tk - 1)) == 0
    for odd_case in (True, False):
        for short_case in (True, False):
            @pl.when((odd if odd_case else jnp.logical_not(odd)) & (short if short_case else jnp.logical_not(short)))
            def _(odd_case=odd_case, short_case=short_case):
                if odd_case:
                    slc_tile(n_kt - 2, causal=False)
                slc_tile(n_kt - 1, causal=True, width=tq if short_case else tk)

    for h in range(hpg):
        rs = slice(h * tq, (h + 1) * tq)
        sl = slice(h * LANES, (h + 1) * LANES)
        o_slc = acc_scr[rs, 0:LANES] * (1.0 / acc_scr[rs, LANES:2 * LANES])
        o_ref[:, sl] = (mix_scr[:, sl] + o_slc * z1_ref[:, sl]).astype(BF16)


def nsa_attention(qp, qr, zg, cmp_kv, kv, e_t, cmp_to_sel_t, batch, seq):
    t = qp.shape[0]
    tq = NSA_TQ
    assert seq % NSA_TK == 0 and NSA_TK % tq == 0 and seq >= NSA_WINDOW + tq
    qt_per_seq = seq // tq
    nc = seq // CMP_STRIDE
    hw = NSA_HPG * NSA_DH

    qz_spec = pl.BlockSpec((tq, hw), lambda b, g, i: (b * qt_per_seq + i, g))
    cmp_specs = [pl.BlockSpec((1, 1, 1, nc, LANES), functools.partial(
        lambda b, g, i, s: (b, s, g, 0, 0), s=s)) for s in range(2)]
    kv_specs = [pl.BlockSpec((seq, LANES), functools.partial(
        lambda b, g, i, s: (b, s * NSA_G + g), s=s)) for s in range(4)]
    const_specs = [pl.BlockSpec(e_t.shape, lambda b, g, i: (0, 0)),
                   pl.BlockSpec(cmp_to_sel_t.shape, lambda b, g, i: (0, 0))]
    rows = NSA_HPG * tq
    return pl.pallas_call(
        functools.partial(_nsa_kernel, seq=seq),
        grid=(batch, NSA_G, qt_per_seq),
        in_specs=[qz_spec] * 5 + cmp_specs + kv_specs + const_specs,
        out_specs=qz_spec,
        out_shape=jax.ShapeDtypeStruct((t, NSA_WIDTH), BF16),
        scratch_shapes=[pltpu.VMEM((rows, 2 * LANES), BF16),
                        pltpu.VMEM((rows, LANES), F32),
                        pltpu.VMEM((rows, 2 * LANES), F32),
                        pltpu.VMEM((tq, hw), F32)],
        compiler_params=_cparams(("arbitrary", "arbitrary", "arbitrary")),
        name="nsa_attention",
    )(qp, qr, *zg, cmp_kv, cmp_kv, kv, kv, kv, kv, e_t, cmp_to_sel_t)


def _swa_kernel(sink_ref, q_ref, z_ref, *rest):
    kv_refs, o_ref = rest[:-1], rest[-1]
    tq = ATT_TQ
    n_pair = SWA_KVH // 2
    per_kv = SWA_HPG * SWA_DH // LANES
    scale = 1.0 / math.sqrt(SWA_DH)
    c_exp = LOG2E * scale
    t0 = pl.program_id(1) * tq
    ones_k = jnp.ones((2 * tq, LANES), BF16)

    def window(kind, var, pair):
        base = (kind * 4 + var) * 2
        sl = slice(pair * LANES, (pair + 1) * LANES)
        return jnp.concatenate([kv_refs[base][:, sl], kv_refs[base + 1][:, sl]], axis=0)

    t_col = t0 + lax.broadcasted_iota(jnp.int32, (tq, 1), 0)
    kpos = t0 - tq + lax.broadcasted_iota(jnp.int32, (1, 2 * tq), 1)
    ok = (kpos <= t_col) & (kpos > t_col - SWA_WINDOW) & (kpos >= 0)
    bias = jnp.where(ok, 0.0, NEG_BIG)
    col0 = lax.broadcasted_iota(jnp.int32, (tq, LANES), 1) == 0
    keep = jnp.where(lax.broadcasted_iota(jnp.int32, (2 * tq, LANES), 0) == 0, 0.0, 1.0).astype(BF16)

    for pair in range(n_pair):
        for j in range(2):
            k_var = [window(0, 2 * j + par, pair) * keep for par in range(2)]
            v_var = [jnp.concatenate([window(1, 2 * j + par, pair) * keep, ones_k], axis=1) for par in range(2)]
            for c in range(per_kv):
                col = ((pair * 2 + j) * per_kv + c) * LANES
                qc = q_ref[:, col:col + LANES]
                o_chunk = None
                for par in range(2):
                    head = (pair * 2 + j) * SWA_HPG + 2 * c + par
                    sink_raw = sink_ref[head] * math.sqrt(SWA_DH)
                    bias_h = jnp.concatenate([jnp.where(col0, sink_raw, bias[:, 0:LANES]), bias[:, LANES:]], axis=1)
                    sc = _dot_nt(qc, k_var[par]) + bias_h
                    m = jnp.max(sc, axis=-1, keepdims=True)
                    e = jnp.exp2((sc - m) * c_exp)
                    acc = _dot(e.astype(BF16), v_var[par])
                    o_par = acc[:, 0:LANES] * (1.0 / acc[:, LANES:2 * LANES])
                    o_chunk = o_par if o_chunk is None else o_chunk + o_par
                o_ref[:, col:col + LANES] = (o_chunk * z_ref[:, col:col + LANES]).astype(BF16)


def swa_attention(q, zs, kvb, sinks, batch, seq):
    t = q.shape[0]
    tq = ATT_TQ
    qt_per_seq = seq // tq
    kvw = SWA_KVH * SWA_DH

    def row(b, i):
        return b * qt_per_seq + i

    def prev_row(b, i):
        return b * qt_per_seq + jnp.maximum(i - 1, 0)

    qz_spec = pl.BlockSpec((tq, SWA_WIDTH), lambda b, i: (row(b, i), 0))
    kv_specs = []
    for cb in range(8):
        kv_specs.append(pl.BlockSpec((tq, kvw), functools.partial(lambda b, i, cb: (prev_row(b, i), cb), cb=cb)))
        kv_specs.append(pl.BlockSpec((tq, kvw), functools.partial(lambda b, i, cb: (row(b, i), cb), cb=cb)))
    return pl.pallas_call(
        _swa_kernel,
        grid=(batch, qt_per_seq),
        in_specs=[pl.BlockSpec(memory_space=pltpu.SMEM), qz_spec, qz_spec] + kv_specs,
        out_specs=qz_spec,
        out_shape=jax.ShapeDtypeStruct((t, SWA_WIDTH), BF16),
        compiler_params=_cparams(("arbitrary", "arbitrary")),
        name="swa_attention",
    )(sinks, q, zs, *([kvb] * 16))


def _selection_constants(seq):
    nc = seq // CMP_STRIDE
    ns = seq // SEL_BLOCK
    nsp = -(-ns // LANES) * LANES
    e_t = np.zeros((seq, nsp), np.float32)
    e_t[np.arange(seq), np.arange(seq) // SEL_BLOCK] = NEG_BIG
    c0 = np.arange(nc - 1) * CMP_STRIDE
    s0 = np.arange(ns) * SEL_BLOCK
    overlap = (c0[:, None] < s0[None, :] + SEL_BLOCK) & (c0[:, None] + CMP_LEN > s0[None, :])
    assert ns < nsp
    cmp_to_sel_t = np.zeros((nsp, nc), np.float32)
    cmp_to_sel_t[:ns, :nc - 1] = overlap.T
    cmp_to_sel_t[nsp - 1, :] = 1.0
    return jnp.asarray(e_t, BF16), jnp.asarray(cmp_to_sel_t, BF16)


def kernel(x, p, a_norm, a_w_in, a_w_out, a_cmp_pos_k, a_cmp_w1_k, a_cmp_w2_k, a_cmp_pos_v, a_cmp_w1_v,
           a_cmp_w2_v, kv_norm, w_kv, b_norm, b_w_in, b_w_out, b_sinks, ple_norm, ple_gate_w, ple_proj,
           final_norm):
    batch, seq, d = x.shape
    assert d == D_MODEL and p.shape[0] == 2 and a_w_in.shape[0] == 1 and b_w_in.shape[0] == 1
    t = batch * seq
    xf = x.reshape(t, d)

    n_gate = 3 * NSA_G * NSA_HPG
    w_q, w_cmp, w_kvs, w_gate, *w_z = split_a_weights(a_w_in)
    wg = w_gate[:, :n_gate].reshape(d, 3, NSA_G, NSA_HPG).transpose(0, 2, 1, 3)
    wg = wg.reshape(d, NSA_G, 3 * NSA_HPG)
    wg = jnp.pad(wg, ((0, 0), (0, 0), (0, LANES - 3 * NSA_HPG))).reshape(d, NSA_G * LANES)
    pos_kv = jnp.stack([a_cmp_pos_k[0], a_cmp_pos_v[0]])
    w1_kv = jnp.stack([a_cmp_w1_k[0], a_cmp_w1_v[0]]).astype(BF16)
    w2_kv = jnp.stack([a_cmp_w2_k[0], a_cmp_w2_v[0]]).astype(BF16)
    wb = b_w_in[0].astype(BF16)
    tabs_a = _rope_tables(seq, NSA_DH)
    tabs_b = _rope_tables(seq, SWA_DH)
    e_t, cmp_to_sel_t = _selection_constants(seq)

    (n_a,) = rms_norm_rows(xf, a_norm, BF16)
    q_rot, q_plain = matmul_rope(n_a, w_q, tabs_a, seq, NSA_DH, MM_TN, True, "a_q_proj")
    (kv_a,) = matmul_rope(n_a, w_kvs, tabs_a, seq, NSA_DH, NSA_KVW, False, "a_kv_proj")
    kcvc = matmul(n_a, w_cmp, "a_cmp_proj")
    gates = matmul(n_a, wg, "a_gate_proj", act="sigmoid")
    zg = [matmul_zgate(n_a, w_z[br], gates, br, "a_z_proj%d" % br) for br in range(3)]
    cmp_kv = compress_kv(kcvc, pos_kv, w1_kv, w2_kv, batch, seq)
    mixed = nsa_attention(q_plain, q_rot, zg, cmp_kv, kv_a, e_t, cmp_to_sel_t, batch, seq)
    h, (n_p,), ssq = matmul_residual(mixed, a_w_out[0].astype(BF16), xf, ple_norm[0:1], "a_out_proj")
    pf = p.reshape(2, t, PLE_DIM)
    ple_wg = ple_gate_w.astype(BF16)
    ple_wp = ple_proj.astype(BF16)
    h, (n_kv, n_b), ssq = ple_update(n_p, ssq, ple_wg, pf, ple_wp, 0, h, jnp.stack([kv_norm, b_norm[0]]), "ple0")

    kvb = matmul_kvb(n_kv, w_kv.astype(BF16), ssq, tabs_b, seq, "kv_proj")
    (q_b,) = matmul_rope(n_b, wb, tabs_b, seq, SWA_DH, MM_TN, False, "b_q_proj", ssq=ssq, cols=(0, SWA_WIDTH))
    zs_b = matmul(n_b, wb, "b_z_proj", act="silu", ssq=ssq, cols=(SWA_WIDTH, SWA_WIDTH))
    ob = swa_attention(q_b, zs_b, kvb, b_sinks[0], batch, seq)
    h, (n_p,), ssq = matmul_residual(ob, b_w_out[0].astype(BF16), h, ple_norm[1:2], "b_out_proj")
    h = ple_update(n_p, ssq, ple_wg, pf, ple_wp, 1, h, None, "ple1")

    (out,) = rms_norm_rows(h, final_norm[None, :], F32)
    return out.reshape(batch, seq, d)
```

```python
import functools
import math

import numpy as np
import jax
import jax.numpy as jnp
from jax import lax
from jax.experimental import pallas as pl
from jax.experimental.pallas import tpu as pltpu

F32 = jnp.float32
BF16 = jnp.bfloat16

D_MODEL = 4096
RMS_EPS = 1e-6
ROPE_THETA = 500000.0
NEG_BIG = -1e30
FORCE_SCORE = 1e9
PLE_DIM = 256

NSA_DH = 128
NSA_G = 4
NSA_HPG = 8
NSA_WIDTH = 4096
NSA_KVW = 512
CMP_LEN = 32
CMP_STRIDE = 16
CMP_HIDDEN = 256
SEL_BLOCK = 64
SEL_TOPK = 16
SEL_LOCAL = 2
NSA_WINDOW = 512

SWA_DH = 64
SWA_KVH = 8
SWA_HPG = 8
SWA_WIDTH = 4096
SWA_WINDOW = 128

SUBLANES = 8
LANES = 128
V7X_VMEM_BYTES = 64 * 1024 * 1024
VMEM_LIMIT_BYTES = V7X_VMEM_BYTES - 6 * 1024 * 1024

MM_TM = 1024
MM_TN = 1024
MM_TN_FUSED = 512
MM_ROW_CHUNKS = 8
NORM_ROWS = 512
ATT_TQ = 128
NSA_TQ = 256
NSA_TK = 512
NSA_GPS = 2

LOG2E = math.log2(math.e)


def _cparams(sem):
    return pltpu.CompilerParams(dimension_semantics=sem, vmem_limit_bytes=VMEM_LIMIT_BYTES)


def _sigmoid(v):
    return 1.0 / (1.0 + jnp.exp(-v))


def _silu(v):
    return v * _sigmoid(v)


def _dot(a, b):
    return jnp.dot(a, b, preferred_element_type=F32)


def _dot_nt(a, b):
    return lax.dot_general(a, b, (((1,), (1,)), ((), ())), preferred_element_type=F32)


def _lane_tile(x, n):
    return jnp.concatenate([x] * n, axis=1)


def _norm_kernel(h_ref, g_ref, *o_refs):
    x = h_ref[...]
    r = lax.rsqrt(jnp.mean(x * x, axis=-1, keepdims=True) + RMS_EPS)
    y = x * r
    for i, o_ref in enumerate(o_refs):
        o_ref[...] = (y * g_ref[i:i + 1, :]).astype(o_ref.dtype)


def rms_norm_rows(h, gains, out_dtype):
    t, d = h.shape
    k = gains.shape[0]
    outs = pl.pallas_call(
        _norm_kernel,
        grid=(t // NORM_ROWS,),
        in_specs=[pl.BlockSpec((NORM_ROWS, d), lambda i: (i, 0)),
                  pl.BlockSpec((k, d), lambda i: (0, 0))],
        out_specs=[pl.BlockSpec((NORM_ROWS, d), lambda i: (i, 0))] * k,
        out_shape=[jax.ShapeDtypeStruct((t, d), out_dtype)] * k,
        compiler_params=_cparams(("arbitrary",)),
        name="rms_norm",
    )(h, gains)
    return outs


def _rope_tables(seq, head_dim):
    rot = head_dim // 4
    half = rot // 2
    inv_freq = ROPE_THETA ** (-jnp.arange(half, dtype=F32) / half)
    ang = jnp.arange(seq, dtype=jnp.int32).astype(F32)[:, None] * inv_freq[None, :]
    cos, sin = jnp.cos(ang), jnp.sin(ang)
    rest = head_dim - rot
    zeros_h = jnp.zeros((seq, half), F32)
    c = jnp.concatenate([cos, cos, jnp.ones((seq, rest), F32)], axis=1)
    s1 = jnp.concatenate([-sin, zeros_h, jnp.zeros((seq, rest), F32)], axis=1)
    s2 = jnp.concatenate([zeros_h, sin, jnp.zeros((seq, rest), F32)], axis=1)
    reps = LANES // head_dim
    return tuple(jnp.tile(a, (1, reps)) for a in (c, s1, s2))


def _rope_lanes(t, c, s1, s2, half):
    return t * c + pltpu.roll(t, LANES - half, 1) * s1 + pltpu.roll(t, half, 1) * s2


def _row_chunks(ref, n=MM_ROW_CHUNKS):
    rows = ref.shape[0] // n
    return [slice(c * rows, (c + 1) * rows) for c in range(n)]


def _scaled_dot(x_ref, w_ref, s_ref, rs):
    acc = _dot(x_ref[rs, :], w_ref[...])
    if s_ref is not None:
        r = lax.rsqrt(s_ref[rs, :] * (1.0 / x_ref.shape[1]) + RMS_EPS)
        acc = acc * _lane_tile(r, acc.shape[1] // LANES)
    return acc


def _write_scaled_copies(h, rs, g_ref, hb_refs):
    for i, hb_ref in enumerate(hb_refs):
        hb_ref[rs, :] = (h * g_ref[i:i + 1, :]).astype(BF16)
    return jnp.broadcast_to(jnp.sum(h * h, axis=-1, keepdims=True), (h.shape[0], LANES))


def _accumulate_ssq(parts, ssq_ref):
    part = jnp.concatenate(parts, axis=0)
    j = pl.program_id(1)

    @pl.when(j == 0)
    def _():
        ssq_ref[...] = part

    @pl.when(j > 0)
    def _():
        ssq_ref[...] = ssq_ref[...] + part


def _mm_kernel(*refs, act, scaled):
    x_ref, w_ref = refs[0], refs[1]
    s_ref = refs[2] if scaled else None
    o_ref = refs[-1]
    for rs in _row_chunks(o_ref):
        acc = _scaled_dot(x_ref, w_ref, s_ref, rs)
        if act == "silu":
            acc = _silu(acc)
        elif act == "sigmoid":
            acc = _sigmoid(acc)
        o_ref[rs, :] = acc.astype(o_ref.dtype)


def _mm_zgate_kernel(x_ref, w_ref, g_ref, o_ref, *, first_col):
    for rs in _row_chunks(o_ref):
        acc = _silu(_dot(x_ref[rs, :], w_ref[...]))
        for h in range(acc.shape[1] // LANES):
            sl = slice(h * LANES, (h + 1) * LANES)
            o_ref[rs, sl] = acc[:, sl] * g_ref[rs, first_col + h:first_col + h + 1]


def _mm_res_kernel(x_ref, w_ref, r_ref, g_ref, h_ref, *rest):
    parts = []
    for rs in _row_chunks(h_ref):
        h = r_ref[rs, :] + _dot(x_ref[rs, :], w_ref[...])
        h_ref[rs, :] = h
        parts.append(_write_scaled_copies(h, rs, g_ref, rest[:-1]))
    _accumulate_ssq(parts, rest[-1])


def _ple_kernel(n_ref, wg_ref, s_ref, p_ref, wp_ref, h_ref, *rest, n_gain):
    o_ref = rest[1] if n_gain else rest[0]
    parts = []
    for rs in _row_chunks(o_ref):
        gate = _sigmoid(_scaled_dot(n_ref, wg_ref, s_ref, rs))
        emb = _dot(p_ref[rs, :].astype(BF16), wp_ref[...])
        h = h_ref[rs, :] + gate * emb
        o_ref[rs, :] = h
        if n_gain:
            parts.append(_write_scaled_copies(h, rs, rest[0], rest[2:-1]))
    if n_gain:
        _accumulate_ssq(parts, rest[-1])


def _mm_rope_kernel(*refs, half, n_rope, plain_too, scaled):
    x_ref, w_ref = refs[0], refs[1]
    s_ref = refs[2] if scaled else None
    c_ref, s1_ref, s2_ref = refs[3:6] if scaled else refs[2:5]
    o_refs = refs[6:] if scaled else refs[5:]
    for rs in _row_chunks(o_refs[0]):
        acc = _scaled_dot(x_ref, w_ref, s_ref, rs)
        c, s1, s2 = c_ref[rs, :], s1_ref[rs, :], s2_ref[rs, :]
        for ch in range(acc.shape[1] // LANES):
            sl = slice(ch * LANES, (ch + 1) * LANES)
            t = acc[:, sl]
            o_refs[0][rs, sl] = (_rope_lanes(t, c, s1, s2, half) if ch < n_rope else t).astype(BF16)
            if plain_too:
                o_refs[1][rs, sl] = t.astype(BF16)


def _mm_kvb_kernel(x_ref, w_ref, s_ref, c_ref, s1_ref, s2_ref, o_ref, *, half):
    kvw = SWA_KVH * SWA_DH
    n_ch = kvw // LANES
    for rs in _row_chunks(o_ref):
        acc = _scaled_dot(x_ref, w_ref, s_ref, rs)
        c, s1, s2 = c_ref[rs, :], s1_ref[rs, :], s2_ref[rs, :]
        low = lax.broadcasted_iota(jnp.int32, (acc.shape[0], LANES), 1) < SWA_DH
        for kind in range(2):
            for ch in range(n_ch):
                t = acc[:, kind * kvw + ch * LANES: kind * kvw + (ch + 1) * LANES]
                if kind == 0:
                    t = _rope_lanes(t, c, s1, s2, half)
                ts = pltpu.roll(t, SWA_DH, 1)
                variants = (jnp.where(low, t, 0.0), jnp.where(low, 0.0, ts),
                            jnp.where(low, ts, 0.0), jnp.where(low, 0.0, t))
                for var, val in enumerate(variants):
                    col = (kind * 4 + var) * kvw + ch * LANES
                    o_ref[rs, col:col + LANES] = val.astype(BF16)


def _mm_tiles(m, n, tn_max=MM_TN):
    tm = min(MM_TM, m)
    tn = min(tn_max, n)
    assert m % tm == 0 and n % tn == 0, (m, n)
    return tm, tn


def _ssq_spec(tm):
    return pl.BlockSpec((tm, LANES), lambda i, j: (i, 0))


def _w_cols(w, cols, m):
    first, n = (0, w.shape[1]) if cols is None else cols
    tm, tn = _mm_tiles(m, n)
    assert first % tn == 0
    return first // tn, n, tm, tn


def matmul(x, w, name, act=None, out_dtype=F32, ssq=None, cols=None):
    m, k = x.shape
    jb, n, tm, tn = _w_cols(w, cols, m)
    scaled = ssq is not None
    return pl.pallas_call(
        functools.partial(_mm_kernel, act=act, scaled=scaled),
        grid=(m // tm, n // tn),
        in_specs=[pl.BlockSpec((tm, k), lambda i, j: (i, 0)),
                  pl.BlockSpec((k, tn), lambda i, j: (0, jb + j))] + ([_ssq_spec(tm)] if scaled else []),
        out_specs=pl.BlockSpec((tm, tn), lambda i, j: (i, j)),
        out_shape=jax.ShapeDtypeStruct((m, n), out_dtype),
        compiler_params=_cparams(("arbitrary", "arbitrary")),
        name=name,
    )(x, w, *([ssq] if scaled else []))


def matmul_zgate(x, w, gates, branch, name):
    m, k = x.shape
    n = w.shape[1]
    tm, tn = _mm_tiles(m, n)
    assert tn == NSA_HPG * NSA_DH
    return pl.pallas_call(
        functools.partial(_mm_zgate_kernel, first_col=branch * NSA_HPG),
        grid=(m // tm, n // tn),
        in_specs=[pl.BlockSpec((tm, k), lambda i, j: (i, 0)),
                  pl.BlockSpec((k, tn), lambda i, j: (0, j)),
                  pl.BlockSpec((tm, LANES), lambda i, j: (i, j))],
        out_specs=pl.BlockSpec((tm, tn), lambda i, j: (i, j)),
        out_shape=jax.ShapeDtypeStruct((m, n), F32),
        compiler_params=_cparams(("arbitrary", "arbitrary")),
        name=name,
    )(x, w, gates)


def matmul_rope(x, w, tabs, seq, head_dim, rope_cols, plain_too, name, ssq=None, cols=None):
    m, k = x.shape
    jb, n, tm, tn = _w_cols(w, cols, m)
    assert seq % tm == 0
    per_seq = seq // tm
    scaled = ssq is not None
    tab_spec = pl.BlockSpec((tm, LANES), lambda i, j: (i % per_seq, 0))
    n_out = 2 if plain_too else 1
    outs = pl.pallas_call(
        functools.partial(_mm_rope_kernel, half=head_dim // 8, n_rope=rope_cols // LANES,
                          plain_too=plain_too, scaled=scaled),
        grid=(m // tm, n // tn),
        in_specs=[pl.BlockSpec((tm, k), lambda i, j: (i, 0)),
                  pl.BlockSpec((k, tn), lambda i, j: (0, jb + j))]
        + ([_ssq_spec(tm)] if scaled else []) + [tab_spec] * 3,
        out_specs=[pl.BlockSpec((tm, tn), lambda i, j: (i, j))] * n_out,
        out_shape=[jax.ShapeDtypeStruct((m, n), BF16)] * n_out,
        compiler_params=_cparams(("arbitrary", "arbitrary")),
        name=name,
    )(x, w, *([ssq] if scaled else []), *tabs)
    return outs


def matmul_kvb(x, w, ssq, tabs, seq, name):
    m, k = x.shape
    n = w.shape[1]
    tm = min(MM_TM, m)
    assert seq % tm == 0 and n == 2 * SWA_KVH * SWA_DH
    per_seq = seq // tm
    tab_spec = pl.BlockSpec((tm, LANES), lambda i: (i % per_seq, 0))
    return pl.pallas_call(
        functools.partial(_mm_kvb_kernel, half=SWA_DH // 8),
        grid=(m // tm,),
        in_specs=[pl.BlockSpec((tm, k), lambda i: (i, 0)),
                  pl.BlockSpec((k, n), lambda i: (0, 0)),
                  pl.BlockSpec((tm, LANES), lambda i: (i, 0))] + [tab_spec] * 3,
        out_specs=pl.BlockSpec((tm, 4 * n), lambda i: (i, 0)),
        out_shape=jax.ShapeDtypeStruct((m, 4 * n), BF16),
        compiler_params=_cparams(("arbitrary",)),
        name=name,
    )(x, w, ssq, *tabs)


def matmul_residual(x, w, res, gains, name):
    m, k = x.shape
    n = w.shape[1]
    kg = gains.shape[0]
    tm, tn = _mm_tiles(m, n, MM_TN)
    tile = pl.BlockSpec((tm, tn), lambda i, j: (i, j))
    outs = pl.pallas_call(
        _mm_res_kernel,
        grid=(m // tm, n // tn),
        in_specs=[pl.BlockSpec((tm, k), lambda i, j: (i, 0)),
                  pl.BlockSpec((k, tn), lambda i, j: (0, j)),
                  tile,
                  pl.BlockSpec((kg, tn), lambda i, j: (0, j))],
        out_specs=[tile] + [tile] * kg + [_ssq_spec(tm)],
        out_shape=[jax.ShapeDtypeStruct((m, n), F32)] + [jax.ShapeDtypeStruct((m, n), BF16)] * kg
        + [jax.ShapeDtypeStruct((m, LANES), F32)],
        compiler_params=_cparams(("arbitrary", "arbitrary")),
        name=name,
    )(x, w, res, gains)
    return outs[0], outs[1:-1], outs[-1]


def ple_update(n, ssq, wg, p, wp, layer, h, gains, name):
    m, k = n.shape
    d = wg.shape[2]
    kp = p.shape[2]
    kg = 0 if gains is None else gains.shape[0]
    tm, tn = _mm_tiles(m, d, MM_TN_FUSED if kg else MM_TN)
    tile = pl.BlockSpec((tm, tn), lambda i, j: (i, j))
    in_specs = [pl.BlockSpec((tm, k), lambda i, j: (i, 0)),
                pl.BlockSpec((None, k, tn), lambda i, j: (layer, 0, j)),
                _ssq_spec(tm),
                pl.BlockSpec((None, tm, kp), lambda i, j: (layer, i, 0)),
                pl.BlockSpec((None, kp, tn), lambda i, j: (layer, 0, j)),
                tile]
    args = [n, wg, ssq, p, wp, h]
    out_specs = [tile]
    out_shape = [jax.ShapeDtypeStruct((m, d), F32)]
    if kg:
        in_specs.append(pl.BlockSpec((kg, tn), lambda i, j: (0, j)))
        args.append(gains)
        out_specs += [tile] * kg + [_ssq_spec(tm)]
        out_shape += [jax.ShapeDtypeStruct((m, d), BF16)] * kg + [jax.ShapeDtypeStruct((m, LANES), F32)]
    outs = pl.pallas_call(
        functools.partial(_ple_kernel, n_gain=kg),
        grid=(m // tm, d // tn),
        in_specs=in_specs,
        out_specs=out_specs,
        out_shape=out_shape,
        compiler_params=_cparams(("arbitrary", "arbitrary")),
        name=name,
    )(*args)
    if kg:
        return outs[0], outs[1:-1], outs[-1]
    return outs[0]


A_W_COLS = 512


def _wt_cast_kernel(wt_ref, o_ref):
    o_ref[...] = wt_ref[...].T.astype(BF16)


def _cast_weight_cols(wt, first, width, name):
    k = wt.shape[1]
    cols = min(A_W_COLS, width)
    assert width % cols == 0
    return pl.pallas_call(
        _wt_cast_kernel,
        grid=(width // cols,),
        in_specs=[pl.BlockSpec((pl.Element(cols), pl.Element(k)), lambda i: (pl.multiple_of(first + i * cols, SUBLANES), 0))],
        out_specs=pl.BlockSpec((k, cols), lambda i: (0, i)),
        out_shape=jax.ShapeDtypeStruct((k, width), BF16),
        compiler_params=_cparams(("arbitrary",)),
        name=name,
    )(wt)


def split_a_weights(wa):
    wt = wa[0].T
    c_q, c_cmp, c_kv = NSA_WIDTH, NSA_WIDTH + 2 * NSA_KVW, NSA_WIDTH + 6 * NSA_KVW
    c_z = c_kv + 3 * NSA_G * NSA_HPG
    segs = [(0, c_q), (c_q, c_cmp - c_q), (c_cmp, c_kv - c_cmp), (c_kv, LANES)]
    segs += [(c_z + br * NSA_WIDTH, NSA_WIDTH) for br in range(3)]
    return [_cast_weight_cols(wt, first, width, "a_w_cast%d" % i) for i, (first, width) in enumerate(segs)]


def _compress_kernel(x_ref, pos_ref, w1_ref, w2_ref, o_ref, *, n_cmp_pad):
    nc = n_cmp_pad
    lo = jnp.zeros((nc, CMP_HIDDEN), F32)
    hi = jnp.zeros((nc, CMP_HIDDEN), F32)
    for l in range(CMP_STRIDE):
        xl = x_ref[pl.ds(l, nc, stride=CMP_STRIDE), :]
        a = (xl + pos_ref[0, l:l + 1, :]).astype(BF16)
        b = (xl + pos_ref[0, CMP_STRIDE + l:CMP_STRIDE + l + 1, :]).astype(BF16)
        lo = lo + _dot(a, w1_ref[0, l * LANES:(l + 1) * LANES, :])
        hi = hi + _dot(b, w1_ref[0, (CMP_STRIDE + l) * LANES:(CMP_STRIDE + l + 1) * LANES, :])
    hid = lo + pltpu.roll(hi, nc - 1, 0)
    out = _dot(_silu(hid).astype(BF16), w2_ref[0])
    n_idx = lax.broadcasted_iota(jnp.int32, (nc, LANES), 0)
    o_ref[0, 0, 0] = jnp.where(n_idx < nc - 1, out, 0.0).astype(BF16)


def compress_kv(kcvc, pos, w1, w2, batch, seq):
    nc = seq // CMP_STRIDE
    return pl.pallas_call(
        functools.partial(_compress_kernel, n_cmp_pad=nc),
        grid=(batch, 2, NSA_G),
        in_specs=[pl.BlockSpec((seq, LANES), lambda b, s, g: (b, s * NSA_G + g)),
                  pl.BlockSpec((1, CMP_LEN, LANES), lambda b, s, g: (s, 0, 0)),
                  pl.BlockSpec((1, CMP_LEN * LANES, CMP_HIDDEN), lambda b, s, g: (s, 0, 0)),
                  pl.BlockSpec((1, CMP_HIDDEN, LANES), lambda b, s, g: (s, 0, 0))],
        out_specs=pl.BlockSpec((1, 1, 1, nc, LANES), lambda b, s, g: (b, s, g, 0, 0)),
        out_shape=jax.ShapeDtypeStruct((batch, 2, NSA_G, nc, LANES), BF16),
        compiler_params=_cparams(("arbitrary", "arbitrary", "arbitrary")),
        name="compress_kv",
    )(kcvc, pos, w1, w2)


def _nsa_kernel(qp_ref, qr_ref, z0_ref, z1_ref, z2_ref,
                kc_ref, vc_ref, ks_ref, vs_ref, kw_ref, vw_ref, et_ref, mt_ref,
                o_ref, qa_scr, m_scr, acc_scr, mix_scr, *, seq):
    tq, tk, hpg, gps = NSA_TQ, NSA_TK, NSA_HPG, NSA_GPS
    rows = hpg * tq
    hw = hpg * LANES
    nc = seq // CMP_STRIDE
    ns = seq // SEL_BLOCK
    n_top = min(SEL_TOPK, ns)
    nsp = mt_ref.shape[0]
    c_exp = LOG2E / math.sqrt(NSA_DH)
    t0 = pl.program_id(2) * tq
    ones_k = jnp.ones((tk, LANES), BF16)

    t_col = t0 + lax.broadcasted_iota(jnp.int32, (tq, 1), 0)

    n_idx = lax.broadcasted_iota(jnp.int32, (1, nc), 1)
    valid = (n_idx * CMP_STRIDE + (CMP_LEN - 1) <= t_col) & (n_idx < nc - 1)
    bias_c = jnp.where(valid, 0.0, NEG_BIG)
    row_ok = t0 + lax.broadcasted_iota(jnp.int32, (tq, LANES), 0) >= CMP_LEN - 1
    lane_ok = t0 + lax.broadcasted_iota(jnp.int32, (ns, tq), 1) >= CMP_LEN - 1
    hq = tq // 2
    span = NSA_WINDOW + hq
    win = []
    for hh in range(2):
        kw0 = pl.multiple_of(jnp.maximum(t0 + hh * hq - NSA_WINDOW, 0), hq)
        t_half = t_col[hh * hq:(hh + 1) * hq]
        kpos_w = kw0 + lax.broadcasted_iota(jnp.int32, (1, span), 1)
        win.append((kw0, jnp.where((kpos_w <= t_half) & (kpos_w > t_half - NSA_WINDOW), 0.0, NEG_BIG)))
    j_io = lax.broadcasted_iota(jnp.int32, (ns, tq), 0)
    t_io = t0 + lax.broadcasted_iota(jnp.int32, (ns, tq), 1)
    dist = lax.shift_right_logical(t_io, int(math.log2(SEL_BLOCK))) - j_io
    forced = (j_io == 0) | ((dist >= 0) & (dist < SEL_LOCAL))

    for gi in range(gps):
        gl = slice(gi * LANES, (gi + 1) * LANES)
        kc = kc_ref[0, 0, gi]
        vc_aug = jnp.concatenate([vc_ref[0, 0, gi], jnp.ones((nc, LANES), BF16)], axis=1)
        q_all = jnp.concatenate([qp_ref[:, gi * hw + h * LANES:gi * hw + (h + 1) * LANES] for h in range(hpg)],
                                axis=0)
        sc = _dot_nt(q_all, kc) + jnp.concatenate([bias_c] * hpg, axis=0)
        m = jnp.max(sc, axis=-1, keepdims=True)
        pe = jnp.exp2((sc - m) * c_exp).astype(BF16)
        acc_c = _dot(pe, vc_aug)
        imp_t = _dot_nt(mt_ref[...], pe)
        imp_parts = []
        for h in range(hpg):
            sl = slice(gi * hw + h * LANES, gi * hw + (h + 1) * LANES)
            rs = slice(h * tq, (h + 1) * tq)
            oc = jnp.where(row_ok, acc_c[rs, 0:LANES] * (1.0 / acc_c[rs, LANES:2 * LANES]), 0.0)
            imp_parts.append(imp_t[0:ns, rs] * (1.0 / imp_t[nsp - 1:nsp, rs]))
            ows = []
            for hh, (kw0, bias_w) in enumerate(win):
                vw_aug = jnp.concatenate([vw_ref[pl.ds(kw0, span), gl], jnp.ones((span, LANES), BF16)], axis=1)
                sc = _dot_nt(qr_ref[hh * hq:(hh + 1) * hq, sl], kw_ref[pl.ds(kw0, span), gl]) + bias_w
                m = jnp.max(sc, axis=-1, keepdims=True)
                pe_w = jnp.exp2((sc - m) * c_exp).astype(BF16)
                acc = _dot(pe_w, vw_aug)
                ows.append(acc[:, 0:LANES] * (1.0 / acc[:, LANES:2 * LANES]))
            ow = jnp.concatenate(ows, axis=0)
            mix_scr[:, sl] = oc * z0_ref[:, sl] + ow * z2_ref[:, sl]
        while len(imp_parts) > 1:
            imp_parts = [imp_parts[i] + imp_parts[i + 1] for i in range(0, len(imp_parts), 2)]
        imp = jnp.where(lane_ok, imp_parts[0], 0.0)

        imp = jnp.where(forced, FORCE_SCORE, imp)
        imp = jnp.where(dist >= 0, imp, -1.0)
        sub = SUBLANES
        ranks = []
        for v in range(ns // sub):
            blk = imp[v * sub:(v + 1) * sub, :]
            j_v = v * sub + lax.broadcasted_iota(jnp.int32, (sub, tq), 0)
            r = jnp.zeros((sub, tq), F32)
            for k in range(ns):
                rk = imp[k:k + 1, :]
                if k < v * sub:
                    ahead = rk >= blk
                elif k >= (v + 1) * sub:
                    ahead = rk > blk
                else:
                    ahead = (rk > blk) | ((rk == blk) & (j_v > k))
                r = r + jnp.where(ahead, 1.0, 0.0)
            ranks.append(r)
        rank = jnp.concatenate(ranks, axis=0)
        not_sel_t = jnp.concatenate([jnp.where(rank < n_top, 0.0, 1.0), jnp.ones((nsp - ns, tq), F32)], axis=0)
        not_sel = not_sel_t.T.astype(BF16)

        for h in range(hpg):
            r0 = gi * rows + h * tq
            qa_scr[r0:r0 + tq, 0:LANES] = qr_ref[:, gi * hw + h * LANES:gi * hw + (h + 1) * LANES]
            qa_scr[r0:r0 + tq, LANES:2 * LANES] = not_sel

    m_scr[...] = jnp.full((gps * rows, LANES), NEG_BIG, F32)
    acc_scr[...] = jnp.zeros((gps * rows, 2 * LANES), F32)
    n_kt = lax.shift_right_logical(t0 + (tq + tk - 1), int(math.log2(tk)))

    def slc_tile(kt, causal, width=tk):
        k0 = pl.multiple_of(kt * tk, tk)
        e_tile = et_ref[pl.ds(k0, width), :]
        if causal:
            kpos = k0 + lax.broadcasted_iota(jnp.int32, (1, width), 1)
            bias = jnp.where(kpos <= t_col, 0.0, NEG_BIG)
        for gi in range(gps):
            gl = slice(gi * LANES, (gi + 1) * LANES)
            k_aug = jnp.concatenate([ks_ref[pl.ds(k0, width), gl], e_tile], axis=1)
            v_aug = jnp.concatenate([vs_ref[pl.ds(k0, width), gl], ones_k[0:width]], axis=1)
            for h in range(hpg):
                rs = slice(gi * rows + h * tq, gi * rows + (h + 1) * tq)
                sc = _dot_nt(qa_scr[rs, :], k_aug)
                if causal:
                    sc = sc + bias
                m_old = m_scr[rs, :]
                m_new = jnp.maximum(m_old, jnp.max(sc, axis=-1, keepdims=True))
                alpha = jnp.exp2((m_old - m_new) * c_exp)
                pe = jnp.exp2((sc - _lane_tile(m_new, width // LANES)) * c_exp)
                acc_scr[rs, :] = _lane_tile(alpha, 2) * acc_scr[rs, :] + _dot(pe.astype(BF16), v_aug)
                m_scr[rs, :] = m_new

    def pair_body(i, carry):
        slc_tile(2 * i, causal=False)
        slc_tile(2 * i + 1, causal=False)
        return carry

    lax.fori_loop(0, lax.shift_right_logical(n_kt - 1, 1), pair_body, 0)
    odd = ((n_kt - 1) & 1) == 1
    short = (t0 & (tk - 1)) == 0
    for odd_case in (True, False):
        for short_case in (True, False):
            @pl.when((odd if odd_case else jnp.logical_not(odd)) & (short if short_case else jnp.logical_not(short)))
            def _(odd_case=odd_case, short_case=short_case):
                if odd_case:
                    slc_tile(n_kt - 2, causal=False)
                slc_tile(n_kt - 1, causal=True, width=tq if short_case else tk)

    for gi in range(gps):
        for h in range(hpg):
            rs = slice(gi * rows + h * tq, gi * rows + (h + 1) * tq)
            sl = slice(gi * hw + h * LANES, gi * hw + (h + 1) * LANES)
            o_slc = acc_scr[rs, 0:LANES] * (1.0 / acc_scr[rs, LANES:2 * LANES])
            o_ref[:, sl] = (mix_scr[:, sl] + o_slc * z1_ref[:, sl]).astype(BF16)


def nsa_attention(qp, qr, zg, cmp_kv, kv, e_t, cmp_to_sel_t, batch, seq):
    t = qp.shape[0]
    tq, gps = NSA_TQ, NSA_GPS
    assert seq % NSA_TK == 0 and NSA_TK % tq == 0 and seq >= NSA_WINDOW + tq and NSA_G % gps == 0
    qt_per_seq = seq // tq
    nc = seq // CMP_STRIDE
    hw = gps * NSA_HPG * NSA_DH
    n_gg = NSA_G // gps

    qz_spec = pl.BlockSpec((tq, hw), lambda b, g, i: (b * qt_per_seq + i, g))
    cmp_specs = [pl.BlockSpec((1, 1, gps, nc, LANES), functools.partial(
        lambda b, g, i, s: (b, s, g, 0, 0), s=s)) for s in range(2)]
    kv_specs = [pl.BlockSpec((seq, gps * LANES), functools.partial(
        lambda b, g, i, s: (b, s * n_gg + g), s=s)) for s in range(4)]
    const_specs = [pl.BlockSpec(e_t.shape, lambda b, g, i: (0, 0)),
                   pl.BlockSpec(cmp_to_sel_t.shape, lambda b, g, i: (0, 0))]
    rows = gps * NSA_HPG * tq
    return pl.pallas_call(
        functools.partial(_nsa_kernel, seq=seq),
        grid=(batch, n_gg, qt_per_seq),
        in_specs=[qz_spec] * 5 + cmp_specs + kv_specs + const_specs,
        out_specs=qz_spec,
        out_shape=jax.ShapeDtypeStruct((t, NSA_WIDTH), BF16),
        scratch_shapes=[pltpu.VMEM((rows, 2 * LANES), BF16),
                        pltpu.VMEM((rows, LANES), F32),
                        pltpu.VMEM((rows, 2 * LANES), F32),
                        pltpu.VMEM((tq, hw), F32)],
        compiler_params=_cparams(("arbitrary", "arbitrary", "arbitrary")),
        name="nsa_attention",
    )(qp, qr, *zg, cmp_kv, cmp_kv, kv, kv, kv, kv, e_t, cmp_to_sel_t)


def _swa_kernel(sink_ref, q_ref, z_ref, *rest):
    kv_refs, o_ref = rest[:-1], rest[-1]
    tq = ATT_TQ
    n_pair = SWA_KVH // 2
    per_kv = SWA_HPG * SWA_DH // LANES
    scale = 1.0 / math.sqrt(SWA_DH)
    c_exp = LOG2E * scale
    t0 = pl.program_id(1) * tq
    ones_k = jnp.ones((2 * tq, LANES), BF16)

    def window(kind, var, pair):
        base = (kind * 4 + var) * 2
        sl = slice(pair * LANES, (pair + 1) * LANES)
        return jnp.concatenate([kv_refs[base][:, sl], kv_refs[base + 1][:, sl]], axis=0)

    t_col = t0 + lax.broadcasted_iota(jnp.int32, (tq, 1), 0)
    kpos = t0 - tq + lax.broadcasted_iota(jnp.int32, (1, 2 * tq), 1)
    ok = (kpos <= t_col) & (kpos > t_col - SWA_WINDOW) & (kpos >= 0)
    bias = jnp.where(ok, 0.0, NEG_BIG)
    col0 = lax.broadcasted_iota(jnp.int32, (tq, LANES), 1) == 0
    keep = jnp.where(lax.broadcasted_iota(jnp.int32, (2 * tq, LANES), 0) == 0, 0.0, 1.0).astype(BF16)

    for pair in range(n_pair):
        for j in range(2):
            k_var = [window(0, 2 * j + par, pair) * keep for par in range(2)]
            v_var = [jnp.concatenate([window(1, 2 * j + par, pair) * keep, ones_k], axis=1) for par in range(2)]
            for c in range(per_kv):
                col = ((pair * 2 + j) * per_kv + c) * LANES
                qc = q_ref[:, col:col + LANES]
                o_chunk = None
                for par in range(2):
                    head = (pair * 2 + j) * SWA_HPG + 2 * c + par
                    sink_raw = sink_ref[head] * math.sqrt(SWA_DH)
                    bias_h = jnp.concatenate([jnp.where(col0, sink_raw, bias[:, 0:LANES]), bias[:, LANES:]], axis=1)
                    sc = _dot_nt(qc, k_var[par]) + bias_h
                    m = jnp.max(sc, axis=-1, keepdims=True)
                    e = jnp.exp2((sc - m) * c_exp)
                    acc = _dot(e.astype(BF16), v_var[par])
                    o_par = acc[:, 0:LANES] * (1.0 / acc[:, LANES:2 * LANES])
                    o_chunk = o_par if o_chunk is None else o_chunk + o_par
                o_ref[:, col:col + LANES] = (o_chunk * z_ref[:, col:col + LANES]).astype(BF16)


def swa_attention(q, zs, kvb, sinks, batch, seq):
    t = q.shape[0]
    tq = ATT_TQ
    qt_per_seq = seq // tq
    kvw = SWA_KVH * SWA_DH

    def row(b, i):
        return b * qt_per_seq + i

    def prev_row(b, i):
        return b * qt_per_seq + jnp.maximum(i - 1, 0)

    qz_spec = pl.BlockSpec((tq, SWA_WIDTH), lambda b, i: (row(b, i), 0))
    kv_specs = []
    for cb in range(8):
        kv_specs.append(pl.BlockSpec((tq, kvw), functools.partial(lambda b, i, cb: (prev_row(b, i), cb), cb=cb)))
        kv_specs.append(pl.BlockSpec((tq, kvw), functools.partial(lambda b, i, cb: (row(b, i), cb), cb=cb)))
    return pl.pallas_call(
        _swa_kernel,
        grid=(batch, qt_per_seq),
        in_specs=[pl.BlockSpec(memory_space=pltpu.SMEM), qz_spec, qz_spec] + kv_specs,
        out_specs=qz_spec,
        out_shape=jax.ShapeDtypeStruct((t, SWA_WIDTH), BF16),
        compiler_params=_cparams(("arbitrary", "arbitrary")),
        name="swa_attention",
    )(sinks, q, zs, *([kvb] * 16))


def _selection_constants(seq):
    nc = seq // CMP_STRIDE
    ns = seq // SEL_BLOCK
    nsp = -(-ns // LANES) * LANES
    e_t = np.zeros((seq, nsp), np.float32)
    e_t[np.arange(seq), np.arange(seq) // SEL_BLOCK] = NEG_BIG
    c0 = np.arange(nc - 1) * CMP_STRIDE
    s0 = np.arange(ns) * SEL_BLOCK
    overlap = (c0[:, None] < s0[None, :] + SEL_BLOCK) & (c0[:, None] + CMP_LEN > s0[None, :])
    assert ns < nsp
    cmp_to_sel_t = np.zeros((nsp, nc), np.float32)
    cmp_to_sel_t[:ns, :nc - 1] = overlap.T
    cmp_to_sel_t[nsp - 1, :] = 1.0
    return jnp.asarray(e_t, BF16), jnp.asarray(cmp_to_sel_t, BF16)


def kernel(x, p, a_norm, a_w_in, a_w_out, a_cmp_pos_k, a_cmp_w1_k, a_cmp_w2_k, a_cmp_pos_v, a_cmp_w1_v,
           a_cmp_w2_v, kv_norm, w_kv, b_norm, b_w_in, b_w_out, b_sinks, ple_norm, ple_gate_w, ple_proj,
           final_norm):
    batch, seq, d = x.shape
    assert d == D_MODEL and p.shape[0] == 2 and a_w_in.shape[0] == 1 and b_w_in.shape[0] == 1
    t = batch * seq
    xf = x.reshape(t, d)

    n_gate = 3 * NSA_G * NSA_HPG
    w_q, w_cmp, w_kvs, w_gate, *w_z = split_a_weights(a_w_in)
    wg = w_gate[:, :n_gate].reshape(d, 3, NSA_G, NSA_HPG).transpose(0, 2, 1, 3)
    wg = wg.reshape(d, NSA_G, 3 * NSA_HPG)
    wg = jnp.pad(wg, ((0, 0), (0, 0), (0, LANES - 3 * NSA_HPG))).reshape(d, NSA_G * LANES)
    pos_kv = jnp.stack([a_cmp_pos_k[0], a_cmp_pos_v[0]])
    w1_kv = jnp.stack([a_cmp_w1_k[0], a_cmp_w1_v[0]]).astype(BF16)
    w2_kv = jnp.stack([a_cmp_w2_k[0], a_cmp_w2_v[0]]).astype(BF16)
    wb = b_w_in[0].astype(BF16)
    tabs_a = _rope_tables(seq, NSA_DH)
    tabs_b = _rope_tables(seq, SWA_DH)
    e_t, cmp_to_sel_t = _selection_constants(seq)

    (n_a,) = rms_norm_rows(xf, a_norm, BF16)
    q_rot, q_plain = matmul_rope(n_a, w_q, tabs_a, seq, NSA_DH, MM_TN, True, "a_q_proj")
    (kv_a,) = matmul_rope(n_a, w_kvs, tabs_a, seq, NSA_DH, NSA_KVW, False, "a_kv_proj")
    kcvc = matmul(n_a, w_cmp, "a_cmp_proj")
    gates = matmul(n_a, wg, "a_gate_proj", act="sigmoid")
    zg = [matmul_zgate(n_a, w_z[br], gates, br, "a_z_proj%d" % br) for br in range(3)]
    cmp_kv = compress_kv(kcvc, pos_kv, w1_kv, w2_kv, batch, seq)
    mixed = nsa_attention(q_plain, q_rot, zg, cmp_kv, kv_a, e_t, cmp_to_sel_t, batch, seq)
    h, (n_p,), ssq = matmul_residual(mixed, a_w_out[0].astype(BF16), xf, ple_norm[0:1], "a_out_proj")
    pf = p.reshape(2, t, PLE_DIM)
    ple_wg = ple_gate_w.astype(BF16)
    ple_wp = ple_proj.astype(BF16)
    h, (n_kv, n_b), ssq = ple_update(n_p, ssq, ple_wg, pf, ple_wp, 0, h, jnp.stack([kv_norm, b_norm[0]]), "ple0")

    kvb = matmul_kvb(n_kv, w_kv.astype(BF16), ssq, tabs_b, seq, "kv_proj")
    (q_b,) = matmul_rope(n_b, wb, tabs_b, seq, SWA_DH, MM_TN, False, "b_q_proj", ssq=ssq, cols=(0, SWA_WIDTH))
    zs_b = matmul(n_b, wb, "b_z_proj", act="silu", ssq=ssq, cols=(SWA_WIDTH, SWA_WIDTH))
    ob = swa_attention(q_b, zs_b, kvb, b_sinks[0], batch, seq)
    h, (n_p,), ssq = matmul_residual(ob, b_w_out[0].astype(BF16), h, ple_norm[1:2], "b_out_proj")
    h = ple_update(n_p, ssq, ple_wg, pf, ple_wp, 1, h, None, "ple1")

    (out,) = rms_norm_rows(h, final_norm[None, :], F32)
    return out.reshape(batch, seq, d)
```

```python
import functools
import math

import numpy as np
import jax
import jax.numpy as jnp
from jax import lax
from jax.experimental import pallas as pl
from jax.experimental.pallas import tpu as pltpu

F32 = jnp.float32
BF16 = jnp.bfloat16

D_MODEL = 4096
RMS_EPS = 1e-6
ROPE_THETA = 500000.0
NEG_BIG = -1e30
FORCE_SCORE = 1e9
PLE_DIM = 256

NSA_DH = 128
NSA_G = 4
NSA_HPG = 8
NSA_WIDTH = 4096
NSA_KVW = 512
CMP_LEN = 32
CMP_STRIDE = 16
CMP_HIDDEN = 256
SEL_BLOCK = 64
SEL_TOPK = 16
SEL_LOCAL = 2
NSA_WINDOW = 512

SWA_DH = 64
SWA_KVH = 8
SWA_HPG = 8
SWA_WIDTH = 4096
SWA_WINDOW = 128

SUBLANES = 8
LANES = 128
V7X_VMEM_BYTES = 64 * 1024 * 1024
VMEM_LIMIT_BYTES = V7X_VMEM_BYTES - 6 * 1024 * 1024

MM_TM = 1024
MM_TN = 1024
MM_ROW_CHUNKS = 8
NORM_ROWS = 512
ATT_TQ = 128
NSA_TQ = 256
NSA_TK = 512
NSA_GPS = 2

LOG2E = math.log2(math.e)


def _cparams(sem):
    return pltpu.CompilerParams(dimension_semantics=sem, vmem_limit_bytes=VMEM_LIMIT_BYTES)


def _sigmoid(v):
    return 1.0 / (1.0 + jnp.exp(-v))


def _silu(v):
    return v * _sigmoid(v)


def _dot(a, b):
    return jnp.dot(a, b, preferred_element_type=F32)


def _dot_nt(a, b):
    return lax.dot_general(a, b, (((1,), (1,)), ((), ())), preferred_element_type=F32)


def _lane_tile(x, n):
    return jnp.concatenate([x] * n, axis=1)


def _norm_kernel(h_ref, g_ref, *o_refs):
    x = h_ref[...]
    r = lax.rsqrt(jnp.mean(x * x, axis=-1, keepdims=True) + RMS_EPS)
    y = x * r
    for i, o_ref in enumerate(o_refs):
        o_ref[...] = (y * g_ref[i:i + 1, :]).astype(o_ref.dtype)


def rms_norm_rows(h, gains, out_dtype):
    t, d = h.shape
    k = gains.shape[0]
    outs = pl.pallas_call(
        _norm_kernel,
        grid=(t // NORM_ROWS,),
        in_specs=[pl.BlockSpec((NORM_ROWS, d), lambda i: (i, 0)),
                  pl.BlockSpec((k, d), lambda i: (0, 0))],
        out_specs=[pl.BlockSpec((NORM_ROWS, d), lambda i: (i, 0))] * k,
        out_shape=[jax.ShapeDtypeStruct((t, d), out_dtype)] * k,
        compiler_params=_cparams(("arbitrary",)),
        name="rms_norm",
    )(h, gains)
    return outs


def _rope_tables(seq, head_dim):
    rot = head_dim // 4
    half = rot // 2
    inv_freq = ROPE_THETA ** (-jnp.arange(half, dtype=F32) / half)
    ang = jnp.arange(seq, dtype=jnp.int32).astype(F32)[:, None] * inv_freq[None, :]
    cos, sin = jnp.cos(ang), jnp.sin(ang)
    rest = head_dim - rot
    zeros_h = jnp.zeros((seq, half), F32)
    c = jnp.concatenate([cos, cos, jnp.ones((seq, rest), F32)], axis=1)
    s1 = jnp.concatenate([-sin, zeros_h, jnp.zeros((seq, rest), F32)], axis=1)
    s2 = jnp.concatenate([zeros_h, sin, jnp.zeros((seq, rest), F32)], axis=1)
    reps = LANES // head_dim
    return tuple(jnp.tile(a, (1, reps)) for a in (c, s1, s2))


def _rope_lanes(t, c, s1, s2, half):
    return t * c + pltpu.roll(t, LANES - half, 1) * s1 + pltpu.roll(t, half, 1) * s2


def _row_chunks(ref, n=MM_ROW_CHUNKS):
    rows = ref.shape[0] // n
    return [slice(c * rows, (c + 1) * rows) for c in range(n)]


def _scaled_dot(x_ref, w_ref, s_ref, rs):
    acc = _dot(x_ref[rs, :], w_ref[...])
    if s_ref is not None:
        r = lax.rsqrt(s_ref[rs, :] * (1.0 / x_ref.shape[1]) + RMS_EPS)
        acc = acc * _lane_tile(r, acc.shape[1] // LANES)
    return acc


def _write_scaled_copies(h, rs, g_ref, hb_refs):
    for i, hb_ref in enumerate(hb_refs):
        hb_ref[rs, :] = (h * g_ref[i:i + 1, :]).astype(BF16)
    return jnp.broadcast_to(jnp.sum(h * h, axis=-1, keepdims=True), (h.shape[0], LANES))


def _accumulate_ssq(parts, ssq_ref):
    part = jnp.concatenate(parts, axis=0)
    j = pl.program_id(1)

    @pl.when(j == 0)
    def _():
        ssq_ref[...] = part

    @pl.when(j > 0)
    def _():
        ssq_ref[...] = ssq_ref[...] + part


def _mm_kernel(*refs, act, scaled):
    x_ref, w_ref = refs[0], refs[1]
    s_ref = refs[2] if scaled else None
    o_ref = refs[-1]
    for rs in _row_chunks(o_ref):
        acc = _scaled_dot(x_ref, w_ref, s_ref, rs)
        if act == "silu":
            acc = _silu(acc)
        elif act == "sigmoid":
            acc = _sigmoid(acc)
        o_ref[rs, :] = acc.astype(o_ref.dtype)


def _mm_zgate_kernel(x_ref, w_ref, g_ref, o_ref, *, first_col):
    for rs in _row_chunks(o_ref):
        acc = _silu(_dot(x_ref[rs, :], w_ref[...]))
        for h in range(acc.shape[1] // LANES):
            sl = slice(h * LANES, (h + 1) * LANES)
            o_ref[rs, sl] = acc[:, sl] * g_ref[rs, first_col + h:first_col + h + 1]


def _mm_res_kernel(x_ref, w_ref, r_ref, g_ref, h_ref, *rest):
    parts = []
    for rs in _row_chunks(h_ref):
        h = r_ref[rs, :] + _dot(x_ref[rs, :], w_ref[...])
        h_ref[rs, :] = h
        parts.append(_write_scaled_copies(h, rs, g_ref, rest[:-1]))
    _accumulate_ssq(parts, rest[-1])


def _ple_kernel(n_ref, wg_ref, s_ref, p_ref, wp_ref, h_ref, *rest, n_gain):
    o_ref = rest[1] if n_gain else rest[0]
    parts = []
    for rs in _row_chunks(o_ref):
        gate = _sigmoid(_scaled_dot(n_ref, wg_ref, s_ref, rs))
        emb = _dot(p_ref[rs, :].astype(BF16), wp_ref[...])
        h = h_ref[rs, :] + gate * emb
        o_ref[rs, :] = h
        if n_gain:
            parts.append(_write_scaled_copies(h, rs, rest[0], rest[2:-1]))
    if n_gain:
        _accumulate_ssq(parts, rest[-1])


def _mm_rope_kernel(*refs, half, n_rope, plain_too, scaled):
    x_ref, w_ref = refs[0], refs[1]
    s_ref = refs[2] if scaled else None
    c_ref, s1_ref, s2_ref = refs[3:6] if scaled else refs[2:5]
    o_refs = refs[6:] if scaled else refs[5:]
    for rs in _row_chunks(o_refs[0]):
        acc = _scaled_dot(x_ref, w_ref, s_ref, rs)
        c, s1, s2 = c_ref[rs, :], s1_ref[rs, :], s2_ref[rs, :]
        for ch in range(acc.shape[1] // LANES):
            sl = slice(ch * LANES, (ch + 1) * LANES)
            t = acc[:, sl]
            o_refs[0][rs, sl] = (_rope_lanes(t, c, s1, s2, half) if ch < n_rope else t).astype(BF16)
            if plain_too:
                o_refs[1][rs, sl] = t.astype(BF16)


def _mm_kvb_kernel(x_ref, w_ref, s_ref, c_ref, s1_ref, s2_ref, o_ref, *, half):
    kvw = SWA_KVH * SWA_DH
    n_ch = kvw // LANES
    for rs in _row_chunks(o_ref):
        acc = _scaled_dot(x_ref, w_ref, s_ref, rs)
        c, s1, s2 = c_ref[rs, :], s1_ref[rs, :], s2_ref[rs, :]
        low = lax.broadcasted_iota(jnp.int32, (acc.shape[0], LANES), 1) < SWA_DH
        for kind in range(2):
            for ch in range(n_ch):
                t = acc[:, kind * kvw + ch * LANES: kind * kvw + (ch + 1) * LANES]
                if kind == 0:
                    t = _rope_lanes(t, c, s1, s2, half)
                ts = pltpu.roll(t, SWA_DH, 1)
                variants = (jnp.where(low, t, 0.0), jnp.where(low, 0.0, ts),
                            jnp.where(low, ts, 0.0), jnp.where(low, 0.0, t))
                for var, val in enumerate(variants):
                    col = (kind * 4 + var) * kvw + ch * LANES
                    o_ref[rs, col:col + LANES] = val.astype(BF16)


def _mm_tiles(m, n, tn_max=MM_TN):
    tm = min(MM_TM, m)
    tn = min(tn_max, n)
    assert m % tm == 0 and n % tn == 0, (m, n)
    return tm, tn


def _ssq_spec(tm):
    return pl.BlockSpec((tm, LANES), lambda i, j: (i, 0))


def _w_cols(w, cols, m):
    first, n = (0, w.shape[1]) if cols is None else cols
    tm, tn = _mm_tiles(m, n)
    assert first % tn == 0
    return first // tn, n, tm, tn


def matmul(x, w, name, act=None, out_dtype=F32, ssq=None, cols=None):
    m, k = x.shape
    jb, n, tm, tn = _w_cols(w, cols, m)
    scaled = ssq is not None
    return pl.pallas_call(
        functools.partial(_mm_kernel, act=act, scaled=scaled),
        grid=(m // tm, n // tn),
        in_specs=[pl.BlockSpec((tm, k), lambda i, j: (i, 0)),
                  pl.BlockSpec((k, tn), lambda i, j: (0, jb + j))] + ([_ssq_spec(tm)] if scaled else []),
        out_specs=pl.BlockSpec((tm, tn), lambda i, j: (i, j)),
        out_shape=jax.ShapeDtypeStruct((m, n), out_dtype),
        compiler_params=_cparams(("arbitrary", "arbitrary")),
        name=name,
    )(x, w, *([ssq] if scaled else []))


def matmul_zgate(x, w, gates, branch, name):
    m, k = x.shape
    n = w.shape[1]
    tm, tn = _mm_tiles(m, n)
    assert tn == NSA_HPG * NSA_DH
    return pl.pallas_call(
        functools.partial(_mm_zgate_kernel, first_col=branch * NSA_HPG),
        grid=(m // tm, n // tn),
        in_specs=[pl.BlockSpec((tm, k), lambda i, j: (i, 0)),
                  pl.BlockSpec((k, tn), lambda i, j: (0, j)),
                  pl.BlockSpec((tm, LANES), lambda i, j: (i, j))],
        out_specs=pl.BlockSpec((tm, tn), lambda i, j: (i, j)),
        out_shape=jax.ShapeDtypeStruct((m, n), F32),
        compiler_params=_cparams(("arbitrary", "arbitrary")),
        name=name,
    )(x, w, gates)


def matmul_rope(x, w, tabs, seq, head_dim, rope_cols, plain_too, name, ssq=None, cols=None):
    m, k = x.shape
    jb, n, tm, tn = _w_cols(w, cols, m)
    assert seq % tm == 0
    per_seq = seq // tm
    scaled = ssq is not None
    tab_spec = pl.BlockSpec((tm, LANES), lambda i, j: (i % per_seq, 0))
    n_out = 2 if plain_too else 1
    outs = pl.pallas_call(
        functools.partial(_mm_rope_kernel, half=head_dim // 8, n_rope=rope_cols // LANES,
                          plain_too=plain_too, scaled=scaled),
        grid=(m // tm, n // tn),
        in_specs=[pl.BlockSpec((tm, k), lambda i, j: (i, 0)),
                  pl.BlockSpec((k, tn), lambda i, j: (0, jb + j))]
        + ([_ssq_spec(tm)] if scaled else []) + [tab_spec] * 3,
        out_specs=[pl.BlockSpec((tm, tn), lambda i, j: (i, j))] * n_out,
        out_shape=[jax.ShapeDtypeStruct((m, n), BF16)] * n_out,
        compiler_params=_cparams(("arbitrary", "arbitrary")),
        name=name,
    )(x, w, *([ssq] if scaled else []), *tabs)
    return outs


def matmul_kvb(x, w, ssq, tabs, seq, name):
    m, k = x.shape
    n = w.shape[1]
    tm = min(MM_TM, m)
    assert seq % tm == 0 and n == 2 * SWA_KVH * SWA_DH
    per_seq = seq // tm
    tab_spec = pl.BlockSpec((tm, LANES), lambda i: (i % per_seq, 0))
    return pl.pallas_call(
        functools.partial(_mm_kvb_kernel, half=SWA_DH // 8),
        grid=(m // tm,),
        in_specs=[pl.BlockSpec((tm, k), lambda i: (i, 0)),
                  pl.BlockSpec((k, n), lambda i: (0, 0)),
                  pl.BlockSpec((tm, LANES), lambda i: (i, 0))] + [tab_spec] * 3,
        out_specs=pl.BlockSpec((tm, 4 * n), lambda i: (i, 0)),
        out_shape=jax.ShapeDtypeStruct((m, 4 * n), BF16),
        compiler_params=_cparams(("arbitrary",)),
        name=name,
    )(x, w, ssq, *tabs)


def matmul_residual(x, w, res, gains, name):
    m, k = x.shape
    n = w.shape[1]
    kg = gains.shape[0]
    tm, tn = _mm_tiles(m, n, MM_TN)
    tile = pl.BlockSpec((tm, tn), lambda i, j: (i, j))
    outs = pl.pallas_call(
        _mm_res_kernel,
        grid=(m // tm, n // tn),
        in_specs=[pl.BlockSpec((tm, k), lambda i, j: (i, 0)),
                  pl.BlockSpec((k, tn), lambda i, j: (0, j)),
                  tile,
                  pl.BlockSpec((kg, tn), lambda i, j: (0, j))],
        out_specs=[tile] + [tile] * kg + [_ssq_spec(tm)],
        out_shape=[jax.ShapeDtypeStruct((m, n), F32)] + [jax.ShapeDtypeStruct((m, n), BF16)] * kg
        + [jax.ShapeDtypeStruct((m, LANES), F32)],
        compiler_params=_cparams(("arbitrary", "arbitrary")),
        name=name,
    )(x, w, res, gains)
    return outs[0], outs[1:-1], outs[-1]


def ple_update(n, ssq, wg, p, wp, layer, h, gains, name):
    m, k = n.shape
    d = wg.shape[2]
    kp = p.shape[2]
    kg = 0 if gains is None else gains.shape[0]
    tm, tn = _mm_tiles(m, d, MM_TN)
    tile = pl.BlockSpec((tm, tn), lambda i, j: (i, j))
    x_mode = dict(pipeline_mode=pl.Buffered(1)) if kg else {}
    in_specs = [pl.BlockSpec((tm, k), lambda i, j: (i, 0), **x_mode),
                pl.BlockSpec((None, k, tn), lambda i, j: (layer, 0, j)),
                _ssq_spec(tm),
                pl.BlockSpec((None, tm, kp), lambda i, j: (layer, i, 0)),
                pl.BlockSpec((None, kp, tn), lambda i, j: (layer, 0, j)),
                tile]
    args = [n, wg, ssq, p, wp, h]
    out_specs = [tile]
    out_shape = [jax.ShapeDtypeStruct((m, d), F32)]
    if kg:
        in_specs.append(pl.BlockSpec((kg, tn), lambda i, j: (0, j)))
        args.append(gains)
        out_specs += [tile] * kg + [_ssq_spec(tm)]
        out_shape += [jax.ShapeDtypeStruct((m, d), BF16)] * kg + [jax.ShapeDtypeStruct((m, LANES), F32)]
    outs = pl.pallas_call(
        functools.partial(_ple_kernel, n_gain=kg),
        grid=(m // tm, d // tn),
        in_specs=in_specs,
        out_specs=out_specs,
        out_shape=out_shape,
        compiler_params=_cparams(("arbitrary", "arbitrary")),
        name=name,
    )(*args)
    if kg:
        return outs[0], outs[1:-1], outs[-1]
    return outs[0]


A_W_COLS = 512


def _wt_cast_kernel(wt_ref, o_ref):
    o_ref[...] = wt_ref[...].T.astype(BF16)


def _cast_weight_cols(wt, first, width, name):
    k = wt.shape[1]
    cols = min(A_W_COLS, width)
    assert width % cols == 0
    return pl.pallas_call(
        _wt_cast_kernel,
        grid=(width // cols,),
        in_specs=[pl.BlockSpec((pl.Element(cols), pl.Element(k)), lambda i: (pl.multiple_of(first + i * cols, SUBLANES), 0))],
        out_specs=pl.BlockSpec((k, cols), lambda i: (0, i)),
        out_shape=jax.ShapeDtypeStruct((k, width), BF16),
        compiler_params=_cparams(("arbitrary",)),
        name=name,
    )(wt)


def split_a_weights(wa):
    wt = wa[0].T
    c_q, c_cmp, c_kv = NSA_WIDTH, NSA_WIDTH + 2 * NSA_KVW, NSA_WIDTH + 6 * NSA_KVW
    c_z = c_kv + 3 * NSA_G * NSA_HPG
    segs = [(0, c_q), (c_q, c_cmp - c_q), (c_cmp, c_kv - c_cmp), (c_kv, LANES)]
    segs += [(c_z + br * NSA_WIDTH, NSA_WIDTH) for br in range(3)]
    return [_cast_weight_cols(wt, first, width, "a_w_cast%d" % i) for i, (first, width) in enumerate(segs)]


def _compress_kernel(x_ref, pos_ref, w1_ref, w2_ref, o_ref, *, n_cmp_pad):
    nc = n_cmp_pad
    lo = jnp.zeros((nc, CMP_HIDDEN), F32)
    hi = jnp.zeros((nc, CMP_HIDDEN), F32)
    for l in range(CMP_STRIDE):
        xl = x_ref[pl.ds(l, nc, stride=CMP_STRIDE), :]
        a = (xl + pos_ref[0, l:l + 1, :]).astype(BF16)
        b = (xl + pos_ref[0, CMP_STRIDE + l:CMP_STRIDE + l + 1, :]).astype(BF16)
        lo = lo + _dot(a, w1_ref[0, l * LANES:(l + 1) * LANES, :])
        hi = hi + _dot(b, w1_ref[0, (CMP_STRIDE + l) * LANES:(CMP_STRIDE + l + 1) * LANES, :])
    hid = lo + pltpu.roll(hi, nc - 1, 0)
    out = _dot(_silu(hid).astype(BF16), w2_ref[0])
    n_idx = lax.broadcasted_iota(jnp.int32, (nc, LANES), 0)
    o_ref[0, 0, 0] = jnp.where(n_idx < nc - 1, out, 0.0).astype(BF16)


def compress_kv(kcvc, pos, w1, w2, batch, seq):
    nc = seq // CMP_STRIDE
    return pl.pallas_call(
        functools.partial(_compress_kernel, n_cmp_pad=nc),
        grid=(batch, 2, NSA_G),
        in_specs=[pl.BlockSpec((seq, LANES), lambda b, s, g: (b, s * NSA_G + g)),
                  pl.BlockSpec((1, CMP_LEN, LANES), lambda b, s, g: (s, 0, 0)),
                  pl.BlockSpec((1, CMP_LEN * LANES, CMP_HIDDEN), lambda b, s, g: (s, 0, 0)),
                  pl.BlockSpec((1, CMP_HIDDEN, LANES), lambda b, s, g: (s, 0, 0))],
        out_specs=pl.BlockSpec((1, 1, 1, nc, LANES), lambda b, s, g: (b, s, g, 0, 0)),
        out_shape=jax.ShapeDtypeStruct((batch, 2, NSA_G, nc, LANES), BF16),
        compiler_params=_cparams(("arbitrary", "arbitrary", "arbitrary")),
        name="compress_kv",
    )(kcvc, pos, w1, w2)


def _nsa_kernel(qp_ref, qr_ref, z0_ref, z1_ref, z2_ref,
                kc_ref, vc_ref, ks_ref, vs_ref, kw_ref, vw_ref, et_ref, mt_ref,
                o_ref, qa_scr, m_scr, acc_scr, mix_scr, *, seq):
    tq, tk, hpg, gps = NSA_TQ, NSA_TK, NSA_HPG, NSA_GPS
    rows = hpg * tq
    hw = hpg * LANES
    nc = seq // CMP_STRIDE
    ns = seq // SEL_BLOCK
    n_top = min(SEL_TOPK, ns)
    nsp = mt_ref.shape[0]
    c_exp = LOG2E / math.sqrt(NSA_DH)
    t0 = pl.program_id(2) * tq
    ones_k = jnp.ones((tk, LANES), BF16)

    t_col = t0 + lax.broadcasted_iota(jnp.int32, (tq, 1), 0)

    n_idx = lax.broadcasted_iota(jnp.int32, (1, nc), 1)
    valid = (n_idx * CMP_STRIDE + (CMP_LEN - 1) <= t_col) & (n_idx < nc - 1)
    bias_c = jnp.where(valid, 0.0, NEG_BIG)
    row_ok = t0 + lax.broadcasted_iota(jnp.int32, (tq, LANES), 0) >= CMP_LEN - 1
    lane_ok = t0 + lax.broadcasted_iota(jnp.int32, (ns, tq), 1) >= CMP_LEN - 1
    hq = tq // 2
    span = NSA_WINDOW + hq
    win = []
    for hh in range(2):
        kw0 = pl.multiple_of(jnp.maximum(t0 + hh * hq - NSA_WINDOW, 0), hq)
        t_half = t_col[hh * hq:(hh + 1) * hq]
        kpos_w = kw0 + lax.broadcasted_iota(jnp.int32, (1, span), 1)
        win.append((kw0, jnp.where((kpos_w <= t_half) & (kpos_w > t_half - NSA_WINDOW), 0.0, NEG_BIG)))
    j_io = lax.broadcasted_iota(jnp.int32, (ns, tq), 0)
    t_io = t0 + lax.broadcasted_iota(jnp.int32, (ns, tq), 1)
    dist = lax.shift_right_logical(t_io, int(math.log2(SEL_BLOCK))) - j_io
    forced = (j_io == 0) | ((dist >= 0) & (dist < SEL_LOCAL))

    for gi in range(gps):
        gl = slice(gi * LANES, (gi + 1) * LANES)
        kc = kc_ref[0, 0, gi]
        vc_aug = jnp.concatenate([vc_ref[0, 0, gi], jnp.ones((nc, LANES), BF16)], axis=1)
        q_all = jnp.concatenate([qp_ref[:, gi * hw + h * LANES:gi * hw + (h + 1) * LANES] for h in range(hpg)],
                                axis=0)
        sc = _dot_nt(q_all, kc) + jnp.concatenate([bias_c] * hpg, axis=0)
        m = jnp.max(sc, axis=-1, keepdims=True)
        pe = jnp.exp2((sc - m) * c_exp).astype(BF16)
        acc_c = _dot(pe, vc_aug)
        imp_t = _dot_nt(mt_ref[...], pe)
        imp_parts = []
        for h in range(hpg):
            sl = slice(gi * hw + h * LANES, gi * hw + (h + 1) * LANES)
            rs = slice(h * tq, (h + 1) * tq)
            oc = jnp.where(row_ok, acc_c[rs, 0:LANES] * (1.0 / acc_c[rs, LANES:2 * LANES]), 0.0)
            imp_parts.append(imp_t[0:ns, rs] * (1.0 / imp_t[nsp - 1:nsp, rs]))
            ows = []
            for hh, (kw0, bias_w) in enumerate(win):
                vw_aug = jnp.concatenate([vw_ref[pl.ds(kw0, span), gl], jnp.ones((span, LANES), BF16)], axis=1)
                sc = _dot_nt(qr_ref[hh * hq:(hh + 1) * hq, sl], kw_ref[pl.ds(kw0, span), gl]) + bias_w
                m = jnp.max(sc, axis=-1, keepdims=True)
                pe_w = jnp.exp2((sc - m) * c_exp).astype(BF16)
                acc = _dot(pe_w, vw_aug)
                ows.append(acc[:, 0:LANES] * (1.0 / acc[:, LANES:2 * LANES]))
            ow = jnp.concatenate(ows, axis=0)
            mix_scr[:, sl] = oc * z0_ref[:, sl] + ow * z2_ref[:, sl]
        while len(imp_parts) > 1:
            imp_parts = [imp_parts[i] + imp_parts[i + 1] for i in range(0, len(imp_parts), 2)]
        imp = jnp.where(lane_ok, imp_parts[0], 0.0)

        imp = jnp.where(forced, FORCE_SCORE, imp)
        imp = jnp.where(dist >= 0, imp, -1.0)
        sub = SUBLANES
        ranks = []
        for v in range(ns // sub):
            blk = imp[v * sub:(v + 1) * sub, :]
            j_v = v * sub + lax.broadcasted_iota(jnp.int32, (sub, tq), 0)
            r = jnp.zeros((sub, tq), F32)
            for k in range(ns):
                rk = imp[k:k + 1, :]
                if k < v * sub:
                    ahead = rk >= blk
                elif k >= (v + 1) * sub:
                    ahead = rk > blk
                else:
                    ahead = (rk > blk) | ((rk == blk) & (j_v > k))
                r = r + jnp.where(ahead, 1.0, 0.0)
            ranks.append(r)
        rank = jnp.concatenate(ranks, axis=0)
        not_sel_t = jnp.concatenate([jnp.where(rank < n_top, 0.0, 1.0), jnp.ones((nsp - ns, tq), F32)], axis=0)
        not_sel = not_sel_t.T.astype(BF16)

        for h in range(hpg):
            r0 = gi * rows + h * tq
            qa_scr[r0:r0 + tq, 0:LANES] = qr_ref[:, gi * hw + h * LANES:gi * hw + (h + 1) * LANES]
            qa_scr[r0:r0 + tq, LANES:2 * LANES] = not_sel

    m_scr[...] = jnp.full((gps * rows, LANES), NEG_BIG, F32)
    acc_scr[...] = jnp.zeros((gps * rows, 2 * LANES), F32)
    n_kt = lax.shift_right_logical(t0 + (tq + tk - 1), int(math.log2(tk)))

    def slc_tile(kt, causal, width=tk):
        k0 = pl.multiple_of(kt * tk, tk)
        e_tile = et_ref[pl.ds(k0, width), :]
        if causal:
            kpos = k0 + lax.broadcasted_iota(jnp.int32, (1, width), 1)
            bias = jnp.where(kpos <= t_col, 0.0, NEG_BIG)
        for gi in range(gps):
            gl = slice(gi * LANES, (gi + 1) * LANES)
            k_aug = jnp.concatenate([ks_ref[pl.ds(k0, width), gl], e_tile], axis=1)
            v_aug = jnp.concatenate([vs_ref[pl.ds(k0, width), gl], ones_k[0:width]], axis=1)
            for h in range(hpg):
                rs = slice(gi * rows + h * tq, gi * rows + (h + 1) * tq)
                sc = _dot_nt(qa_scr[rs, :], k_aug)
                if causal:
                    sc = sc + bias
                m_old = m_scr[rs, :]
                m_new = jnp.maximum(m_old, jnp.max(sc, axis=-1, keepdims=True))
                alpha = jnp.exp2((m_old - m_new) * c_exp)
                pe = jnp.exp2((sc - _lane_tile(m_new, width // LANES)) * c_exp)
                acc_scr[rs, :] = _lane_tile(alpha, 2) * acc_scr[rs, :] + _dot(pe.astype(BF16), v_aug)
                m_scr[rs, :] = m_new

    def pair_body(i, carry):
        slc_tile(2 * i, causal=False)
        slc_tile(2 * i + 1, causal=False)
        return carry

    lax.fori_loop(0, lax.shift_right_logical(n_kt - 1, 1), pair_body, 0)
    odd = ((n_kt - 1) & 1) == 1
    short = (t0 & (tk - 1)) == 0
    for odd_case in (True, False):
        for short_case in (True, False):
            @pl.when((odd if odd_case else jnp.logical_not(odd)) & (short if short_case else jnp.logical_not(short)))
            def _(odd_case=odd_case, short_case=short_case):
                if odd_case:
                    slc_tile(n_kt - 2, causal=False)
                slc_tile(n_kt - 1, causal=True, width=tq if short_case else tk)

    for gi in range(gps):
        for h in range(hpg):
            rs = slice(gi * rows + h * tq, gi * rows + (h + 1) * tq)
            sl = slice(gi * hw + h * LANES, gi * hw + (h + 1) * LANES)
            o_slc = acc_scr[rs, 0:LANES] * (1.0 / acc_scr[rs, LANES:2 * LANES])
            o_ref[:, sl] = (mix_scr[:, sl] + o_slc * z1_ref[:, sl]).astype(BF16)


def nsa_attention(qp, qr, zg, cmp_kv, kv, e_t, cmp_to_sel_t, batch, seq):
    t = qp.shape[0]
    tq, gps = NSA_TQ, NSA_GPS
    assert seq % NSA_TK == 0 and NSA_TK % tq == 0 and seq >= NSA_WINDOW + tq and NSA_G % gps == 0
    qt_per_seq = seq // tq
    nc = seq // CMP_STRIDE
    hw = gps * NSA_HPG * NSA_DH
    n_gg = NSA_G // gps

    qz_spec = pl.BlockSpec((tq, hw), lambda b, g, i: (b * qt_per_seq + i, g))
    cmp_specs = [pl.BlockSpec((1, 1, gps, nc, LANES), functools.partial(
        lambda b, g, i, s: (b, s, g, 0, 0), s=s)) for s in range(2)]
    kv_specs = [pl.BlockSpec((seq, gps * LANES), functools.partial(
        lambda b, g, i, s: (b, s * n_gg + g), s=s)) for s in range(4)]
    const_specs = [pl.BlockSpec(e_t.shape, lambda b, g, i: (0, 0)),
                   pl.BlockSpec(cmp_to_sel_t.shape, lambda b, g, i: (0, 0))]
    rows = gps * NSA_HPG * tq
    return pl.pallas_call(
        functools.partial(_nsa_kernel, seq=seq),
        grid=(batch, n_gg, qt_per_seq),
        in_specs=[qz_spec] * 5 + cmp_specs + kv_specs + const_specs,
        out_specs=qz_spec,
        out_shape=jax.ShapeDtypeStruct((t, NSA_WIDTH), BF16),
        scratch_shapes=[pltpu.VMEM((rows, 2 * LANES), BF16),
                        pltpu.VMEM((rows, LANES), F32),
                        pltpu.VMEM((rows, 2 * LANES), F32),
                        pltpu.VMEM((tq, hw), F32)],
        compiler_params=_cparams(("arbitrary", "arbitrary", "arbitrary")),
        name="nsa_attention",
    )(qp, qr, *zg, cmp_kv, cmp_kv, kv, kv, kv, kv, e_t, cmp_to_sel_t)


def _swa_kernel(sink_ref, q_ref, z_ref, *rest):
    kv_refs, o_ref = rest[:-1], rest[-1]
    tq = ATT_TQ
    n_pair = SWA_KVH // 2
    per_kv = SWA_HPG * SWA_DH // LANES
    scale = 1.0 / math.sqrt(SWA_DH)
    c_exp = LOG2E * scale
    t0 = pl.program_id(1) * tq
    ones_k = jnp.ones((2 * tq, LANES), BF16)

    def window(kind, var, pair):
        base = (kind * 4 + var) * 2
        sl = slice(pair * LANES, (pair + 1) * LANES)
        return jnp.concatenate([kv_refs[base][:, sl], kv_refs[base + 1][:, sl]], axis=0)

    t_col = t0 + lax.broadcasted_iota(jnp.int32, (tq, 1), 0)
    kpos = t0 - tq + lax.broadcasted_iota(jnp.int32, (1, 2 * tq), 1)
    ok = (kpos <= t_col) & (kpos > t_col - SWA_WINDOW) & (kpos >= 0)
    bias = jnp.where(ok, 0.0, NEG_BIG)
    col0 = lax.broadcasted_iota(jnp.int32, (tq, LANES), 1) == 0
    keep = jnp.where(lax.broadcasted_iota(jnp.int32, (2 * tq, LANES), 0) == 0, 0.0, 1.0).astype(BF16)

    for pair in range(n_pair):
        for j in range(2):
            k_var = [window(0, 2 * j + par, pair) * keep for par in range(2)]
            v_var = [jnp.concatenate([window(1, 2 * j + par, pair) * keep, ones_k], axis=1) for par in range(2)]
            for c in range(per_kv):
                col = ((pair * 2 + j) * per_kv + c) * LANES
                qc = q_ref[:, col:col + LANES]
                o_chunk = None
                for par in range(2):
                    head = (pair * 2 + j) * SWA_HPG + 2 * c + par
                    sink_raw = sink_ref[head] * math.sqrt(SWA_DH)
                    bias_h = jnp.concatenate([jnp.where(col0, sink_raw, bias[:, 0:LANES]), bias[:, LANES:]], axis=1)
                    sc = _dot_nt(qc, k_var[par]) + bias_h
                    m = jnp.max(sc, axis=-1, keepdims=True)
                    e = jnp.exp2((sc - m) * c_exp)
                    acc = _dot(e.astype(BF16), v_var[par])
                    o_par = acc[:, 0:LANES] * (1.0 / acc[:, LANES:2 * LANES])
                    o_chunk = o_par if o_chunk is None else o_chunk + o_par
                o_ref[:, col:col + LANES] = (o_chunk * z_ref[:, col:col + LANES]).astype(BF16)


def swa_attention(q, zs, kvb, sinks, batch, seq):
    t = q.shape[0]
    tq = ATT_TQ
    qt_per_seq = seq // tq
    kvw = SWA_KVH * SWA_DH

    def row(b, i):
        return b * qt_per_seq + i

    def prev_row(b, i):
        return b * qt_per_seq + jnp.maximum(i - 1, 0)

    qz_spec = pl.BlockSpec((tq, SWA_WIDTH), lambda b, i: (row(b, i), 0))
    kv_specs = []
    for cb in range(8):
        kv_specs.append(pl.BlockSpec((tq, kvw), functools.partial(lambda b, i, cb: (prev_row(b, i), cb), cb=cb)))
        kv_specs.append(pl.BlockSpec((tq, kvw), functools.partial(lambda b, i, cb: (row(b, i), cb), cb=cb)))
    return pl.pallas_call(
        _swa_kernel,
        grid=(batch, qt_per_seq),
        in_specs=[pl.BlockSpec(memory_space=pltpu.SMEM), qz_spec, qz_spec] + kv_specs,
        out_specs=qz_spec,
        out_shape=jax.ShapeDtypeStruct((t, SWA_WIDTH), BF16),
        compiler_params=_cparams(("arbitrary", "arbitrary")),
        name="swa_attention",
    )(sinks, q, zs, *([kvb] * 16))


def _selection_constants(seq):
    nc = seq // CMP_STRIDE
    ns = seq // SEL_BLOCK
    nsp = -(-ns // LANES) * LANES
    e_t = np.zeros((seq, nsp), np.float32)
    e_t[np.arange(seq), np.arange(seq) // SEL_BLOCK] = NEG_BIG
    c0 = np.arange(nc - 1) * CMP_STRIDE
    s0 = np.arange(ns) * SEL_BLOCK
    overlap = (c0[:, None] < s0[None, :] + SEL_BLOCK) & (c0[:, None] + CMP_LEN > s0[None, :])
    assert ns < nsp
    cmp_to_sel_t = np.zeros((nsp, nc), np.float32)
    cmp_to_sel_t[:ns, :nc - 1] = overlap.T
    cmp_to_sel_t[nsp - 1, :] = 1.0
    return jnp.asarray(e_t, BF16), jnp.asarray(cmp_to_sel_t, BF16)


def kernel(x, p, a_norm, a_w_in, a_w_out, a_cmp_pos_k, a_cmp_w1_k, a_cmp_w2_k, a_cmp_pos_v, a_cmp_w1_v,
           a_cmp_w2_v, kv_norm, w_kv, b_norm, b_w_in, b_w_out, b_sinks, ple_norm, ple_gate_w, ple_proj,
           final_norm):
    batch, seq, d = x.shape
    assert d == D_MODEL and p.shape[0] == 2 and a_w_in.shape[0] == 1 and b_w_in.shape[0] == 1
    t = batch * seq
    xf = x.reshape(t, d)

    n_gate = 3 * NSA_G * NSA_HPG
    w_q, w_cmp, w_kvs, w_gate, *w_z = split_a_weights(a_w_in)
    wg = w_gate[:, :n_gate].reshape(d, 3, NSA_G, NSA_HPG).transpose(0, 2, 1, 3)
    wg = wg.reshape(d, NSA_G, 3 * NSA_HPG)
    wg = jnp.pad(wg, ((0, 0), (0, 0), (0, LANES - 3 * NSA_HPG))).reshape(d, NSA_G * LANES)
    pos_kv = jnp.stack([a_cmp_pos_k[0], a_cmp_pos_v[0]])
    w1_kv = jnp.stack([a_cmp_w1_k[0], a_cmp_w1_v[0]]).astype(BF16)
    w2_kv = jnp.stack([a_cmp_w2_k[0], a_cmp_w2_v[0]]).astype(BF16)
    wb = b_w_in[0].astype(BF16)
    tabs_a = _rope_tables(seq, NSA_DH)
    tabs_b = _rope_tables(seq, SWA_DH)
    e_t, cmp_to_sel_t = _selection_constants(seq)

    (n_a,) = rms_norm_rows(xf, a_norm, BF16)
    q_rot, q_plain = matmul_rope(n_a, w_q, tabs_a, seq, NSA_DH, MM_TN, True, "a_q_proj")
    (kv_a,) = matmul_rope(n_a, w_kvs, tabs_a, seq, NSA_DH, NSA_KVW, False, "a_kv_proj")
    kcvc = matmul(n_a, w_cmp, "a_cmp_proj")
    gates = matmul(n_a, wg, "a_gate_proj", act="sigmoid")
    zg = [matmul_zgate(n_a, w_z[br], gates, br, "a_z_proj%d" % br) for br in range(3)]
    cmp_kv = compress_kv(kcvc, pos_kv, w1_kv, w2_kv, batch, seq)
    mixed = nsa_attention(q_plain, q_rot, zg, cmp_kv, kv_a, e_t, cmp_to_sel_t, batch, seq)
    h, (n_p,), ssq = matmul_residual(mixed, a_w_out[0].astype(BF16), xf, ple_norm[0:1], "a_out_proj")
    pf = p.reshape(2, t, PLE_DIM)
    ple_wg = ple_gate_w.astype(BF16)
    ple_wp = ple_proj.astype(BF16)
    h, (n_kv, n_b), ssq = ple_update(n_p, ssq, ple_wg, pf, ple_wp, 0, h, jnp.stack([kv_norm, b_norm[0]]), "ple0")

    kvb = matmul_kvb(n_kv, w_kv.astype(BF16), ssq, tabs_b, seq, "kv_proj")
    (q_b,) = matmul_rope(n_b, wb, tabs_b, seq, SWA_DH, MM_TN, False, "b_q_proj", ssq=ssq, cols=(0, SWA_WIDTH))
    zs_b = matmul(n_b, wb, "b_z_proj", act="silu", ssq=ssq, cols=(SWA_WIDTH, SWA_WIDTH))
    ob = swa_attention(q_b, zs_b, kvb, b_sinks[0], batch, seq)
    h, (n_p,), ssq = matmul_residual(ob, b_w_out[0].astype(BF16), h, ple_norm[1:2], "b_out_proj")
    h = ple_update(n_p, ssq, ple_wg, pf, ple_wp, 1, h, None, "ple1")

    (out,) = rms_norm_rows(h, final_norm[None, :], F32)
    return out.reshape(batch, seq, d)
```

```python
import functools
import math

import numpy as np
import jax
import jax.numpy as jnp
from jax import lax
from jax.experimental import pallas as pl
from jax.experimental.pallas import tpu as pltpu

F32 = jnp.float32
BF16 = jnp.bfloat16

D_MODEL = 4096
RMS_EPS = 1e-6
ROPE_THETA = 500000.0
NEG_BIG = -1e30
FORCE_SCORE = 1e9
PLE_DIM = 256

NSA_DH = 128
NSA_G = 4
NSA_HPG = 8
NSA_WIDTH = 4096
NSA_KVW = 512
CMP_LEN = 32
CMP_STRIDE = 16
CMP_HIDDEN = 256
SEL_BLOCK = 64
SEL_TOPK = 16
SEL_LOCAL = 2
NSA_WINDOW = 512

SWA_DH = 64
SWA_KVH = 8
SWA_HPG = 8
SWA_WIDTH = 4096
SWA_WINDOW = 128

SUBLANES = 8
LANES = 128
V7X_VMEM_BYTES = 64 * 1024 * 1024
VMEM_LIMIT_BYTES = V7X_VMEM_BYTES - 4 * 1024 * 1024

MM_TM = 1024
MM_TN = 1024
MM_ROW_CHUNKS = 8
NORM_ROWS = 512
ATT_TQ = 128
NSA_TQ = 256
NSA_TK = 512
NSA_GPS = 2

LOG2E = math.log2(math.e)


def _cparams(sem):
    return pltpu.CompilerParams(dimension_semantics=sem, vmem_limit_bytes=VMEM_LIMIT_BYTES)


def _sigmoid(v):
    return 1.0 / (1.0 + jnp.exp(-v))


def _silu(v):
    return v * _sigmoid(v)


def _dot(a, b):
    return jnp.dot(a, b, preferred_element_type=F32)


def _dot_nt(a, b):
    return lax.dot_general(a, b, (((1,), (1,)), ((), ())), preferred_element_type=F32)


def _lane_tile(x, n):
    return jnp.concatenate([x] * n, axis=1)


def _norm_kernel(h_ref, g_ref, *o_refs):
    x = h_ref[...]
    r = lax.rsqrt(jnp.mean(x * x, axis=-1, keepdims=True) + RMS_EPS)
    y = x * r
    for i, o_ref in enumerate(o_refs):
        o_ref[...] = (y * g_ref[i:i + 1, :]).astype(o_ref.dtype)


def rms_norm_rows(h, gains, out_dtype):
    t, d = h.shape
    k = gains.shape[0]
    outs = pl.pallas_call(
        _norm_kernel,
        grid=(t // NORM_ROWS,),
        in_specs=[pl.BlockSpec((NORM_ROWS, d), lambda i: (i, 0)),
                  pl.BlockSpec((k, d), lambda i: (0, 0))],
        out_specs=[pl.BlockSpec((NORM_ROWS, d), lambda i: (i, 0))] * k,
        out_shape=[jax.ShapeDtypeStruct((t, d), out_dtype)] * k,
        compiler_params=_cparams(("arbitrary",)),
        name="rms_norm",
    )(h, gains)
    return outs


def _rope_tables(seq, head_dim):
    rot = head_dim // 4
    half = rot // 2
    inv_freq = ROPE_THETA ** (-jnp.arange(half, dtype=F32) / half)
    ang = jnp.arange(seq, dtype=jnp.int32).astype(F32)[:, None] * inv_freq[None, :]
    cos, sin = jnp.cos(ang), jnp.sin(ang)
    rest = head_dim - rot
    zeros_h = jnp.zeros((seq, half), F32)
    c = jnp.concatenate([cos, cos, jnp.ones((seq, rest), F32)], axis=1)
    s1 = jnp.concatenate([-sin, zeros_h, jnp.zeros((seq, rest), F32)], axis=1)
    s2 = jnp.concatenate([zeros_h, sin, jnp.zeros((seq, rest), F32)], axis=1)
    reps = LANES // head_dim
    return tuple(jnp.tile(a, (1, reps)) for a in (c, s1, s2))


def _rope_lanes(t, c, s1, s2, half):
    return t * c + pltpu.roll(t, LANES - half, 1) * s1 + pltpu.roll(t, half, 1) * s2


def _row_chunks(ref, n=MM_ROW_CHUNKS):
    rows = ref.shape[0] // n
    return [slice(c * rows, (c + 1) * rows) for c in range(n)]


def _scaled_dot(x_ref, w_ref, s_ref, rs):
    acc = _dot(x_ref[rs, :], w_ref[...])
    if s_ref is not None:
        r = lax.rsqrt(s_ref[rs, :] * (1.0 / x_ref.shape[1]) + RMS_EPS)
        acc = acc * _lane_tile(r, acc.shape[1] // LANES)
    return acc


def _write_bf16_copy(h, rs, hb_ref):
    hb_ref[rs, :] = h.astype(BF16)
    return jnp.broadcast_to(jnp.sum(h * h, axis=-1, keepdims=True), (h.shape[0], LANES))


def _accumulate_ssq(parts, ssq_ref):
    part = jnp.concatenate(parts, axis=0)
    j = pl.program_id(1)

    @pl.when(j == 0)
    def _():
        ssq_ref[...] = part

    @pl.when(j > 0)
    def _():
        ssq_ref[...] = ssq_ref[...] + part


def _mm_kernel(*refs, act, scaled):
    x_ref, w_ref = refs[0], refs[1]
    s_ref = refs[2] if scaled else None
    o_ref = refs[-1]
    for rs in _row_chunks(o_ref):
        acc = _scaled_dot(x_ref, w_ref, s_ref, rs)
        if act == "silu":
            acc = _silu(acc)
        elif act == "sigmoid":
            acc = _sigmoid(acc)
        o_ref[rs, :] = acc.astype(o_ref.dtype)


def _mm_zgate_kernel(x_ref, w_ref, g_ref, o_ref, *, first_col):
    for rs in _row_chunks(o_ref):
        acc = _silu(_dot(x_ref[rs, :], w_ref[...]))
        for h in range(acc.shape[1] // LANES):
            sl = slice(h * LANES, (h + 1) * LANES)
            o_ref[rs, sl] = acc[:, sl] * g_ref[rs, first_col + h:first_col + h + 1]


def _mm_res_kernel(x_ref, w_ref, r_ref, h_ref, hb_ref, ssq_ref):
    parts = []
    for rs in _row_chunks(h_ref):
        h = r_ref[rs, :] + _dot(x_ref[rs, :], w_ref[...])
        h_ref[rs, :] = h
        parts.append(_write_bf16_copy(h, rs, hb_ref))
    _accumulate_ssq(parts, ssq_ref)


def _ple_kernel(n_ref, wg_ref, s_ref, p_ref, wp_ref, h_ref, o_ref, *rest):
    parts = []
    for rs in _row_chunks(o_ref):
        gate = _sigmoid(_scaled_dot(n_ref, wg_ref, s_ref, rs))
        emb = _dot(p_ref[rs, :].astype(BF16), wp_ref[...])
        h = h_ref[rs, :] + gate * emb
        o_ref[rs, :] = h
        if rest:
            parts.append(_write_bf16_copy(h, rs, rest[0]))
    if rest:
        _accumulate_ssq(parts, rest[1])


def _mm_rope_kernel(*refs, half, n_rope, plain_too, scaled):
    x_ref, w_ref = refs[0], refs[1]
    s_ref = refs[2] if scaled else None
    c_ref, s1_ref, s2_ref = refs[3:6] if scaled else refs[2:5]
    o_refs = refs[6:] if scaled else refs[5:]
    for rs in _row_chunks(o_refs[0]):
        acc = _scaled_dot(x_ref, w_ref, s_ref, rs)
        c, s1, s2 = c_ref[rs, :], s1_ref[rs, :], s2_ref[rs, :]
        for ch in range(acc.shape[1] // LANES):
            sl = slice(ch * LANES, (ch + 1) * LANES)
            t = acc[:, sl]
            o_refs[0][rs, sl] = (_rope_lanes(t, c, s1, s2, half) if ch < n_rope else t).astype(BF16)
            if plain_too:
                o_refs[1][rs, sl] = t.astype(BF16)


def _mm_kvb_kernel(x_ref, w_ref, s_ref, c_ref, s1_ref, s2_ref, o_ref, *, half):
    kvw = SWA_KVH * SWA_DH
    n_ch = kvw // LANES
    for rs in _row_chunks(o_ref):
        acc = _scaled_dot(x_ref, w_ref, s_ref, rs)
        c, s1, s2 = c_ref[rs, :], s1_ref[rs, :], s2_ref[rs, :]
        low = lax.broadcasted_iota(jnp.int32, (acc.shape[0], LANES), 1) < SWA_DH
        for kind in range(2):
            for ch in range(n_ch):
                t = acc[:, kind * kvw + ch * LANES: kind * kvw + (ch + 1) * LANES]
                if kind == 0:
                    t = _rope_lanes(t, c, s1, s2, half)
                ts = pltpu.roll(t, SWA_DH, 1)
                variants = (jnp.where(low, t, 0.0), jnp.where(low, 0.0, ts),
                            jnp.where(low, ts, 0.0), jnp.where(low, 0.0, t))
                for var, val in enumerate(variants):
                    col = (kind * 4 + var) * kvw + ch * LANES
                    o_ref[rs, col:col + LANES] = val.astype(BF16)


def _mm_tiles(m, n, tn_max=MM_TN):
    tm = min(MM_TM, m)
    tn = min(tn_max, n)
    assert m % tm == 0 and n % tn == 0, (m, n)
    return tm, tn


def _ssq_spec(tm):
    return pl.BlockSpec((tm, LANES), lambda i, j: (i, 0))


def _w_cols(w, cols, m):
    first, n = (0, w.shape[1]) if cols is None else cols
    tm, tn = _mm_tiles(m, n)
    assert first % tn == 0
    return first // tn, n, tm, tn


def matmul(x, w, name, act=None, out_dtype=F32, ssq=None, cols=None):
    m, k = x.shape
    jb, n, tm, tn = _w_cols(w, cols, m)
    scaled = ssq is not None
    return pl.pallas_call(
        functools.partial(_mm_kernel, act=act, scaled=scaled),
        grid=(m // tm, n // tn),
        in_specs=[pl.BlockSpec((tm, k), lambda i, j: (i, 0)),
                  pl.BlockSpec((k, tn), lambda i, j: (0, jb + j))] + ([_ssq_spec(tm)] if scaled else []),
        out_specs=pl.BlockSpec((tm, tn), lambda i, j: (i, j)),
        out_shape=jax.ShapeDtypeStruct((m, n), out_dtype),
        compiler_params=_cparams(("arbitrary", "arbitrary")),
        name=name,
    )(x, w, *([ssq] if scaled else []))


def matmul_zgate(x, w, gates, branch, name):
    m, k = x.shape
    n = w.shape[1]
    tm, tn = _mm_tiles(m, n)
    assert tn == NSA_HPG * NSA_DH
    return pl.pallas_call(
        functools.partial(_mm_zgate_kernel, first_col=branch * NSA_HPG),
        grid=(m // tm, n // tn),
        in_specs=[pl.BlockSpec((tm, k), lambda i, j: (i, 0)),
                  pl.BlockSpec((k, tn), lambda i, j: (0, j)),
                  pl.BlockSpec((tm, LANES), lambda i, j: (i, j))],
        out_specs=pl.BlockSpec((tm, tn), lambda i, j: (i, j)),
        out_shape=jax.ShapeDtypeStruct((m, n), F32),
        compiler_params=_cparams(("arbitrary", "arbitrary")),
        name=name,
    )(x, w, gates)


def matmul_rope(x, w, tabs, seq, head_dim, rope_cols, plain_too, name, ssq=None, cols=None):
    m, k = x.shape
    jb, n, tm, tn = _w_cols(w, cols, m)
    assert seq % tm == 0
    per_seq = seq // tm
    scaled = ssq is not None
    tab_spec = pl.BlockSpec((tm, LANES), lambda i, j: (i % per_seq, 0))
    n_out = 2 if plain_too else 1
    outs = pl.pallas_call(
        functools.partial(_mm_rope_kernel, half=head_dim // 8, n_rope=rope_cols // LANES,
                          plain_too=plain_too, scaled=scaled),
        grid=(m // tm, n // tn),
        in_specs=[pl.BlockSpec((tm, k), lambda i, j: (i, 0)),
                  pl.BlockSpec((k, tn), lambda i, j: (0, jb + j))]
        + ([_ssq_spec(tm)] if scaled else []) + [tab_spec] * 3,
        out_specs=[pl.BlockSpec((tm, tn), lambda i, j: (i, j))] * n_out,
        out_shape=[jax.ShapeDtypeStruct((m, n), BF16)] * n_out,
        compiler_params=_cparams(("arbitrary", "arbitrary")),
        name=name,
    )(x, w, *([ssq] if scaled else []), *tabs)
    return outs


def matmul_kvb(x, w, ssq, tabs, seq, name):
    m, k = x.shape
    n = w.shape[1]
    tm = min(MM_TM, m)
    assert seq % tm == 0 and n == 2 * SWA_KVH * SWA_DH
    per_seq = seq // tm
    tab_spec = pl.BlockSpec((tm, LANES), lambda i: (i % per_seq, 0))
    return pl.pallas_call(
        functools.partial(_mm_kvb_kernel, half=SWA_DH // 8),
        grid=(m // tm,),
        in_specs=[pl.BlockSpec((tm, k), lambda i: (i, 0)),
                  pl.BlockSpec((k, n), lambda i: (0, 0)),
                  pl.BlockSpec((tm, LANES), lambda i: (i, 0))] + [tab_spec] * 3,
        out_specs=pl.BlockSpec((tm, 4 * n), lambda i: (i, 0)),
        out_shape=jax.ShapeDtypeStruct((m, 4 * n), BF16),
        compiler_params=_cparams(("arbitrary",)),
        name=name,
    )(x, w, ssq, *tabs)


def matmul_residual(x, w, res, name):
    m, k = x.shape
    n = w.shape[1]
    tm, tn = _mm_tiles(m, n)
    tile = pl.BlockSpec((tm, tn), lambda i, j: (i, j))
    return pl.pallas_call(
        _mm_res_kernel,
        grid=(m // tm, n // tn),
        in_specs=[pl.BlockSpec((tm, k), lambda i, j: (i, 0)),
                  pl.BlockSpec((k, tn), lambda i, j: (0, j)),
                  tile],
        out_specs=[tile, tile, _ssq_spec(tm)],
        out_shape=[jax.ShapeDtypeStruct((m, n), F32), jax.ShapeDtypeStruct((m, n), BF16),
                   jax.ShapeDtypeStruct((m, LANES), F32)],
        compiler_params=_cparams(("arbitrary", "arbitrary")),
        name=name,
    )(x, w, res)


def ple_update(n, ssq, wg, p, wp, layer, h, emit_copy, name):
    m, k = n.shape
    d = wg.shape[2]
    kp = p.shape[2]
    tm, tn = _mm_tiles(m, d)
    tile = pl.BlockSpec((tm, tn), lambda i, j: (i, j))
    out_specs = [tile]
    out_shape = [jax.ShapeDtypeStruct((m, d), F32)]
    if emit_copy:
        out_specs += [tile, _ssq_spec(tm)]
        out_shape += [jax.ShapeDtypeStruct((m, d), BF16), jax.ShapeDtypeStruct((m, LANES), F32)]
    outs = pl.pallas_call(
        _ple_kernel,
        grid=(m // tm, d // tn),
        in_specs=[pl.BlockSpec((tm, k), lambda i, j: (i, 0)),
                  pl.BlockSpec((None, k, tn), lambda i, j: (layer, 0, j)),
                  _ssq_spec(tm),
                  pl.BlockSpec((None, tm, kp), lambda i, j: (layer, i, 0)),
                  pl.BlockSpec((None, kp, tn), lambda i, j: (layer, 0, j)),
                  tile],
        out_specs=out_specs,
        out_shape=out_shape,
        compiler_params=_cparams(("arbitrary", "arbitrary")),
        name=name,
    )(n, wg, ssq, p, wp, h)
    return outs if emit_copy else outs[0]


A_W_COLS = 512


def _wt_cast_kernel(wt_ref, o_ref):
    o_ref[...] = wt_ref[...].T.astype(BF16)


def _cast_weight_cols(wt, first, width, name):
    k = wt.shape[1]
    cols = min(A_W_COLS, width)
    assert width % cols == 0
    return pl.pallas_call(
        _wt_cast_kernel,
        grid=(width // cols,),
        in_specs=[pl.BlockSpec((pl.Element(cols), pl.Element(k)), lambda i: (pl.multiple_of(first + i * cols, SUBLANES), 0))],
        out_specs=pl.BlockSpec((k, cols), lambda i: (0, i)),
        out_shape=jax.ShapeDtypeStruct((k, width), BF16),
        compiler_params=_cparams(("arbitrary",)),
        name=name,
    )(wt)


def split_a_weights(wa):
    wt = wa[0].T
    c_q, c_cmp, c_kv = NSA_WIDTH, NSA_WIDTH + 2 * NSA_KVW, NSA_WIDTH + 6 * NSA_KVW
    c_z = c_kv + 3 * NSA_G * NSA_HPG
    segs = [(0, c_q), (c_q, c_cmp - c_q), (c_cmp, c_kv - c_cmp), (c_kv, LANES)]
    segs += [(c_z + br * NSA_WIDTH, NSA_WIDTH) for br in range(3)]
    return [_cast_weight_cols(wt, first, width, "a_w_cast%d" % i) for i, (first, width) in enumerate(segs)]


def _compress_kernel(x_ref, pos_ref, w1_ref, w2_ref, o_ref, *, n_cmp_pad):
    nc = n_cmp_pad
    lo = jnp.zeros((nc, CMP_HIDDEN), F32)
    hi = jnp.zeros((nc, CMP_HIDDEN), F32)
    for l in range(CMP_STRIDE):
        xl = x_ref[pl.ds(l, nc, stride=CMP_STRIDE), :]
        a = (xl + pos_ref[0, l:l + 1, :]).astype(BF16)
        b = (xl + pos_ref[0, CMP_STRIDE + l:CMP_STRIDE + l + 1, :]).astype(BF16)
        lo = lo + _dot(a, w1_ref[0, l * LANES:(l + 1) * LANES, :])
        hi = hi + _dot(b, w1_ref[0, (CMP_STRIDE + l) * LANES:(CMP_STRIDE + l + 1) * LANES, :])
    hid = lo + pltpu.roll(hi, nc - 1, 0)
    out = _dot(_silu(hid).astype(BF16), w2_ref[0])
    n_idx = lax.broadcasted_iota(jnp.int32, (nc, LANES), 0)
    o_ref[0, 0, 0] = jnp.where(n_idx < nc - 1, out, 0.0).astype(BF16)


def compress_kv(kcvc, pos, w1, w2, batch, seq):
    nc = seq // CMP_STRIDE
    return pl.pallas_call(
        functools.partial(_compress_kernel, n_cmp_pad=nc),
        grid=(batch, 2, NSA_G),
        in_specs=[pl.BlockSpec((seq, LANES), lambda b, s, g: (b, s * NSA_G + g)),
                  pl.BlockSpec((1, CMP_LEN, LANES), lambda b, s, g: (s, 0, 0)),
                  pl.BlockSpec((1, CMP_LEN * LANES, CMP_HIDDEN), lambda b, s, g: (s, 0, 0)),
                  pl.BlockSpec((1, CMP_HIDDEN, LANES), lambda b, s, g: (s, 0, 0))],
        out_specs=pl.BlockSpec((1, 1, 1, nc, LANES), lambda b, s, g: (b, s, g, 0, 0)),
        out_shape=jax.ShapeDtypeStruct((batch, 2, NSA_G, nc, LANES), BF16),
        compiler_params=_cparams(("arbitrary", "arbitrary", "arbitrary")),
        name="compress_kv",
    )(kcvc, pos, w1, w2)


def _nsa_kernel(qp_ref, qr_ref, z0_ref, z1_ref, z2_ref,
                kc_ref, vc_ref, ks_ref, vs_ref, kw_ref, vw_ref, et_ref, mt_ref,
                o_ref, qa_scr, m_scr, acc_scr, mix_scr, *, seq):
    tq, tk, hpg, gps = NSA_TQ, NSA_TK, NSA_HPG, NSA_GPS
    rows = hpg * tq
    hw = hpg * LANES
    nc = seq // CMP_STRIDE
    ns = seq // SEL_BLOCK
    n_top = min(SEL_TOPK, ns)
    nsp = mt_ref.shape[0]
    c_exp = LOG2E / math.sqrt(NSA_DH)
    t0 = pl.program_id(2) * tq
    ones_k = jnp.ones((tk, LANES), BF16)

    t_col = t0 + lax.broadcasted_iota(jnp.int32, (tq, 1), 0)

    n_idx = lax.broadcasted_iota(jnp.int32, (1, nc), 1)
    valid = (n_idx * CMP_STRIDE + (CMP_LEN - 1) <= t_col) & (n_idx < nc - 1)
    bias_c = jnp.where(valid, 0.0, NEG_BIG)
    row_ok = t0 + lax.broadcasted_iota(jnp.int32, (tq, LANES), 0) >= CMP_LEN - 1
    lane_ok = t0 + lax.broadcasted_iota(jnp.int32, (ns, tq), 1) >= CMP_LEN - 1
    hq = tq // 2
    span = NSA_WINDOW + hq
    win = []
    for hh in range(2):
        kw0 = pl.multiple_of(jnp.maximum(t0 + hh * hq - NSA_WINDOW, 0), hq)
        t_half = t_col[hh * hq:(hh + 1) * hq]
        kpos_w = kw0 + lax.broadcasted_iota(jnp.int32, (1, span), 1)
        win.append((kw0, jnp.where((kpos_w <= t_half) & (kpos_w > t_half - NSA_WINDOW), 0.0, NEG_BIG)))
    j_io = lax.broadcasted_iota(jnp.int32, (ns, tq), 0)
    t_io = t0 + lax.broadcasted_iota(jnp.int32, (ns, tq), 1)
    dist = lax.shift_right_logical(t_io, int(math.log2(SEL_BLOCK))) - j_io
    forced = (j_io == 0) | ((dist >= 0) & (dist < SEL_LOCAL))

    for gi in range(gps):
        gl = slice(gi * LANES, (gi + 1) * LANES)
        kc = kc_ref[0, 0, gi]
        vc_aug = jnp.concatenate([vc_ref[0, 0, gi], jnp.ones((nc, LANES), BF16)], axis=1)
        q_all = jnp.concatenate([qp_ref[:, gi * hw + h * LANES:gi * hw + (h + 1) * LANES] for h in range(hpg)],
                                axis=0)
        sc = _dot_nt(q_all, kc) + jnp.concatenate([bias_c] * hpg, axis=0)
        m = jnp.max(sc, axis=-1, keepdims=True)
        pe = jnp.exp2((sc - m) * c_exp).astype(BF16)
        acc_c = _dot(pe, vc_aug)
        imp_t = _dot_nt(mt_ref[...], pe)
        imp_parts = []
        for h in range(hpg):
            sl = slice(gi * hw + h * LANES, gi * hw + (h + 1) * LANES)
            rs = slice(h * tq, (h + 1) * tq)
            oc = jnp.where(row_ok, acc_c[rs, 0:LANES] * (1.0 / acc_c[rs, LANES:2 * LANES]), 0.0)
            imp_parts.append(imp_t[0:ns, rs] * (1.0 / imp_t[nsp - 1:nsp, rs]))
            ows = []
            for hh, (kw0, bias_w) in enumerate(win):
                vw_aug = jnp.concatenate([vw_ref[pl.ds(kw0, span), gl], jnp.ones((span, LANES), BF16)], axis=1)
                sc = _dot_nt(qr_ref[hh * hq:(hh + 1) * hq, sl], kw_ref[pl.ds(kw0, span), gl]) + bias_w
                m = jnp.max(sc, axis=-1, keepdims=True)
                pe_w = jnp.exp2((sc - m) * c_exp).astype(BF16)
                acc = _dot(pe_w, vw_aug)
                ows.append(acc[:, 0:LANES] * (1.0 / acc[:, LANES:2 * LANES]))
            ow = jnp.concatenate(ows, axis=0)
            mix_scr[:, sl] = oc * z0_ref[:, sl] + ow * z2_ref[:, sl]
        while len(imp_parts) > 1:
            imp_parts = [imp_parts[i] + imp_parts[i + 1] for i in range(0, len(imp_parts), 2)]
        imp = jnp.where(lane_ok, imp_parts[0], 0.0)

        imp = jnp.where(forced, FORCE_SCORE, imp)
        imp = jnp.where(dist >= 0, imp, -1.0)
        sub = SUBLANES
        ranks = []
        for v in range(ns // sub):
            blk = imp[v * sub:(v + 1) * sub, :]
            j_v = v * sub + lax.broadcasted_iota(jnp.int32, (sub, tq), 0)
            r = jnp.zeros((sub, tq), F32)
            for k in range(ns):
                rk = imp[k:k + 1, :]
                if k < v * sub:
                    ahead = rk >= blk
                elif k >= (v + 1) * sub:
                    ahead = rk > blk
                else:
                    ahead = (rk > blk) | ((rk == blk) & (j_v > k))
                r = r + jnp.where(ahead, 1.0, 0.0)
            ranks.append(r)
        rank = jnp.concatenate(ranks, axis=0)
        not_sel_t = jnp.concatenate([jnp.where(rank < n_top, 0.0, 1.0), jnp.ones((nsp - ns, tq), F32)], axis=0)
        not_sel = not_sel_t.T.astype(BF16)

        for h in range(hpg):
            r0 = gi * rows + h * tq
            qa_scr[r0:r0 + tq, 0:LANES] = qr_ref[:, gi * hw + h * LANES:gi * hw + (h + 1) * LANES]
            qa_scr[r0:r0 + tq, LANES:2 * LANES] = not_sel

    m_scr[...] = jnp.full((gps * rows, LANES), NEG_BIG, F32)
    acc_scr[...] = jnp.zeros((gps * rows, 2 * LANES), F32)
    n_kt = lax.shift_right_logical(t0 + (tq + tk - 1), int(math.log2(tk)))

    def slc_tile(kt, causal, width=tk):
        k0 = pl.multiple_of(kt * tk, tk)
        e_tile = et_ref[pl.ds(k0, width), :]
        if causal:
            kpos = k0 + lax.broadcasted_iota(jnp.int32, (1, width), 1)
            bias = jnp.where(kpos <= t_col, 0.0, NEG_BIG)
        for gi in range(gps):
            gl = slice(gi * LANES, (gi + 1) * LANES)
            k_aug = jnp.concatenate([ks_ref[pl.ds(k0, width), gl], e_tile], axis=1)
            v_aug = jnp.concatenate([vs_ref[pl.ds(k0, width), gl], ones_k[0:width]], axis=1)
            for h in range(hpg):
                rs = slice(gi * rows + h * tq, gi * rows + (h + 1) * tq)
                sc = _dot_nt(qa_scr[rs, :], k_aug)
                if causal:
                    sc = sc + bias
                m_old = m_scr[rs, :]
                m_new = jnp.maximum(m_old, jnp.max(sc, axis=-1, keepdims=True))
                alpha = jnp.exp2((m_old - m_new) * c_exp)
                pe = jnp.exp2((sc - _lane_tile(m_new, width // LANES)) * c_exp)
                acc_scr[rs, :] = _lane_tile(alpha, 2) * acc_scr[rs, :] + _dot(pe.astype(BF16), v_aug)
                m_scr[rs, :] = m_new

    def pair_body(i, carry):
        slc_tile(2 * i, causal=False)
        slc_tile(2 * i + 1, causal=False)
        return carry

    lax.fori_loop(0, lax.shift_right_logical(n_kt - 1, 1), pair_body, 0)
    odd = ((n_kt - 1) & 1) == 1
    short = (t0 & (tk - 1)) == 0
    for odd_case in (True, False):
        for short_case in (True, False):
            @pl.when((odd if odd_case else jnp.logical_not(odd)) & (short if short_case else jnp.logical_not(short)))
            def _(odd_case=odd_case, short_case=short_case):
                if odd_case:
                    slc_tile(n_kt - 2, causal=False)
                slc_tile(n_kt - 1, causal=True, width=tq if short_case else tk)

    for gi in range(gps):
        for h in range(hpg):
            rs = slice(gi * rows + h * tq, gi * rows + (h + 1) * tq)
            sl = slice(gi * hw + h * LANES, gi * hw + (h + 1) * LANES)
            o_slc = acc_scr[rs, 0:LANES] * (1.0 / acc_scr[rs, LANES:2 * LANES])
            o_ref[:, sl] = (mix_scr[:, sl] + o_slc * z1_ref[:, sl]).astype(BF16)


def nsa_attention(qp, qr, zg, cmp_kv, kv, e_t, cmp_to_sel_t, batch, seq):
    t = qp.shape[0]
    tq, gps = NSA_TQ, NSA_GPS
    assert seq % NSA_TK == 0 and NSA_TK % tq == 0 and seq >= NSA_WINDOW + tq and NSA_G % gps == 0
    qt_per_seq = seq // tq
    nc = seq // CMP_STRIDE
    hw = gps * NSA_HPG * NSA_DH
    n_gg = NSA_G // gps

    qz_spec = pl.BlockSpec((tq, hw), lambda b, g, i: (b * qt_per_seq + i, g))
    cmp_specs = [pl.BlockSpec((1, 1, gps, nc, LANES), functools.partial(
        lambda b, g, i, s: (b, s, g, 0, 0), s=s)) for s in range(2)]
    kv_specs = [pl.BlockSpec((seq, gps * LANES), functools.partial(
        lambda b, g, i, s: (b, s * n_gg + g), s=s)) for s in range(4)]
    const_specs = [pl.BlockSpec(e_t.shape, lambda b, g, i: (0, 0)),
                   pl.BlockSpec(cmp_to_sel_t.shape, lambda b, g, i: (0, 0))]
    rows = gps * NSA_HPG * tq
    return pl.pallas_call(
        functools.partial(_nsa_kernel, seq=seq),
        grid=(batch, n_gg, qt_per_seq),
        in_specs=[qz_spec] * 5 + cmp_specs + kv_specs + const_specs,
        out_specs=qz_spec,
        out_shape=jax.ShapeDtypeStruct((t, NSA_WIDTH), BF16),
        scratch_shapes=[pltpu.VMEM((rows, 2 * LANES), BF16),
                        pltpu.VMEM((rows, LANES), F32),
                        pltpu.VMEM((rows, 2 * LANES), F32),
                        pltpu.VMEM((tq, hw), F32)],
        compiler_params=_cparams(("arbitrary", "arbitrary", "arbitrary")),
        name="nsa_attention",
    )(qp, qr, *zg, cmp_kv, cmp_kv, kv, kv, kv, kv, e_t, cmp_to_sel_t)


def _swa_kernel(sink_ref, q_ref, z_ref, *rest):
    kv_refs, o_ref = rest[:-1], rest[-1]
    tq = ATT_TQ
    n_pair = SWA_KVH // 2
    per_kv = SWA_HPG * SWA_DH // LANES
    scale = 1.0 / math.sqrt(SWA_DH)
    c_exp = LOG2E * scale
    t0 = pl.program_id(1) * tq
    ones_k = jnp.ones((2 * tq, LANES), BF16)

    def window(kind, var, pair):
        base = (kind * 4 + var) * 2
        sl = slice(pair * LANES, (pair + 1) * LANES)
        return jnp.concatenate([kv_refs[base][:, sl], kv_refs[base + 1][:, sl]], axis=0)

    t_col = t0 + lax.broadcasted_iota(jnp.int32, (tq, 1), 0)
    kpos = t0 - tq + lax.broadcasted_iota(jnp.int32, (1, 2 * tq), 1)
    ok = (kpos <= t_col) & (kpos > t_col - SWA_WINDOW) & (kpos >= 0)
    bias = jnp.where(ok, 0.0, NEG_BIG)
    col0 = lax.broadcasted_iota(jnp.int32, (tq, LANES), 1) == 0
    keep = jnp.where(lax.broadcasted_iota(jnp.int32, (2 * tq, LANES), 0) == 0, 0.0, 1.0).astype(BF16)

    for pair in range(n_pair):
        for j in range(2):
            k_var = [window(0, 2 * j + par, pair) * keep for par in range(2)]
            v_var = [jnp.concatenate([window(1, 2 * j + par, pair) * keep, ones_k], axis=1) for par in range(2)]
            for c in range(per_kv):
                col = ((pair * 2 + j) * per_kv + c) * LANES
                qc = q_ref[:, col:col + LANES]
                o_chunk = None
                for par in range(2):
                    head = (pair * 2 + j) * SWA_HPG + 2 * c + par
                    sink_raw = sink_ref[head] * math.sqrt(SWA_DH)
                    bias_h = jnp.concatenate([jnp.where(col0, sink_raw, bias[:, 0:LANES]), bias[:, LANES:]], axis=1)
                    sc = _dot_nt(qc, k_var[par]) + bias_h
                    m = jnp.max(sc, axis=-1, keepdims=True)
                    e = jnp.exp2((sc - m) * c_exp)
                    acc = _dot(e.astype(BF16), v_var[par])
                    o_par = acc[:, 0:LANES] * (1.0 / acc[:, LANES:2 * LANES])
                    o_chunk = o_par if o_chunk is None else o_chunk + o_par
                o_ref[:, col:col + LANES] = (o_chunk * z_ref[:, col:col + LANES]).astype(BF16)


def swa_attention(q, zs, kvb, sinks, batch, seq):
    t = q.shape[0]
    tq = ATT_TQ
    qt_per_seq = seq // tq
    kvw = SWA_KVH * SWA_DH

    def row(b, i):
        return b * qt_per_seq + i

    def prev_row(b, i):
        return b * qt_per_seq + jnp.maximum(i - 1, 0)

    qz_spec = pl.BlockSpec((tq, SWA_WIDTH), lambda b, i: (row(b, i), 0))
    kv_specs = []
    for cb in range(8):
        kv_specs.append(pl.BlockSpec((tq, kvw), functools.partial(lambda b, i, cb: (prev_row(b, i), cb), cb=cb)))
        kv_specs.append(pl.BlockSpec((tq, kvw), functools.partial(lambda b, i, cb: (row(b, i), cb), cb=cb)))
    return pl.pallas_call(
        _swa_kernel,
        grid=(batch, qt_per_seq),
        in_specs=[pl.BlockSpec(memory_space=pltpu.SMEM), qz_spec, qz_spec] + kv_specs,
        out_specs=qz_spec,
        out_shape=jax.ShapeDtypeStruct((t, SWA_WIDTH), BF16),
        compiler_params=_cparams(("arbitrary", "arbitrary")),
        name="swa_attention",
    )(sinks, q, zs, *([kvb] * 16))


def _selection_constants(seq):
    nc = seq // CMP_STRIDE
    ns = seq // SEL_BLOCK
    nsp = -(-ns // LANES) * LANES
    e_t = np.zeros((seq, nsp), np.float32)
    e_t[np.arange(seq), np.arange(seq) // SEL_BLOCK] = NEG_BIG
    c0 = np.arange(nc - 1) * CMP_STRIDE
    s0 = np.arange(ns) * SEL_BLOCK
    overlap = (c0[:, None] < s0[None, :] + SEL_BLOCK) & (c0[:, None] + CMP_LEN > s0[None, :])
    assert ns < nsp
    cmp_to_sel_t = np.zeros((nsp, nc), np.float32)
    cmp_to_sel_t[:ns, :nc - 1] = overlap.T
    cmp_to_sel_t[nsp - 1, :] = 1.0
    return jnp.asarray(e_t, BF16), jnp.asarray(cmp_to_sel_t, BF16)


def kernel(x, p, a_norm, a_w_in, a_w_out, a_cmp_pos_k, a_cmp_w1_k, a_cmp_w2_k, a_cmp_pos_v, a_cmp_w1_v,
           a_cmp_w2_v, kv_norm, w_kv, b_norm, b_w_in, b_w_out, b_sinks, ple_norm, ple_gate_w, ple_proj,
           final_norm):
    batch, seq, d = x.shape
    assert d == D_MODEL and p.shape[0] == 2 and a_w_in.shape[0] == 1 and b_w_in.shape[0] == 1
    t = batch * seq
    xf = x.reshape(t, d)

    n_gate = 3 * NSA_G * NSA_HPG
    w_q, w_cmp, w_kvs, w_gate, *w_z = split_a_weights(a_w_in)
    wg = w_gate[:, :n_gate].reshape(d, 3, NSA_G, NSA_HPG).transpose(0, 2, 1, 3)
    wg = wg.reshape(d, NSA_G, 3 * NSA_HPG)
    wg = jnp.pad(wg, ((0, 0), (0, 0), (0, LANES - 3 * NSA_HPG))).reshape(d, NSA_G * LANES)
    pos_kv = jnp.stack([a_cmp_pos_k[0], a_cmp_pos_v[0]])
    w1_kv = jnp.stack([a_cmp_w1_k[0], a_cmp_w1_v[0]]).astype(BF16)
    w2_kv = jnp.stack([a_cmp_w2_k[0], a_cmp_w2_v[0]]).astype(BF16)
    wb = (b_norm[0][:, None] * b_w_in[0]).astype(BF16)
    w_kvb = (kv_norm[:, None] * w_kv).astype(BF16)
    ple_wg = (ple_norm[:, :, None] * ple_gate_w).astype(BF16)
    ple_wp = ple_proj.astype(BF16)
    tabs_a = _rope_tables(seq, NSA_DH)
    tabs_b = _rope_tables(seq, SWA_DH)
    e_t, cmp_to_sel_t = _selection_constants(seq)

    (n_a,) = rms_norm_rows(xf, a_norm, BF16)
    q_rot, q_plain = matmul_rope(n_a, w_q, tabs_a, seq, NSA_DH, MM_TN, True, "a_q_proj")
    (kv_a,) = matmul_rope(n_a, w_kvs, tabs_a, seq, NSA_DH, NSA_KVW, False, "a_kv_proj")
    kcvc = matmul(n_a, w_cmp, "a_cmp_proj")
    gates = matmul(n_a, wg, "a_gate_proj", act="sigmoid")
    zg = [matmul_zgate(n_a, w_z[br], gates, br, "a_z_proj%d" % br) for br in range(3)]
    cmp_kv = compress_kv(kcvc, pos_kv, w1_kv, w2_kv, batch, seq)
    mixed = nsa_attention(q_plain, q_rot, zg, cmp_kv, kv_a, e_t, cmp_to_sel_t, batch, seq)
    h, hb, ssq = matmul_residual(mixed, a_w_out[0].astype(BF16), xf, "a_out_proj")
    pf = p.reshape(2, t, PLE_DIM)
    h, hb, ssq = ple_update(hb, ssq, ple_wg, pf, ple_wp, 0, h, True, "ple0")

    kvb = matmul_kvb(hb, w_kvb, ssq, tabs_b, seq, "kv_proj")
    (q_b,) = matmul_rope(hb, wb, tabs_b, seq, SWA_DH, MM_TN, False, "b_q_proj", ssq=ssq, cols=(0, SWA_WIDTH))
    zs_b = matmul(hb, wb, "b_z_proj", act="silu", ssq=ssq, cols=(SWA_WIDTH, SWA_WIDTH))
    ob = swa_attention(q_b, zs_b, kvb, b_sinks[0], batch, seq)
    h, hb, ssq = matmul_residual(ob, b_w_out[0].astype(BF16), h, "b_out_proj")
    h = ple_update(hb, ssq, ple_wg, pf, ple_wp, 1, h, False, "ple1")

    (out,) = rms_norm_rows(h, final_norm[None, :], F32)
    return out.reshape(batch, seq, d)
```

```python
import functools
import math

import numpy as np
import jax
import jax.numpy as jnp
from jax import lax
from jax.experimental import pallas as pl
from jax.experimental.pallas import tpu as pltpu

F32 = jnp.float32
BF16 = jnp.bfloat16

D_MODEL = 4096
RMS_EPS = 1e-6
ROPE_THETA = 500000.0
NEG_BIG = -1e30
FORCE_SCORE = 1e9
PLE_DIM = 256

NSA_DH = 128
NSA_G = 4
NSA_HPG = 8
NSA_WIDTH = 4096
NSA_KVW = 512
CMP_LEN = 32
CMP_STRIDE = 16
CMP_HIDDEN = 256
SEL_BLOCK = 64
SEL_TOPK = 16
SEL_LOCAL = 2
NSA_WINDOW = 512

SWA_DH = 64
SWA_KVH = 8
SWA_HPG = 8
SWA_WIDTH = 4096
SWA_WINDOW = 128

SUBLANES = 8
LANES = 128
V7X_VMEM_BYTES = 64 * 1024 * 1024
VMEM_LIMIT_BYTES = V7X_VMEM_BYTES - 4 * 1024 * 1024

MM_TM = 1024
MM_TN = 1024
MM_ROW_CHUNKS = 8
NORM_ROWS = 512
ATT_TQ = 128
NSA_TQ = 256
NSA_TK = 512
NSA_GPS = 2

LOG2E = math.log2(math.e)
NSA_QK_SCALE = LOG2E / math.sqrt(NSA_DH)
SWA_QK_SCALE = LOG2E / math.sqrt(SWA_DH)


def _cparams(sem):
    return pltpu.CompilerParams(dimension_semantics=sem, vmem_limit_bytes=VMEM_LIMIT_BYTES)


def _sigmoid(v):
    return 1.0 / (1.0 + jnp.exp(-v))


def _silu(v):
    return v * _sigmoid(v)


def _dot(a, b):
    return jnp.dot(a, b, preferred_element_type=F32)


def _dot_nt(a, b):
    return lax.dot_general(a, b, (((1,), (1,)), ((), ())), preferred_element_type=F32)


def _lane_tile(x, n):
    return jnp.concatenate([x] * n, axis=1)


def _norm_kernel(h_ref, g_ref, *o_refs):
    x = h_ref[...]
    r = lax.rsqrt(jnp.mean(x * x, axis=-1, keepdims=True) + RMS_EPS)
    y = x * r
    for i, o_ref in enumerate(o_refs):
        o_ref[...] = (y * g_ref[i:i + 1, :]).astype(o_ref.dtype)


def rms_norm_rows(h, gains, out_dtype):
    t, d = h.shape
    k = gains.shape[0]
    outs = pl.pallas_call(
        _norm_kernel,
        grid=(t // NORM_ROWS,),
        in_specs=[pl.BlockSpec((NORM_ROWS, d), lambda i: (i, 0)),
                  pl.BlockSpec((k, d), lambda i: (0, 0))],
        out_specs=[pl.BlockSpec((NORM_ROWS, d), lambda i: (i, 0))] * k,
        out_shape=[jax.ShapeDtypeStruct((t, d), out_dtype)] * k,
        compiler_params=_cparams(("arbitrary",)),
        name="rms_norm",
    )(h, gains)
    return outs


def _rope_tables(seq, head_dim):
    rot = head_dim // 4
    half = rot // 2
    inv_freq = ROPE_THETA ** (-jnp.arange(half, dtype=F32) / half)
    ang = jnp.arange(seq, dtype=jnp.int32).astype(F32)[:, None] * inv_freq[None, :]
    cos, sin = jnp.cos(ang), jnp.sin(ang)
    rest = head_dim - rot
    zeros_h = jnp.zeros((seq, half), F32)
    c = jnp.concatenate([cos, cos, jnp.ones((seq, rest), F32)], axis=1)
    s1 = jnp.concatenate([-sin, zeros_h, jnp.zeros((seq, rest), F32)], axis=1)
    s2 = jnp.concatenate([zeros_h, sin, jnp.zeros((seq, rest), F32)], axis=1)
    reps = LANES // head_dim
    return tuple(jnp.tile(a, (1, reps)) for a in (c, s1, s2))


def _rope_lanes(t, c, s1, s2, half):
    return t * c + pltpu.roll(t, LANES - half, 1) * s1 + pltpu.roll(t, half, 1) * s2


def _row_chunks(ref, n=MM_ROW_CHUNKS):
    rows = ref.shape[0] // n
    return [slice(c * rows, (c + 1) * rows) for c in range(n)]


def _scaled_dot(x_ref, w_ref, s_ref, rs):
    acc = _dot(x_ref[rs, :], w_ref[...])
    if s_ref is not None:
        r = lax.rsqrt(s_ref[rs, :] * (1.0 / x_ref.shape[1]) + RMS_EPS)
        acc = acc * _lane_tile(r, acc.shape[1] // LANES)
    return acc


def _write_bf16_copy(h, rs, hb_ref):
    hb_ref[rs, :] = h.astype(BF16)
    return jnp.broadcast_to(jnp.sum(h * h, axis=-1, keepdims=True), (h.shape[0], LANES))


def _accumulate_ssq(parts, ssq_ref):
    part = jnp.concatenate(parts, axis=0)
    j = pl.program_id(1)

    @pl.when(j == 0)
    def _():
        ssq_ref[...] = part

    @pl.when(j > 0)
    def _():
        ssq_ref[...] = ssq_ref[...] + part


def _mm_kernel(*refs, act, scaled):
    x_ref, w_ref = refs[0], refs[1]
    s_ref = refs[2] if scaled else None
    o_ref = refs[-1]
    for rs in _row_chunks(o_ref):
        acc = _scaled_dot(x_ref, w_ref, s_ref, rs)
        if act == "silu":
            acc = _silu(acc)
        elif act == "sigmoid":
            acc = _sigmoid(acc)
        o_ref[rs, :] = acc.astype(o_ref.dtype)


def _mm_zgate_kernel(x_ref, w_ref, g_ref, o_ref, *, first_col):
    for rs in _row_chunks(o_ref):
        acc = _silu(_dot(x_ref[rs, :], w_ref[...]))
        for h in range(acc.shape[1] // LANES):
            sl = slice(h * LANES, (h + 1) * LANES)
            o_ref[rs, sl] = acc[:, sl] * g_ref[rs, first_col + h:first_col + h + 1]


def _mm_res_kernel(x_ref, w_ref, r_ref, h_ref, hb_ref, ssq_ref):
    parts = []
    for rs in _row_chunks(h_ref):
        h = r_ref[rs, :] + _dot(x_ref[rs, :], w_ref[...])
        h_ref[rs, :] = h
        parts.append(_write_bf16_copy(h, rs, hb_ref))
    _accumulate_ssq(parts, ssq_ref)


def _ple_kernel(n_ref, wg_ref, s_ref, p_ref, wp_ref, h_ref, o_ref, *rest):
    parts = []
    for rs in _row_chunks(o_ref):
        gate = _sigmoid(_scaled_dot(n_ref, wg_ref, s_ref, rs))
        emb = _dot(p_ref[rs, :].astype(BF16), wp_ref[...])
        h = h_ref[rs, :] + gate * emb
        o_ref[rs, :] = h
        if rest:
            parts.append(_write_bf16_copy(h, rs, rest[0]))
    if rest:
        _accumulate_ssq(parts, rest[1])


def _mm_rope_kernel(*refs, half, n_rope, plain_too, scaled):
    x_ref, w_ref = refs[0], refs[1]
    s_ref = refs[2] if scaled else None
    c_ref, s1_ref, s2_ref = refs[3:6] if scaled else refs[2:5]
    o_refs = refs[6:] if scaled else refs[5:]
    for rs in _row_chunks(o_refs[0]):
        acc = _scaled_dot(x_ref, w_ref, s_ref, rs)
        c, s1, s2 = c_ref[rs, :], s1_ref[rs, :], s2_ref[rs, :]
        for ch in range(acc.shape[1] // LANES):
            sl = slice(ch * LANES, (ch + 1) * LANES)
            t = acc[:, sl]
            o_refs[0][rs, sl] = (_rope_lanes(t, c, s1, s2, half) if ch < n_rope else t).astype(BF16)
            if plain_too:
                o_refs[1][rs, sl] = t.astype(BF16)


def _mm_kvb_kernel(x_ref, w_ref, s_ref, c_ref, s1_ref, s2_ref, o_ref, *, half):
    kvw = SWA_KVH * SWA_DH
    n_ch = kvw // LANES
    for rs in _row_chunks(o_ref):
        acc = _scaled_dot(x_ref, w_ref, s_ref, rs)
        c, s1, s2 = c_ref[rs, :], s1_ref[rs, :], s2_ref[rs, :]
        low = lax.broadcasted_iota(jnp.int32, (acc.shape[0], LANES), 1) < SWA_DH
        for kind in range(2):
            for ch in range(n_ch):
                t = acc[:, kind * kvw + ch * LANES: kind * kvw + (ch + 1) * LANES]
                if kind == 0:
                    t = _rope_lanes(t, c, s1, s2, half)
                ts = pltpu.roll(t, SWA_DH, 1)
                variants = (jnp.where(low, t, 0.0), jnp.where(low, 0.0, ts),
                            jnp.where(low, ts, 0.0), jnp.where(low, 0.0, t))
                for var, val in enumerate(variants):
                    col = (kind * 4 + var) * kvw + ch * LANES
                    o_ref[rs, col:col + LANES] = val.astype(BF16)


def _mm_tiles(m, n, tn_max=MM_TN):
    tm = min(MM_TM, m)
    tn = min(tn_max, n)
    assert m % tm == 0 and n % tn == 0, (m, n)
    return tm, tn


def _ssq_spec(tm):
    return pl.BlockSpec((tm, LANES), lambda i, j: (i, 0))


def _w_cols(w, cols, m):
    first, n = (0, w.shape[1]) if cols is None else cols
    tm, tn = _mm_tiles(m, n)
    assert first % tn == 0
    return first // tn, n, tm, tn


def matmul(x, w, name, act=None, out_dtype=F32, ssq=None, cols=None):
    m, k = x.shape
    jb, n, tm, tn = _w_cols(w, cols, m)
    scaled = ssq is not None
    return pl.pallas_call(
        functools.partial(_mm_kernel, act=act, scaled=scaled),
        grid=(m // tm, n // tn),
        in_specs=[pl.BlockSpec((tm, k), lambda i, j: (i, 0)),
                  pl.BlockSpec((k, tn), lambda i, j: (0, jb + j))] + ([_ssq_spec(tm)] if scaled else []),
        out_specs=pl.BlockSpec((tm, tn), lambda i, j: (i, j)),
        out_shape=jax.ShapeDtypeStruct((m, n), out_dtype),
        compiler_params=_cparams(("arbitrary", "arbitrary")),
        name=name,
    )(x, w, *([ssq] if scaled else []))


def matmul_zgate(x, w, gates, branch, name):
    m, k = x.shape
    n = w.shape[1]
    tm, tn = _mm_tiles(m, n)
    assert tn == NSA_HPG * NSA_DH
    return pl.pallas_call(
        functools.partial(_mm_zgate_kernel, first_col=branch * NSA_HPG),
        grid=(m // tm, n // tn),
        in_specs=[pl.BlockSpec((tm, k), lambda i, j: (i, 0)),
                  pl.BlockSpec((k, tn), lambda i, j: (0, j)),
                  pl.BlockSpec((tm, LANES), lambda i, j: (i, j))],
        out_specs=pl.BlockSpec((tm, tn), lambda i, j: (i, j)),
        out_shape=jax.ShapeDtypeStruct((m, n), F32),
        compiler_params=_cparams(("arbitrary", "arbitrary")),
        name=name,
    )(x, w, gates)


def matmul_rope(x, w, tabs, seq, head_dim, rope_cols, plain_too, name, ssq=None, cols=None):
    m, k = x.shape
    jb, n, tm, tn = _w_cols(w, cols, m)
    assert seq % tm == 0
    per_seq = seq // tm
    scaled = ssq is not None
    tab_spec = pl.BlockSpec((tm, LANES), lambda i, j: (i % per_seq, 0))
    n_out = 2 if plain_too else 1
    outs = pl.pallas_call(
        functools.partial(_mm_rope_kernel, half=head_dim // 8, n_rope=rope_cols // LANES,
                          plain_too=plain_too, scaled=scaled),
        grid=(m // tm, n // tn),
        in_specs=[pl.BlockSpec((tm, k), lambda i, j: (i, 0)),
                  pl.BlockSpec((k, tn), lambda i, j: (0, jb + j))]
        + ([_ssq_spec(tm)] if scaled else []) + [tab_spec] * 3,
        out_specs=[pl.BlockSpec((tm, tn), lambda i, j: (i, j))] * n_out,
        out_shape=[jax.ShapeDtypeStruct((m, n), BF16)] * n_out,
        compiler_params=_cparams(("arbitrary", "arbitrary")),
        name=name,
    )(x, w, *([ssq] if scaled else []), *tabs)
    return outs


def matmul_kvb(x, w, ssq, tabs, seq, name):
    m, k = x.shape
    n = w.shape[1]
    tm = min(MM_TM, m)
    assert seq % tm == 0 and n == 2 * SWA_KVH * SWA_DH
    per_seq = seq // tm
    tab_spec = pl.BlockSpec((tm, LANES), lambda i: (i % per_seq, 0))
    return pl.pallas_call(
        functools.partial(_mm_kvb_kernel, half=SWA_DH // 8),
        grid=(m // tm,),
        in_specs=[pl.BlockSpec((tm, k), lambda i: (i, 0)),
                  pl.BlockSpec((k, n), lambda i: (0, 0)),
                  pl.BlockSpec((tm, LANES), lambda i: (i, 0))] + [tab_spec] * 3,
        out_specs=pl.BlockSpec((tm, 4 * n), lambda i: (i, 0)),
        out_shape=jax.ShapeDtypeStruct((m, 4 * n), BF16),
        compiler_params=_cparams(("arbitrary",)),
        name=name,
    )(x, w, ssq, *tabs)


def matmul_residual(x, w, res, name):
    m, k = x.shape
    n = w.shape[1]
    tm, tn = _mm_tiles(m, n)
    tile = pl.BlockSpec((tm, tn), lambda i, j: (i, j))
    return pl.pallas_call(
        _mm_res_kernel,
        grid=(m // tm, n // tn),
        in_specs=[pl.BlockSpec((tm, k), lambda i, j: (i, 0)),
                  pl.BlockSpec((k, tn), lambda i, j: (0, j)),
                  tile],
        out_specs=[tile, tile, _ssq_spec(tm)],
        out_shape=[jax.ShapeDtypeStruct((m, n), F32), jax.ShapeDtypeStruct((m, n), BF16),
                   jax.ShapeDtypeStruct((m, LANES), F32)],
        compiler_params=_cparams(("arbitrary", "arbitrary")),
        name=name,
    )(x, w, res)


def ple_update(n, ssq, wg, p, wp, layer, h, emit_copy, name):
    m, k = n.shape
    d = wg.shape[2]
    kp = p.shape[2]
    tm, tn = _mm_tiles(m, d)
    tile = pl.BlockSpec((tm, tn), lambda i, j: (i, j))
    out_specs = [tile]
    out_shape = [jax.ShapeDtypeStruct((m, d), F32)]
    if emit_copy:
        out_specs += [tile, _ssq_spec(tm)]
        out_shape += [jax.ShapeDtypeStruct((m, d), BF16), jax.ShapeDtypeStruct((m, LANES), F32)]
    outs = pl.pallas_call(
        _ple_kernel,
        grid=(m // tm, d // tn),
        in_specs=[pl.BlockSpec((tm, k), lambda i, j: (i, 0)),
                  pl.BlockSpec((None, k, tn), lambda i, j: (layer, 0, j)),
                  _ssq_spec(tm),
                  pl.BlockSpec((None, tm, kp), lambda i, j: (layer, i, 0)),
                  pl.BlockSpec((None, kp, tn), lambda i, j: (layer, 0, j)),
                  tile],
        out_specs=out_specs,
        out_shape=out_shape,
        compiler_params=_cparams(("arbitrary", "arbitrary")),
        name=name,
    )(n, wg, ssq, p, wp, h)
    return outs if emit_copy else outs[0]


A_W_COLS = 512


def _wt_cast_kernel(wt_ref, o_ref, *, scale):
    o_ref[...] = (wt_ref[...].T * scale).astype(BF16)


def _cast_weight_cols(wt, first, width, name, scale=1.0):
    k = wt.shape[1]
    cols = min(A_W_COLS, width)
    assert width % cols == 0
    return pl.pallas_call(
        functools.partial(_wt_cast_kernel, scale=scale),
        grid=(width // cols,),
        in_specs=[pl.BlockSpec((pl.Element(cols), pl.Element(k)), lambda i: (pl.multiple_of(first + i * cols, SUBLANES), 0))],
        out_specs=pl.BlockSpec((k, cols), lambda i: (0, i)),
        out_shape=jax.ShapeDtypeStruct((k, width), BF16),
        compiler_params=_cparams(("arbitrary",)),
        name=name,
    )(wt)


def split_a_weights(wa):
    wt = wa[0].T
    c_q, c_cmp, c_kv = NSA_WIDTH, NSA_WIDTH + 2 * NSA_KVW, NSA_WIDTH + 6 * NSA_KVW
    c_z = c_kv + 3 * NSA_G * NSA_HPG
    segs = [(0, c_q), (c_q, c_cmp - c_q), (c_cmp, c_kv - c_cmp), (c_kv, LANES)]
    segs += [(c_z + br * NSA_WIDTH, NSA_WIDTH) for br in range(3)]
    scales = [NSA_QK_SCALE] + [1.0] * (len(segs) - 1)
    return [_cast_weight_cols(wt, first, width, "a_w_cast%d" % i, sc)
            for i, ((first, width), sc) in enumerate(zip(segs, scales))]


def _compress_kernel(x_ref, pos_ref, w1_ref, w2_ref, o_ref, *, n_cmp_pad):
    nc = n_cmp_pad
    lo = jnp.zeros((nc, CMP_HIDDEN), F32)
    hi = jnp.zeros((nc, CMP_HIDDEN), F32)
    for l in range(CMP_STRIDE):
        xl = x_ref[pl.ds(l, nc, stride=CMP_STRIDE), :]
        a = (xl + pos_ref[0, l:l + 1, :]).astype(BF16)
        b = (xl + pos_ref[0, CMP_STRIDE + l:CMP_STRIDE + l + 1, :]).astype(BF16)
        lo = lo + _dot(a, w1_ref[0, l * LANES:(l + 1) * LANES, :])
        hi = hi + _dot(b, w1_ref[0, (CMP_STRIDE + l) * LANES:(CMP_STRIDE + l + 1) * LANES, :])
    hid = lo + pltpu.roll(hi, nc - 1, 0)
    out = _dot(_silu(hid).astype(BF16), w2_ref[0])
    n_idx = lax.broadcasted_iota(jnp.int32, (nc, LANES), 0)
    o_ref[0, 0, 0] = jnp.where(n_idx < nc - 1, out, 0.0).astype(BF16)


def compress_kv(kcvc, pos, w1, w2, batch, seq):
    nc = seq // CMP_STRIDE
    return pl.pallas_call(
        functools.partial(_compress_kernel, n_cmp_pad=nc),
        grid=(batch, 2, NSA_G),
        in_specs=[pl.BlockSpec((seq, LANES), lambda b, s, g: (b, s * NSA_G + g)),
                  pl.BlockSpec((1, CMP_LEN, LANES), lambda b, s, g: (s, 0, 0)),
                  pl.BlockSpec((1, CMP_LEN * LANES, CMP_HIDDEN), lambda b, s, g: (s, 0, 0)),
                  pl.BlockSpec((1, CMP_HIDDEN, LANES), lambda b, s, g: (s, 0, 0))],
        out_specs=pl.BlockSpec((1, 1, 1, nc, LANES), lambda b, s, g: (b, s, g, 0, 0)),
        out_shape=jax.ShapeDtypeStruct((batch, 2, NSA_G, nc, LANES), BF16),
        compiler_params=_cparams(("arbitrary", "arbitrary", "arbitrary")),
        name="compress_kv",
    )(kcvc, pos, w1, w2)


def _nsa_kernel(qp_ref, qr_ref, z0_ref, z1_ref, z2_ref,
                kc_ref, vc_ref, ks_ref, vs_ref, kw_ref, vw_ref, et_ref, mt_ref,
                o_ref, qa_scr, m_scr, acc_scr, mix_scr, *, seq):
    tq, tk, hpg, gps = NSA_TQ, NSA_TK, NSA_HPG, NSA_GPS
    rows = hpg * tq
    hw = hpg * LANES
    nc = seq // CMP_STRIDE
    ns = seq // SEL_BLOCK
    n_top = min(SEL_TOPK, ns)
    nsp = mt_ref.shape[0]
    t0 = pl.program_id(2) * tq
    ones_k = jnp.ones((tk, LANES), BF16)

    t_col = t0 + lax.broadcasted_iota(jnp.int32, (tq, 1), 0)

    n_idx = lax.broadcasted_iota(jnp.int32, (1, nc), 1)
    valid = (n_idx * CMP_STRIDE + (CMP_LEN - 1) <= t_col) & (n_idx < nc - 1)
    bias_c = jnp.where(valid, 0.0, NEG_BIG)
    row_ok = t0 + lax.broadcasted_iota(jnp.int32, (tq, LANES), 0) >= CMP_LEN - 1
    lane_ok = t0 + lax.broadcasted_iota(jnp.int32, (ns, tq), 1) >= CMP_LEN - 1
    hq = tq // 2
    span = NSA_WINDOW + hq
    win = []
    for hh in range(2):
        kw0 = pl.multiple_of(jnp.maximum(t0 + hh * hq - NSA_WINDOW, 0), hq)
        t_half = t_col[hh * hq:(hh + 1) * hq]
        kpos_w = kw0 + lax.broadcasted_iota(jnp.int32, (1, span), 1)
        win.append((kw0, jnp.where((kpos_w <= t_half) & (kpos_w > t_half - NSA_WINDOW), 0.0, NEG_BIG)))
    j_io = lax.broadcasted_iota(jnp.int32, (ns, tq), 0)
    t_io = t0 + lax.broadcasted_iota(jnp.int32, (ns, tq), 1)
    dist = lax.shift_right_logical(t_io, int(math.log2(SEL_BLOCK))) - j_io
    forced = (j_io == 0) | ((dist >= 0) & (dist < SEL_LOCAL))

    for gi in range(gps):
        gl = slice(gi * LANES, (gi + 1) * LANES)
        kc = kc_ref[0, 0, gi]
        vc_aug = jnp.concatenate([vc_ref[0, 0, gi], jnp.ones((nc, LANES), BF16)], axis=1)
        q_all = jnp.concatenate([qp_ref[:, gi * hw + h * LANES:gi * hw + (h + 1) * LANES] for h in range(hpg)],
                                axis=0)
        sc = _dot_nt(q_all, kc) + jnp.concatenate([bias_c] * hpg, axis=0)
        m = jnp.max(sc, axis=-1, keepdims=True)
        pe = jnp.exp2(sc - m).astype(BF16)
        acc_c = _dot(pe, vc_aug)
        imp_t = _dot_nt(mt_ref[...], pe)
        imp_parts = []
        for h in range(hpg):
            sl = slice(gi * hw + h * LANES, gi * hw + (h + 1) * LANES)
            rs = slice(h * tq, (h + 1) * tq)
            oc = jnp.where(row_ok, acc_c[rs, 0:LANES] * (1.0 / acc_c[rs, LANES:2 * LANES]), 0.0)
            imp_parts.append(imp_t[0:ns, rs] * (1.0 / imp_t[nsp - 1:nsp, rs]))
            ows = []
            for hh, (kw0, bias_w) in enumerate(win):
                vw_aug = jnp.concatenate([vw_ref[pl.ds(kw0, span), gl], jnp.ones((span, LANES), BF16)], axis=1)
                sc = _dot_nt(qr_ref[hh * hq:(hh + 1) * hq, sl], kw_ref[pl.ds(kw0, span), gl]) + bias_w
                m = jnp.max(sc, axis=-1, keepdims=True)
                pe_w = jnp.exp2(sc - m).astype(BF16)
                acc = _dot(pe_w, vw_aug)
                ows.append(acc[:, 0:LANES] * (1.0 / acc[:, LANES:2 * LANES]))
            ow = jnp.concatenate(ows, axis=0)
            mix_scr[:, sl] = oc * z0_ref[:, sl] + ow * z2_ref[:, sl]
        while len(imp_parts) > 1:
            imp_parts = [imp_parts[i] + imp_parts[i + 1] for i in range(0, len(imp_parts), 2)]
        imp = jnp.where(lane_ok, imp_parts[0], 0.0)

        imp = jnp.where(forced, FORCE_SCORE, imp)
        imp = jnp.where(dist >= 0, imp, -1.0)
        sub = SUBLANES
        ranks = []
        for v in range(ns // sub):
            blk = imp[v * sub:(v + 1) * sub, :]
            j_v = v * sub + lax.broadcasted_iota(jnp.int32, (sub, tq), 0)
            r = jnp.zeros((sub, tq), F32)
            for k in range(ns):
                rk = imp[k:k + 1, :]
                if k < v * sub:
                    ahead = rk >= blk
                elif k >= (v + 1) * sub:
                    ahead = rk > blk
                else:
                    ahead = (rk > blk) | ((rk == blk) & (j_v > k))
                r = r + jnp.where(ahead, 1.0, 0.0)
            ranks.append(r)
        rank = jnp.concatenate(ranks, axis=0)
        not_sel_t = jnp.concatenate([jnp.where(rank < n_top, 0.0, 1.0), jnp.ones((nsp - ns, tq), F32)], axis=0)
        not_sel = not_sel_t.T.astype(BF16)

        for h in range(hpg):
            r0 = gi * rows + h * tq
            qa_scr[r0:r0 + tq, 0:LANES] = qr_ref[:, gi * hw + h * LANES:gi * hw + (h + 1) * LANES]
            qa_scr[r0:r0 + tq, LANES:2 * LANES] = not_sel

    m_scr[...] = jnp.full((gps * rows, LANES), NEG_BIG, F32)
    acc_scr[...] = jnp.zeros((gps * rows, 2 * LANES), F32)
    n_kt = lax.shift_right_logical(t0 + (tq + tk - 1), int(math.log2(tk)))

    def slc_tile(kt, causal, width=tk):
        k0 = pl.multiple_of(kt * tk, tk)
        e_tile = et_ref[pl.ds(k0, width), :]
        if causal:
            kpos = k0 + lax.broadcasted_iota(jnp.int32, (1, width), 1)
            bias = jnp.where(kpos <= t_col, 0.0, NEG_BIG)
        for gi in range(gps):
            gl = slice(gi * LANES, (gi + 1) * LANES)
            k_aug = jnp.concatenate([ks_ref[pl.ds(k0, width), gl], e_tile], axis=1)
            v_aug = jnp.concatenate([vs_ref[pl.ds(k0, width), gl], ones_k[0:width]], axis=1)
            for h in range(hpg):
                rs = slice(gi * rows + h * tq, gi * rows + (h + 1) * tq)
                sc = _dot_nt(qa_scr[rs, :], k_aug)
                if causal:
                    sc = sc + bias
                m_old = m_scr[rs, :]
                m_new = jnp.maximum(m_old, jnp.max(sc, axis=-1, keepdims=True))
                alpha = jnp.exp2(m_old - m_new)
                pe = jnp.exp2(sc - _lane_tile(m_new, width // LANES))
                acc_scr[rs, :] = _lane_tile(alpha, 2) * acc_scr[rs, :] + _dot(pe.astype(BF16), v_aug)
                m_scr[rs, :] = m_new

    def pair_body(i, carry):
        slc_tile(2 * i, causal=False)
        slc_tile(2 * i + 1, causal=False)
        return carry

    lax.fori_loop(0, lax.shift_right_logical(n_kt - 1, 1), pair_body, 0)
    odd = ((n_kt - 1) & 1) == 1
    short = (t0 & (tk - 1)) == 0
    for odd_case in (True, False):
        for short_case in (True, False):
            @pl.when((odd if odd_case else jnp.logical_not(odd)) & (short if short_case else jnp.logical_not(short)))
            def _(odd_case=odd_case, short_case=short_case):
                if odd_case:
                    slc_tile(n_kt - 2, causal=False)
                slc_tile(n_kt - 1, causal=True, width=tq if short_case else tk)

    for gi in range(gps):
        for h in range(hpg):
            rs = slice(gi * rows + h * tq, gi * rows + (h + 1) * tq)
            sl = slice(gi * hw + h * LANES, gi * hw + (h + 1) * LANES)
            o_slc = acc_scr[rs, 0:LANES] * (1.0 / acc_scr[rs, LANES:2 * LANES])
            o_ref[:, sl] = (mix_scr[:, sl] + o_slc * z1_ref[:, sl]).astype(BF16)


def nsa_attention(qp, qr, zg, cmp_kv, kv, e_t, cmp_to_sel_t, batch, seq):
    t = qp.shape[0]
    tq, gps = NSA_TQ, NSA_GPS
    assert seq % NSA_TK == 0 and NSA_TK % tq == 0 and seq >= NSA_WINDOW + tq and NSA_G % gps == 0
    qt_per_seq = seq // tq
    nc = seq // CMP_STRIDE
    hw = gps * NSA_HPG * NSA_DH
    n_gg = NSA_G // gps

    qz_spec = pl.BlockSpec((tq, hw), lambda b, g, i: (b * qt_per_seq + i, g))
    cmp_specs = [pl.BlockSpec((1, 1, gps, nc, LANES), functools.partial(
        lambda b, g, i, s: (b, s, g, 0, 0), s=s)) for s in range(2)]
    kv_specs = [pl.BlockSpec((seq, gps * LANES), functools.partial(
        lambda b, g, i, s: (b, s * n_gg + g), s=s)) for s in range(4)]
    const_specs = [pl.BlockSpec(e_t.shape, lambda b, g, i: (0, 0)),
                   pl.BlockSpec(cmp_to_sel_t.shape, lambda b, g, i: (0, 0))]
    rows = gps * NSA_HPG * tq
    return pl.pallas_call(
        functools.partial(_nsa_kernel, seq=seq),
        grid=(batch, n_gg, qt_per_seq),
        in_specs=[qz_spec] * 5 + cmp_specs + kv_specs + const_specs,
        out_specs=qz_spec,
        out_shape=jax.ShapeDtypeStruct((t, NSA_WIDTH), BF16),
        scratch_shapes=[pltpu.VMEM((rows, 2 * LANES), BF16),
                        pltpu.VMEM((rows, LANES), F32),
                        pltpu.VMEM((rows, 2 * LANES), F32),
                        pltpu.VMEM((tq, hw), F32)],
        compiler_params=_cparams(("arbitrary", "arbitrary", "arbitrary")),
        name="nsa_attention",
    )(qp, qr, *zg, cmp_kv, cmp_kv, kv, kv, kv, kv, e_t, cmp_to_sel_t)


def _swa_kernel(sink_ref, q_ref, z_ref, *rest):
    kv_refs, o_ref = rest[:-1], rest[-1]
    tq = ATT_TQ
    n_pair = SWA_KVH // 2
    per_kv = SWA_HPG * SWA_DH // LANES
    t0 = pl.program_id(1) * tq
    ones_k = jnp.ones((2 * tq, LANES), BF16)

    def window(kind, var, pair):
        base = (kind * 4 + var) * 2
        sl = slice(pair * LANES, (pair + 1) * LANES)
        return jnp.concatenate([kv_refs[base][:, sl], kv_refs[base + 1][:, sl]], axis=0)

    t_col = t0 + lax.broadcasted_iota(jnp.int32, (tq, 1), 0)
    kpos = t0 - tq + lax.broadcasted_iota(jnp.int32, (1, 2 * tq), 1)
    ok = (kpos <= t_col) & (kpos > t_col - SWA_WINDOW) & (kpos >= 0)
    bias = jnp.where(ok, 0.0, NEG_BIG)
    col0 = lax.broadcasted_iota(jnp.int32, (tq, LANES), 1) == 0
    keep = jnp.where(lax.broadcasted_iota(jnp.int32, (2 * tq, LANES), 0) == 0, 0.0, 1.0).astype(BF16)

    for pair in range(n_pair):
        for j in range(2):
            k_var = [window(0, 2 * j + par, pair) * keep for par in range(2)]
            v_var = [jnp.concatenate([window(1, 2 * j + par, pair) * keep, ones_k], axis=1) for par in range(2)]
            for c in range(per_kv):
                col = ((pair * 2 + j) * per_kv + c) * LANES
                qc = q_ref[:, col:col + LANES]
                o_chunk = None
                for par in range(2):
                    head = (pair * 2 + j) * SWA_HPG + 2 * c + par
                    sink_raw = sink_ref[head] * LOG2E
                    bias_h = jnp.concatenate([jnp.where(col0, sink_raw, bias[:, 0:LANES]), bias[:, LANES:]], axis=1)
                    sc = _dot_nt(qc, k_var[par]) + bias_h
                    m = jnp.max(sc, axis=-1, keepdims=True)
                    e = jnp.exp2(sc - m)
                    acc = _dot(e.astype(BF16), v_var[par])
                    o_par = acc[:, 0:LANES] * (1.0 / acc[:, LANES:2 * LANES])
                    o_chunk = o_par if o_chunk is None else o_chunk + o_par
                o_ref[:, col:col + LANES] = (o_chunk * z_ref[:, col:col + LANES]).astype(BF16)


def swa_attention(q, zs, kvb, sinks, batch, seq):
    t = q.shape[0]
    tq = ATT_TQ
    qt_per_seq = seq // tq
    kvw = SWA_KVH * SWA_DH

    def row(b, i):
        return b * qt_per_seq + i

    def prev_row(b, i):
        return b * qt_per_seq + jnp.maximum(i - 1, 0)

    qz_spec = pl.BlockSpec((tq, SWA_WIDTH), lambda b, i: (row(b, i), 0))
    kv_specs = []
    for cb in range(8):
        kv_specs.append(pl.BlockSpec((tq, kvw), functools.partial(lambda b, i, cb: (prev_row(b, i), cb), cb=cb)))
        kv_specs.append(pl.BlockSpec((tq, kvw), functools.partial(lambda b, i, cb: (row(b, i), cb), cb=cb)))
    return pl.pallas_call(
        _swa_kernel,
        grid=(batch, qt_per_seq),
        in_specs=[pl.BlockSpec(memory_space=pltpu.SMEM), qz_spec, qz_spec] + kv_specs,
        out_specs=qz_spec,
        out_shape=jax.ShapeDtypeStruct((t, SWA_WIDTH), BF16),
        compiler_params=_cparams(("arbitrary", "arbitrary")),
        name="swa_attention",
    )(sinks, q, zs, *([kvb] * 16))


def _selection_constants(seq):
    nc = seq // CMP_STRIDE
    ns = seq // SEL_BLOCK
    nsp = -(-ns // LANES) * LANES
    e_t = np.zeros((seq, nsp), np.float32)
    e_t[np.arange(seq), np.arange(seq) // SEL_BLOCK] = NEG_BIG
    c0 = np.arange(nc - 1) * CMP_STRIDE
    s0 = np.arange(ns) * SEL_BLOCK
    overlap = (c0[:, None] < s0[None, :] + SEL_BLOCK) & (c0[:, None] + CMP_LEN > s0[None, :])
    assert ns < nsp
    cmp_to_sel_t = np.zeros((nsp, nc), np.float32)
    cmp_to_sel_t[:ns, :nc - 1] = overlap.T
    cmp_to_sel_t[nsp - 1, :] = 1.0
    return jnp.asarray(e_t, BF16), jnp.asarray(cmp_to_sel_t, BF16)


def kernel(x, p, a_norm, a_w_in, a_w_out, a_cmp_pos_k, a_cmp_w1_k, a_cmp_w2_k, a_cmp_pos_v, a_cmp_w1_v,
           a_cmp_w2_v, kv_norm, w_kv, b_norm, b_w_in, b_w_out, b_sinks, ple_norm, ple_gate_w, ple_proj,
           final_norm):
    batch, seq, d = x.shape
    assert d == D_MODEL and p.shape[0] == 2 and a_w_in.shape[0] == 1 and b_w_in.shape[0] == 1
    t = batch * seq
    xf = x.reshape(t, d)

    n_gate = 3 * NSA_G * NSA_HPG
    w_q, w_cmp, w_kvs, w_gate, *w_z = split_a_weights(a_w_in)
    wg = w_gate[:, :n_gate].reshape(d, 3, NSA_G, NSA_HPG).transpose(0, 2, 1, 3)
    wg = wg.reshape(d, NSA_G, 3 * NSA_HPG)
    wg = jnp.pad(wg, ((0, 0), (0, 0), (0, LANES - 3 * NSA_HPG))).reshape(d, NSA_G * LANES)
    pos_kv = jnp.stack([a_cmp_pos_k[0], a_cmp_pos_v[0]])
    w1_kv = jnp.stack([a_cmp_w1_k[0], a_cmp_w1_v[0]]).astype(BF16)
    w2_kv = jnp.stack([a_cmp_w2_k[0], a_cmp_w2_v[0]]).astype(BF16)
    col_scale = jnp.concatenate([jnp.full((SWA_WIDTH,), SWA_QK_SCALE, F32), jnp.ones((SWA_WIDTH,), F32)])
    wb = (b_norm[0][:, None] * b_w_in[0] * col_scale[None, :]).astype(BF16)
    w_kvb = (kv_norm[:, None] * w_kv).astype(BF16)
    ple_wg = (ple_norm[:, :, None] * ple_gate_w).astype(BF16)
    ple_wp = ple_proj.astype(BF16)
    tabs_a = _rope_tables(seq, NSA_DH)
    tabs_b = _rope_tables(seq, SWA_DH)
    e_t, cmp_to_sel_t = _selection_constants(seq)

    (n_a,) = rms_norm_rows(xf, a_norm, BF16)
    q_rot, q_plain = matmul_rope(n_a, w_q, tabs_a, seq, NSA_DH, MM_TN, True, "a_q_proj")
    (kv_a,) = matmul_rope(n_a, w_kvs, tabs_a, seq, NSA_DH, NSA_KVW, False, "a_kv_proj")
    kcvc = matmul(n_a, w_cmp, "a_cmp_proj")
    gates = matmul(n_a, wg, "a_gate_proj", act="sigmoid")
    zg = [matmul_zgate(n_a, w_z[br], gates, br, "a_z_proj%d" % br) for br in range(3)]
    cmp_kv = compress_kv(kcvc, pos_kv, w1_kv, w2_kv, batch, seq)
    mixed = nsa_attention(q_plain, q_rot, zg, cmp_kv, kv_a, e_t, cmp_to_sel_t, batch, seq)
    h, hb, ssq = matmul_residual(mixed, a_w_out[0].astype(BF16), xf, "a_out_proj")
    pf = p.reshape(2, t, PLE_DIM)
    h, hb, ssq = ple_update(hb, ssq, ple_wg, pf, ple_wp, 0, h, True, "ple0")

    kvb = matmul_kvb(hb, w_kvb, ssq, tabs_b, seq, "kv_proj")
    (q_b,) = matmul_rope(hb, wb, tabs_b, seq, SWA_DH, MM_TN, False, "b_q_proj", ssq=ssq, cols=(0, SWA_WIDTH))
    zs_b = matmul(hb, wb, "b_z_proj", act="silu", ssq=ssq, cols=(SWA_WIDTH, SWA_WIDTH))
    ob = swa_attention(q_b, zs_b, kvb, b_sinks[0], batch, seq)
    h, hb, ssq = matmul_residual(ob, b_w_out[0].astype(BF16), h, "b_out_proj")
    h = ple_update(hb, ssq, ple_wg, pf, ple_wp, 1, h, False, "ple1")

    (out,) = rms_norm_rows(h, final_norm[None, :], F32)
    return out.reshape(batch, seq, d)
```
